```python
import jax, jax.numpy as jnp
from jax import lax
import numpy as np

D_MODEL = 1024
BATCH = 32
SEQ = 2048
DEPTH = 1

GLA_HEADS = 4
GLA_V_W = D_MODEL // 2
GLA_DV = GLA_V_W // GLA_HEADS
GLA_DK = GLA_DV // 2
GLA_QK_W = GLA_HEADS * GLA_DK
GLA_GATE_RANK = 16
GLA_LOGIT_NORM = 16.0
GLA_CHUNK = 64
RWKV_HEAD = 64
RWKV_W = D_MODEL // 2
RWKV_HEADS = RWKV_W // RWKV_HEAD
RWKV_DECAY_LORA = 64
RWKV_AAA_LORA = 64
RWKV_GATE_LORA = 128
RWKV_GN_EPS = RWKV_HEAD * 1e-5
GLA_SPLITS = (GLA_QK_W, GLA_QK_W, GLA_V_W, GLA_V_W, GLA_GATE_RANK, GLA_GATE_RANK)
RWKV_SPLITS = (RWKV_W, RWKV_W, RWKV_W, RWKV_DECAY_LORA, RWKV_AAA_LORA, RWKV_GATE_LORA)
GLA_PROJ_W = sum(GLA_SPLITS)
RWKV_PROJ_W = sum(RWKV_SPLITS)
GATE_PROJ_W = 2 * D_MODEL
N_PROJ = GLA_PROJ_W + RWKV_PROJ_W + GATE_PROJ_W
D_FF = ((8 * D_MODEL // 3) + 63) // 64 * 64
CONV_W = 3
NORM_EPS = 1e-6
HEAD_NORM_EPS = 1e-5

kernel_name = "bidir_gla_rwkv7_gated_hybrid"


def _rmsnorm(x, g):
    xf = x.astype(jnp.float32)
    y = xf * lax.rsqrt(jnp.mean(xf * xf, axis=-1, keepdims=True) + NORM_EPS)
    return (y * g.astype(jnp.float32)).astype(x.dtype)


def _shift_prev(u):
    return jnp.pad(u[:, :-1], ((0, 0), (1, 0), (0, 0)))


def _shift_next(u):
    return jnp.pad(u[:, 1:], ((0, 0), (0, 1), (0, 0)))


def _split(t, sizes):
    return jnp.split(t, np.cumsum(sizes)[:-1].tolist(), axis=-1)


def _gla_chunked(q, k, v, log_a):
    f32 = jnp.float32
    B_, T, H, K = q.shape
    V = v.shape[-1]
    C = GLA_CHUNK
    n = T // C
    q = q.astype(f32).reshape(B_, n, C, H, K)
    k = k.astype(f32).reshape(B_, n, C, H, K)
    v = v.astype(f32).reshape(B_, n, C, H, V)
    b = jnp.cumsum(log_a.astype(f32).reshape(B_, n, C, H, K), axis=2)
    b_ref = b[:, :, C // 2:C // 2 + 1]
    qi = q * jnp.exp(b - b_ref)
    ki = k * jnp.exp(b_ref - b)
    A = jnp.einsum('bnchk,bnshk->bnhcs', qi, ki)
    A = jnp.where(jnp.tril(jnp.ones((C, C), dtype=bool)), A, 0.0)
    o_intra = jnp.einsum('bnhcs,bnshv->bnchv', A, v)
    b_last = b[:, :, -1:]
    kv = jnp.einsum('bnchk,bnchv->bnhkv', k * jnp.exp(b_last - b), v)
    decay = jnp.exp(b_last[:, :, 0])

    def step(S, inp):
        d, u = inp
        return S * d[..., None] + u, S

    _, S_prev = lax.scan(step, jnp.zeros((B_, H, K, V), f32),
                         (jnp.moveaxis(decay, 1, 0), jnp.moveaxis(kv, 1, 0)))
    S_prev = jnp.moveaxis(S_prev, 0, 1)
    o_inter = jnp.einsum('bnchk,bnhkv->bnchv', q * jnp.exp(b), S_prev)
    return (o_intra + o_inter).reshape(B_, T, H, V)


def _gla_branch(p, wa2_f, ba_f, wa2_b, ba_b, norm_g, proj):
    f32 = jnp.float32
    B_, T, _ = p.shape
    q, k, v, og, af, ab = _split(p, GLA_SPLITS)
    q = q.reshape(B_, T, GLA_HEADS, GLA_DK) * (GLA_DK ** -0.5)
    k = k.reshape(B_, T, GLA_HEADS, GLA_DK)
    v = v.reshape(B_, T, GLA_HEADS, GLA_DV)
    la_f = (jax.nn.log_sigmoid((af @ wa2_f + ba_f).astype(f32)) / GLA_LOGIT_NORM).reshape(B_, T, GLA_HEADS, GLA_DK)
    la_b = (jax.nn.log_sigmoid((ab @ wa2_b + ba_b).astype(f32)) / GLA_LOGIT_NORM).reshape(B_, T, GLA_HEADS, GLA_DK)
    flip = lambda t: jnp.flip(t, axis=1)
    o = _gla_chunked(q, k, v, la_f) + flip(_gla_chunked(flip(q), flip(k), flip(v), flip(la_b)))
    o = o * lax.rsqrt(jnp.mean(o * o, axis=-1, keepdims=True) + HEAD_NORM_EPS)
    o = (o.reshape(B_, T, GLA_V_W) * norm_g.astype(f32)).astype(p.dtype)
    o = o * jax.nn.silu(og)
    return o @ proj


def _rwkv7_step(S, inp):
    r, w, k, v, a, b = inp
    sa = jnp.einsum('bhvk,bhk->bhv', S, a)
    S = S * w[:, :, None, :] + sa[..., None] * b[:, :, None, :] + v[..., None] * k[:, :, None, :]
    y = jnp.einsum('bhvk,bhk->bhv', S, r)
    return S, y


def _rwkv_branch(p, mu_prev, mu_next, w0_f, w2_f, w0_b, w2_b, a0, a2, g2, k_k, k_a, r_k, ln_w, ln_b, proj):
    f32 = jnp.float32
    B_, T, _ = p.shape
    s = p + mu_prev * (_shift_prev(p) - p) + mu_next * (_shift_next(p) - p)
    r, k, v, wl, al, gl = _split(s, RWKV_SPLITS)
    tw = jnp.tanh(wl)

    def decay(w0, w2):
        w = -jax.nn.softplus(-(w0 + tw @ w2).astype(f32)) - 0.5
        return jnp.exp(-jnp.exp(w))

    a = jax.nn.sigmoid(a0 + al @ a2)
    g = jax.nn.sigmoid(gl) @ g2
    heads = lambda t: t.reshape(B_, T, RWKV_HEADS, RWKV_HEAD).astype(f32)
    kk = heads(k * k_k)
    kk = kk / jnp.maximum(jnp.sqrt(jnp.sum(kk * kk, axis=-1, keepdims=True)), 1e-12)
    k = k * (1.0 + (a - 1.0) * k_a)
    rh, kh, vh, ah = heads(r), heads(k), heads(v), heads(a)
    tm = lambda t: jnp.moveaxis(t, 1, 0)
    r_s, k_s, v_s, a_s, b_s = tm(rh), tm(kh), tm(vh), tm(-kk), tm(kk * ah)
    S0 = jnp.zeros((B_, RWKV_HEADS, RWKV_HEAD, RWKV_HEAD), f32)
    _, y_f = lax.scan(_rwkv7_step, S0, (r_s, tm(heads(decay(w0_f, w2_f))), k_s, v_s, a_s, b_s))
    _, y_b = lax.scan(_rwkv7_step, S0, (r_s, tm(heads(decay(w0_b, w2_b))), k_s, v_s, a_s, b_s), reverse=True)
    y = jnp.moveaxis(y_f + y_b, 0, 1)
    mu = jnp.mean(y, axis=-1, keepdims=True)
    var = jnp.mean(jnp.square(y - mu), axis=-1, keepdims=True)
    y = ((y - mu) * lax.rsqrt(var + RWKV_GN_EPS)).reshape(B_, T, RWKV_W) * ln_w.astype(f32) + ln_b.astype(f32)
    bonus = (jnp.sum(rh * kh * r_k.astype(f32), axis=-1, keepdims=True) * vh).reshape(B_, T, RWKV_W)
    o = (y + bonus).astype(p.dtype) * g
    return o @ proj


def setup_inputs(seed: int = 0) -> dict:
    key = jax.random.key(seed)
    ks = jax.random.split(key, 32)
    L, D = DEPTH, D_MODEL
    f32 = jnp.float32
    nrm = lambda k, shape, scale: jax.random.normal(k, shape, f32) * scale
    uni = lambda k, shape: jax.random.uniform(k, shape, f32, 0.0, 0.5)
    centre = jnp.array([0.0, 1.0, 0.0], f32)[None, :, None]
    return {
        "x": nrm(ks[0], (BATCH, SEQ, D), 1.0),
        "norm1_g": 1.0 + nrm(ks[1], (L, D), 0.02),
        "w_in": nrm(ks[2], (L, D, N_PROJ), D ** -0.5),
        "gla_wa2_f": nrm(ks[3], (L, GLA_GATE_RANK, GLA_QK_W), GLA_GATE_RANK ** -0.5),
        "gla_ba_f": 1.0 + nrm(ks[4], (L, GLA_QK_W), 0.5),
        "gla_wa2_b": nrm(ks[5], (L, GLA_GATE_RANK, GLA_QK_W), GLA_GATE_RANK ** -0.5),
        "gla_ba_b": 1.0 + nrm(ks[6], (L, GLA_QK_W), 0.5),
        "gla_norm_g": 1.0 + nrm(ks[7], (L, GLA_V_W), 0.02),
        "gla_proj": nrm(ks[8], (L, GLA_V_W, D), GLA_V_W ** -0.5),
        "rwkv_mu_prev": uni(ks[9], (L, RWKV_PROJ_W)),
        "rwkv_mu_next": uni(ks[10], (L, RWKV_PROJ_W)),
        "rwkv_w0_f": -1.0 + nrm(ks[11], (L, RWKV_W), 0.5),
        "rwkv_w2_f": nrm(ks[12], (L, RWKV_DECAY_LORA, RWKV_W), RWKV_DECAY_LORA ** -0.5),
        "rwkv_w0_b": -1.0 + nrm(ks[13], (L, RWKV_W), 0.5),
        "rwkv_w2_b": nrm(ks[14], (L, RWKV_DECAY_LORA, RWKV_W), RWKV_DECAY_LORA ** -0.5),
        "rwkv_a0": nrm(ks[15], (L, RWKV_W), 0.1),
        "rwkv_a2": nrm(ks[16], (L, RWKV_AAA_LORA, RWKV_W), RWKV_AAA_LORA ** -0.5),
        "rwkv_g2": nrm(ks[17], (L, RWKV_GATE_LORA, RWKV_W), RWKV_GATE_LORA ** -0.5),
        "rwkv_k_k": 0.85 + nrm(ks[18], (L, RWKV_W), 0.05),
        "rwkv_k_a": 1.0 + nrm(ks[19], (L, RWKV_W), 0.05),
        "rwkv_r_k": nrm(ks[20], (L, RWKV_HEADS, RWKV_HEAD), 0.1),
        "rwkv_ln_w": 1.0 + nrm(ks[21], (L, RWKV_W), 0.02),
        "rwkv_ln_b": nrm(ks[22], (L, RWKV_W), 0.02),
        "rwkv_proj": nrm(ks[23], (L, RWKV_W, D), RWKV_W ** -0.5),
        "w_out": nrm(ks[24], (L, D, D), D ** -0.5),
        "norm2_g": 1.0 + nrm(ks[25], (L, D), 0.02),
        "ffn_up": nrm(ks[26], (L, D, 2 * D_FF), D ** -0.5),
        "ffn_conv_w": centre + nrm(ks[27], (L, CONV_W, 2 * D_FF), 0.2),
        "ffn_conv_b": nrm(ks[28], (L, 2 * D_FF), 0.02),
        "ffn_down": nrm(ks[29], (L, D_FF, D), D_FF ** -0.5),
        "norm_f_g": 1.0 + nrm(ks[30], (D,), 0.02),
    }


def reference(x, norm1_g, w_in, gla_wa2_f, gla_ba_f, gla_wa2_b, gla_ba_b, gla_norm_g, gla_proj,
              rwkv_mu_prev, rwkv_mu_next, rwkv_w0_f, rwkv_w2_f, rwkv_w0_b, rwkv_w2_b, rwkv_a0, rwkv_a2,
              rwkv_g2, rwkv_k_k, rwkv_k_a, rwkv_r_k, rwkv_ln_w, rwkv_ln_b, rwkv_proj, w_out,
              norm2_g, ffn_up, ffn_conv_w, ffn_conv_b, ffn_down, norm_f_g):
    for l in range(DEPTH):
        h = _rmsnorm(x, norm1_g[l])
        p = h @ w_in[l]
        p_gla, p_rwkv, p_gate = jnp.split(p, [GLA_PROJ_W, GLA_PROJ_W + RWKV_PROJ_W], axis=-1)
        y_a = _gla_branch(p_gla, gla_wa2_f[l], gla_ba_f[l], gla_wa2_b[l], gla_ba_b[l],
                          gla_norm_g[l], gla_proj[l])
        y_b = _rwkv_branch(p_rwkv, rwkv_mu_prev[l], rwkv_mu_next[l], rwkv_w0_f[l], rwkv_w2_f[l],
                           rwkv_w0_b[l], rwkv_w2_b[l], rwkv_a0[l], rwkv_a2[l], rwkv_g2[l],
                           rwkv_k_k[l], rwkv_k_a[l], rwkv_r_k[l], rwkv_ln_w[l], rwkv_ln_b[l], rwkv_proj[l])
        gate_a, gate_b = jnp.split(p_gate, 2, axis=-1)
        merged = jax.nn.sigmoid(gate_a) * y_a + jax.nn.sigmoid(gate_b) * y_b
        x = x + merged @ w_out[l]
        h2 = _rmsnorm(x, norm2_g[l])
        u = h2 @ ffn_up[l]
        cw = ffn_conv_w[l]
        u = cw[0] * _shift_prev(u) + cw[1] * u + cw[2] * _shift_next(u) + ffn_conv_b[l]
        u_gate, u_val = jnp.split(u, 2, axis=-1)
        x = x + (jax.nn.silu(u_gate) * u_val) @ ffn_down[l]
    return _rmsnorm(x, norm_f_g)
```

```python
import functools

import jax
import jax.numpy as jnp
from jax import lax
from jax.experimental import pallas as pl
from jax.experimental.pallas import tpu as pltpu

F32 = jnp.float32
BF16 = jnp.bfloat16

D_MODEL = 1024
GLA_HEADS = 4
GLA_DK = 64
GLA_DV = 128
GLA_QK_W = GLA_HEADS * GLA_DK
GLA_V_W = GLA_HEADS * GLA_DV
GLA_GATE_RANK = 16
GLA_LOGIT_NORM = 16.0
CHUNK = 64
RWKV_HEAD = 64
RWKV_W = 512
RWKV_HEADS = RWKV_W // RWKV_HEAD
RWKV_DECAY_LORA = 64
RWKV_AAA_LORA = 64
RWKV_GATE_LORA = 128
RWKV_GN_EPS = RWKV_HEAD * 1e-5
D_FF = 2752
NORM_EPS = 1e-6
HEAD_NORM_EPS = 1e-5

COL_GA, COL_GB = 0, 1024
COL_QK, COL_V, COL_OG = 2048, 2560, 3072
COL_R, COL_RK, COL_RV = 3584, 4096, 4608
COL_RS, COL_GS = 5120, 5376
P_COLS = 5632
HALO = 16

FF_CHUNK = 256
FF_PAD = 2816
FF_NCHUNK = FF_PAD // FF_CHUNK

VMEM_LIMIT = 56 * 1024 * 1024
ROW_TILE = 1024
FFN_ROW_TILE = 512
SEQ_TILE = 256
RWKV_GROUP = 4


def _sigmoid(x):
    return 1.0 / (1.0 + jnp.exp(-x))


def _softplus(z):
    return jnp.maximum(z, 0.0) + jnp.log(1.0 + jnp.exp(-jnp.abs(z)))


def _split3(x):
    hi = x.astype(BF16)
    r1 = x - hi.astype(F32)
    mid = r1.astype(BF16)
    lo = (r1 - mid.astype(F32)).astype(BF16)
    return hi, mid, lo


def _dot(a, b):
    return jnp.dot(a, b, preferred_element_type=F32)


def _dot_nt(a, b):
    return lax.dot_general(a, b, (((1,), (1,)), ((), ())), preferred_element_type=F32)


def _dot_tn(a, b):
    return lax.dot_general(a, b, (((0,), (0,)), ((), ())), preferred_element_type=F32)


def _dot_exact_lhs(sel, x):
    hi, mid, lo = _split3(x)
    s = sel.astype(BF16)
    return _dot(s, hi) + _dot(s, mid) + _dot(s, lo)


def _dot_exact_rhs(x, sel):
    hi, mid, lo = _split3(x)
    s = sel.astype(BF16)
    return _dot(hi, s) + _dot(mid, s) + _dot(lo, s)


def _dot_x3(a, b):
    ah = a.astype(BF16)
    al = (a - ah.astype(F32)).astype(BF16)
    bh = b.astype(BF16)
    bl = (b - bh.astype(F32)).astype(BF16)
    return _dot(ah, bh) + _dot(al, bh) + _dot(ah, bl)


def _rmsnorm(x, g):
    ms = jnp.mean(x * x, axis=-1, keepdims=True)
    return x * lax.rsqrt(ms + NORM_EPS) * g


def _iota2(shape, dim):
    return lax.broadcasted_iota(jnp.int32, shape, dim)


def _div(x, size):
    return lax.shift_right_logical(x, size.bit_length() - 1)


def _mod(x, size):
    return lax.bitwise_and(x, size - 1)


def _chunk_tri(tb, reverse):
    row = _iota2((tb, tb), 0)
    col = _iota2((tb, tb), 1)
    same = _div(row, CHUNK) == _div(col, CHUNK)
    tri = jnp.where(same & ((col >= row) if reverse else (col <= row)), 1.0, 0.0)
    return tri, jnp.where(same, 1.0, 0.0)


def _blockdiag(z, head_masks):
    return jnp.concatenate([jnp.where(m, z, 0.0) for m in head_masks], axis=0).astype(BF16)


def _inproj_kernel(xp_ref, x_ref, xn_ref, g_ref, w_ref, mup_ref, mun_ref, o_ref,
                   h_ref, p_ref, *, tm, tiles_per_seq):
    i = pl.program_id(0)
    j = pl.program_id(1)

    @pl.when(j == 0)
    def _():
        pos = lax.rem(i, tiles_per_seq)
        keep_prev = jnp.where(pos == 0, 0.0, 1.0)
        keep_next = jnp.where(pos == tiles_per_seq - 1, 0.0, 1.0)
        g = g_ref[...]
        h_ref[0:HALO, :] = _rmsnorm(xp_ref[...] * keep_prev, g).astype(BF16)
        h_ref[HALO:HALO + tm, :] = _rmsnorm(x_ref[...], g).astype(BF16)
        h_ref[HALO + tm:HALO + tm + HALO, :] = _rmsnorm(xn_ref[...] * keep_next, g).astype(BF16)

    p_ref[...] = _dot(h_ref[...], w_ref[...])
    p = p_ref[HALO:HALO + tm, :]
    pp = p_ref[HALO - 1:HALO - 1 + tm, :]
    pn = p_ref[HALO + 1:HALO + 1 + tm, :]
    o_ref[...] = p + mup_ref[...] * (pp - p) + mun_ref[...] * (pn - p)


def _inproj(x2, g, w_packed, mu_prev, mu_next, seq_len):
    n = x2.shape[0]
    tm = min(ROW_TILE, seq_len)
    tn = 512
    tiles_per_seq = seq_len // tm
    hb = tm // HALO
    nhb = n // HALO
    kern = functools.partial(_inproj_kernel, tm=tm, tiles_per_seq=tiles_per_seq)
    return pl.pallas_call(
        kern,
        grid=(n // tm, P_COLS // tn),
        in_specs=[
            pl.BlockSpec((HALO, D_MODEL), lambda i, j: (jnp.maximum(i * hb - 1, 0), 0)),
            pl.BlockSpec((tm, D_MODEL), lambda i, j: (i, 0)),
            pl.BlockSpec((HALO, D_MODEL), lambda i, j: (jnp.minimum((i + 1) * hb, nhb - 1), 0)),
            pl.BlockSpec((1, D_MODEL), lambda i, j: (0, 0)),
            pl.BlockSpec((D_MODEL, tn), lambda i, j: (0, j)),
            pl.BlockSpec((1, tn), lambda i, j: (0, j)),
            pl.BlockSpec((1, tn), lambda i, j: (0, j)),
        ],
        out_specs=pl.BlockSpec((tm, tn), lambda i, j: (i, j)),
        out_shape=jax.ShapeDtypeStruct((n, P_COLS), F32),
        scratch_shapes=[
            pltpu.VMEM((tm + 2 * HALO, D_MODEL), BF16),
            pltpu.VMEM((tm + 2 * HALO, tn), F32),
        ],
        compiler_params=pltpu.CompilerParams(
            dimension_semantics=("arbitrary", "arbitrary"),
            vmem_limit_bytes=VMEM_LIMIT),
        name="inproj",
    )(x2, x2, x2, g, w_packed, mu_prev, mu_next)


def _gla_kernel(*refs, reverse, tb, final):
    if final:
        (qk_ref, v_ref, gs_ref, wa2_ref, ba_ref, of_ref, og_ref, ng_ref,
         o_ref, st_ref, b_ref, acc_ref) = refs
    else:
        qk_ref, v_ref, gs_ref, wa2_ref, ba_ref, o_ref, st_ref, b_ref = refs
    j = pl.program_id(1)
    nc = tb // CHUNK

    @pl.when(j == 0)
    def _():
        st_ref[...] = jnp.zeros_like(st_ref)

    logit = _dot_x3(gs_ref[...], wa2_ref[...]) + ba_ref[...]
    la = (jnp.minimum(logit, 0.0) - jnp.log(1.0 + jnp.exp(-jnp.abs(logit)))) * (1.0 / GLA_LOGIT_NORM)
    tri, _ = _chunk_tri(tb, reverse)
    b_ref[...] = _dot_exact_lhs(tri, la)

    lane_k = _div(_iota2((CHUNK, GLA_QK_W), 1), GLA_DK)
    lane_v = _div(_iota2((CHUNK, GLA_V_W), 1), GLA_DV)
    k_masks = [lane_k == h for h in range(GLA_HEADS)]
    v_masks = [lane_v == h for h in range(GLA_HEADS)]
    t_idx = _iota2((CHUNK, GLA_QK_W), 0)
    s_idx = _mod(_iota2((CHUNK, GLA_QK_W), 1), CHUNK)
    causal = (s_idx >= t_idx) if reverse else (s_idx <= t_idx)
    st_mask = (_div(_iota2((GLA_V_W, GLA_QK_W), 0), GLA_DV)
               == _div(_iota2((GLA_V_W, GLA_QK_W), 1), GLA_DK))
    i_ref = (CHUNK - 1 - CHUNK // 2) if reverse else CHUNK // 2
    i_last = 0 if reverse else CHUNK - 1

    def chunk_body(ci, carry):
        c = (nc - 1 - ci) if reverse else ci
        r0 = pl.multiple_of(c * CHUNK, CHUNK)
        rows = pl.ds(r0, CHUNK)
        b = b_ref[rows, :]
        b_mid = b[i_ref:i_ref + 1, :]
        b_last = b[i_last:i_last + 1, :]
        q = qk_ref[rows, 0:GLA_QK_W] * (GLA_DK ** -0.5)
        k = qk_ref[rows, GLA_QK_W:2 * GLA_QK_W]
        v = v_ref[rows, :]
        qi = (q * jnp.exp(b - b_mid)).astype(BF16)
        kd = (k * jnp.exp(b_last - b)).astype(BF16)
        qe = (q * jnp.exp(b)).astype(BF16)
        vb = v.astype(BF16)
        ki_bd = _blockdiag(k * jnp.exp(b_mid - b), k_masks)
        v_bd = _blockdiag(v, v_masks)
        a = jnp.where(causal, _dot_nt(qi, ki_bd), 0.0)
        st = st_ref[...]
        o = _dot(a.astype(BF16), v_bd) + _dot_nt(qe, st.astype(BF16))
        st_ref[...] = st * jnp.exp(b_last) + jnp.where(st_mask, _dot_tn(vb, kd), 0.0)
        if final:
            acc_ref[rows, :] = of_ref[rows, :] + o
        else:
            o_ref[rows, :] = o
        return carry

    lax.fori_loop(0, nc, chunk_body, 0)

    if final:
        o = acc_ref[...]
        parts = []
        for h in range(GLA_HEADS):
            oh = o[:, h * GLA_DV:(h + 1) * GLA_DV]
            ms = jnp.mean(oh * oh, axis=-1, keepdims=True)
            parts.append(oh * lax.rsqrt(ms + HEAD_NORM_EPS))
        on = jnp.concatenate(parts, axis=1) * ng_ref[...]
        og = og_ref[...]
        o_ref[...] = on * (og * _sigmoid(og))


def _gla(p, wa2p, ba, batch, seq_len, reverse, o_fwd=None, norm_g=None):
    final = o_fwd is not None
    tb = min(SEQ_TILE, seq_len)
    nt = seq_len // tb
    n = batch * seq_len

    def rowblk(b, j):
        return b * nt + ((nt - 1 - j) if reverse else j)

    in_specs = [
        pl.BlockSpec((tb, 512), lambda b, j: (rowblk(b, j), COL_QK // 512)),
        pl.BlockSpec((tb, 512), lambda b, j: (rowblk(b, j), COL_V // 512)),
        pl.BlockSpec((tb, 256), lambda b, j: (rowblk(b, j), COL_GS // 256)),
        pl.BlockSpec((256, GLA_QK_W), lambda b, j: (0, 0)),
        pl.BlockSpec((1, GLA_QK_W), lambda b, j: (0, 0)),
    ]
    args = [p, p, p, wa2p, ba]
    scratch = [pltpu.VMEM((GLA_V_W, GLA_QK_W), F32), pltpu.VMEM((tb, GLA_QK_W), F32)]
    if final:
        in_specs += [
            pl.BlockSpec((tb, GLA_V_W), lambda b, j: (rowblk(b, j), 0)),
            pl.BlockSpec((tb, 512), lambda b, j: (rowblk(b, j), COL_OG // 512)),
            pl.BlockSpec((1, GLA_V_W), lambda b, j: (0, 0)),
        ]
        args += [o_fwd, p, norm_g]
        scratch.append(pltpu.VMEM((tb, GLA_V_W), F32))
    kern = functools.partial(_gla_kernel, reverse=reverse, tb=tb, final=final)
    return pl.pallas_call(
        kern,
        grid=(batch, nt),
        in_specs=in_specs,
        out_specs=pl.BlockSpec((tb, GLA_V_W), lambda b, j: (rowblk(b, j), 0)),
        out_shape=jax.ShapeDtypeStruct((n, GLA_V_W), F32),
        scratch_shapes=scratch,
        compiler_params=pltpu.CompilerParams(
            dimension_semantics=("arbitrary", "arbitrary"),
            vmem_limit_bytes=VMEM_LIMIT),
        name="gla_bwd" if reverse else "gla_fwd",
    )(*args)


def _rwkv_kernel(*refs, reverse, tb, final):
    if final:
        (r_ref, k_ref, v_ref, rs_ref, w0_ref, w2_ref, a0_ref, a2_ref, kk_ref, ka_ref,
         yf_ref, g2_ref, rk_ref, lnw_ref, lnb_ref,
         o_ref, p_ref, at_s, bt_s, kt_s, rt_s, bd_s, kd_s, gl_s, y_s) = refs
    else:
        (r_ref, k_ref, v_ref, rs_ref, w0_ref, w2_ref, a0_ref, a2_ref, kk_ref, ka_ref,
         o_ref, p_ref, at_s, bt_s, kt_s, rt_s, bd_s, kd_s, gl_s) = refs
        y_s = o_ref
    j = pl.program_id(1)
    nc = tb // CHUNK
    G = RWKV_GROUP
    L = G * RWKV_HEAD
    ngrp = RWKV_W // L

    @pl.when(j == 0)
    def _():
        p_ref[...] = jnp.zeros_like(p_ref)

    r = r_ref[...]
    k = k_ref[...]
    lora_in = rs_ref[:, 0:RWKV_DECAY_LORA + RWKV_AAA_LORA]
    w_pre = w0_ref[...] + _dot_x3(jnp.tanh(lora_in), w2_ref[...])
    logw = -jnp.exp(-_softplus(-w_pre) - 0.5)
    a_lr = _sigmoid(a0_ref[...] + _dot(lora_in.astype(BF16), a2_ref[...]))
    hr = _div(_iota2((RWKV_W, RWKV_W), 0), RWKV_HEAD)
    hc = _div(_iota2((RWKV_W, RWKV_W), 1), RWKV_HEAD)
    head_ones = jnp.where(hr == hc, 1.0, 0.0)
    kk = k * kk_ref[...]
    ssq = _dot_exact_rhs(kk * kk, head_ones)
    kk = kk / jnp.maximum(jnp.sqrt(ssq), 1e-12)
    k2 = k * (1.0 + (a_lr - 1.0) * ka_ref[...])

    tri, ones_c = _chunk_tri(tb, reverse)
    cs = _dot_exact_lhs(tri, logw)
    cs_tot = _dot_exact_lhs(ones_c, logw)
    g_inv = jnp.exp(-cs)
    g_rem = jnp.exp(cs_tot - cs)
    at_s[...] = (-kk) * jnp.exp(cs - logw)
    bt_s[...] = (kk * a_lr) * g_inv
    kt_s[...] = k2 * g_inv
    rt_s[...] = r * jnp.exp(cs)
    bd_s[...] = (kk * a_lr) * g_rem
    kd_s[...] = k2 * g_rem
    gl_s[...] = jnp.exp(cs_tot)

    lane_h = _div(_iota2((CHUNK, L), 1), RWKV_HEAD)
    h_masks = [lane_h == h for h in range(G)]
    t_idx = _iota2((CHUNK, L), 0)
    s_idx = _mod(_iota2((CHUNK, L), 1), CHUNK)
    strict = (s_idx > t_idx) if reverse else (s_idx < t_idx)
    incl = (s_idx >= t_idx) if reverse else (s_idx <= t_idx)
    eye_p = jnp.where(s_idx == t_idx, 1.0, 0.0)
    bd_mask = _div(_iota2((L, L), 0), RWKV_HEAD) == _div(_iota2((L, L), 1), RWKV_HEAD)
    eye_l = jnp.where(_iota2((L, L), 0) == _iota2((L, L), 1), 1.0, 0.0)

    def blockdiag(z):
        return _blockdiag(z, h_masks)

    def chunk_body(ci, carry):
        c = (nc - 1 - ci) if reverse else ci
        r0 = pl.multiple_of(c * CHUNK, CHUNK)
        rows = pl.ds(r0, CHUNK)
        for gi in range(ngrp):
            lanes = slice(gi * L, (gi + 1) * L)
            at = at_s[rows, lanes]
            bt = bt_s[rows, lanes]
            kt = kt_s[rows, lanes]
            rt = rt_s[rows, lanes]
            bd = bd_s[rows, lanes]
            kd = kd_s[rows, lanes]
            v = v_ref[rows, lanes]
            glast = gl_s[pl.ds(r0, 1), lanes]
            x = jnp.concatenate([at, rt], axis=0).astype(BF16)
            g1 = _dot_nt(x, blockdiag(bt))
            g2 = _dot_nt(x, blockdiag(kt))
            lab = jnp.where(strict, g1[0:CHUNK], 0.0)
            mrb = jnp.where(incl, g1[CHUNK:2 * CHUNK], 0.0)
            lak = jnp.where(strict, g2[0:CHUNK], 0.0)
            mrk = jnp.where(incl, g2[CHUNK:2 * CHUNK], 0.0)
            tm = eye_p + lab
            xp = _dot(lab.astype(BF16), blockdiag(lab))
            for _ in range(4):
                y = _dot(jnp.concatenate([tm, xp], axis=0).astype(BF16), blockdiag(xp))
                tm = tm + y[0:CHUNK]
                xp = y[CHUNK:2 * CHUNK]
            tm = tm + _dot(tm.astype(BF16), blockdiag(xp))
            v_bd = blockdiag(v)
            lakv = _dot(lak.astype(BF16), v_bd)
            wu = _dot(tm.astype(BF16), jnp.concatenate([blockdiag(at), blockdiag(lakv)], axis=1))
            wa = wu[:, 0:L]
            uv = wu[:, L:2 * L]
            ro = _dot(mrb.astype(BF16), jnp.concatenate([blockdiag(wa), blockdiag(uv)], axis=1))
            reff = rt + ro[:, 0:L]
            oloc = ro[:, L:2 * L] + _dot(mrk.astype(BF16), v_bd)
            gh = _dot_tn(bd.astype(BF16), jnp.concatenate([wa, uv], axis=1).astype(BF16))
            gm = eye_l * glast + jnp.where(bd_mask, gh[:, 0:L], 0.0)
            hm = jnp.where(bd_mask, gh[:, L:2 * L] + _dot_tn(kd.astype(BF16), v.astype(BF16)), 0.0)
            pst = p_ref[gi].astype(BF16)
            y_s[rows, lanes] = _dot(reff.astype(BF16), pst) + oloc
            p_ref[gi] = _dot(gm.astype(BF16), pst) + hm
        return carry

    lax.fori_loop(0, nc, chunk_body, 0)

    if final:
        y = y_s[...] + yf_ref[...]
        inv_n = 1.0 / RWKV_HEAD
        mu = _dot_exact_rhs(y, head_ones) * inv_n
        yc = y - mu
        var = _dot_exact_rhs(yc * yc, head_ones) * inv_n
        yn = yc * lax.rsqrt(var + RWKV_GN_EPS) * lnw_ref[...] + lnb_ref[...]
        bonus = _dot_exact_rhs(r * k2 * rk_ref[...], head_ones) * v_ref[...]
        gl = rs_ref[:, RWKV_DECAY_LORA + RWKV_AAA_LORA:]
        g = _dot(_sigmoid(gl).astype(BF16), g2_ref[...])
        o_ref[...] = (yn + bonus) * g


def _rwkv(p, w0, w2p, a0, a2p, k_k, k_a, batch, seq_len, reverse,
          y_fwd=None, g2=None, r_k=None, ln_w=None, ln_b=None):
    final = y_fwd is not None
    tb = min(SEQ_TILE, seq_len)
    nt = seq_len // tb
    n = batch * seq_len
    L = RWKV_GROUP * RWKV_HEAD
    ngrp = RWKV_W // L
    lora = RWKV_DECAY_LORA + RWKV_AAA_LORA

    def rowblk(b, j):
        return b * nt + ((nt - 1 - j) if reverse else j)

    vec = pl.BlockSpec((1, RWKV_W), lambda b, j: (0, 0))
    in_specs = [
        pl.BlockSpec((tb, 512), lambda b, j: (rowblk(b, j), COL_R // 512)),
        pl.BlockSpec((tb, 512), lambda b, j: (rowblk(b, j), COL_RK // 512)),
        pl.BlockSpec((tb, 512), lambda b, j: (rowblk(b, j), COL_RV // 512)),
        pl.BlockSpec((tb, 256), lambda b, j: (rowblk(b, j), COL_RS // 256)),
        vec,
        pl.BlockSpec((lora, RWKV_W), lambda b, j: (0, 0)),
        vec,
        pl.BlockSpec((lora, RWKV_W), lambda b, j: (0, 0)),
        vec, vec,
    ]
    args = [p, p, p, p, w0, w2p, a0, a2p, k_k, k_a]
    scratch = [pltpu.VMEM((ngrp, L, L), F32)] + [pltpu.VMEM((tb, RWKV_W), F32)] * 7
    if final:
        in_specs += [
            pl.BlockSpec((tb, RWKV_W), lambda b, j: (rowblk(b, j), 0)),
            pl.BlockSpec((RWKV_GATE_LORA, RWKV_W), lambda b, j: (0, 0)),
            vec, vec, vec,
        ]
        args += [y_fwd, g2, r_k, ln_w, ln_b]
        scratch.append(pltpu.VMEM((tb, RWKV_W), F32))
    kern = functools.partial(_rwkv_kernel, reverse=reverse, tb=tb, final=final)
    return pl.pallas_call(
        kern,
        grid=(batch, nt),
        in_specs=in_specs,
        out_specs=pl.BlockSpec((tb, RWKV_W), lambda b, j: (rowblk(b, j), 0)),
        out_shape=jax.ShapeDtypeStruct((n, RWKV_W), F32),
        scratch_shapes=scratch,
        compiler_params=pltpu.CompilerParams(
            dimension_semantics=("arbitrary", "arbitrary"),
            vmem_limit_bytes=VMEM_LIMIT),
        name="rwkv_bwd" if reverse else "rwkv_fwd",
    )(*args)


def _merge_kernel(x_ref, oa_ref, ob_ref, ga_ref, gb_ref, wa_ref, wb_ref, wo_ref, o_ref):
    ya = _dot(oa_ref[...].astype(BF16), wa_ref[...])
    yb = _dot(ob_ref[...].astype(BF16), wb_ref[...])
    merged = _sigmoid(ga_ref[...]) * ya + _sigmoid(gb_ref[...]) * yb
    o_ref[...] = x_ref[...] + _dot(merged.astype(BF16), wo_ref[...])


def _merge(x2, oa, ob, p, gla_proj, rwkv_proj, w_out, seq_len):
    n = x2.shape[0]
    tm = min(ROW_TILE, seq_len)
    return pl.pallas_call(
        _merge_kernel,
        grid=(n // tm,),
        in_specs=[
            pl.BlockSpec((tm, D_MODEL), lambda i: (i, 0)),
            pl.BlockSpec((tm, GLA_V_W), lambda i: (i, 0)),
            pl.BlockSpec((tm, RWKV_W), lambda i: (i, 0)),
            pl.BlockSpec((tm, D_MODEL), lambda i: (i, COL_GA // D_MODEL)),
            pl.BlockSpec((tm, D_MODEL), lambda i: (i, COL_GB // D_MODEL)),
            pl.BlockSpec((GLA_V_W, D_MODEL), lambda i: (0, 0)),
            pl.BlockSpec((RWKV_W, D_MODEL), lambda i: (0, 0)),
            pl.BlockSpec((D_MODEL, D_MODEL), lambda i: (0, 0)),
        ],
        out_specs=pl.BlockSpec((tm, D_MODEL), lambda i: (i, 0)),
        out_shape=jax.ShapeDtypeStruct((n, D_MODEL), F32),
        compiler_params=pltpu.CompilerParams(
            dimension_semantics=("arbitrary",),
            vmem_limit_bytes=VMEM_LIMIT),
        name="merge",
    )(x2, oa, ob, p, p, gla_proj, rwkv_proj, w_out)


def _ffn_kernel(xp_ref, x_ref, xn_ref, g2_ref, wu_ref, cw_ref, cb_ref, wd_ref, gf_ref, o_ref,
                h_ref, u_ref, acc_ref, *, tm, tiles_per_seq):
    i = pl.program_id(0)
    pos = lax.rem(i, tiles_per_seq)
    keep_prev = jnp.where(pos == 0, 0.0, 1.0)
    keep_next = jnp.where(pos == tiles_per_seq - 1, 0.0, 1.0)
    g2 = g2_ref[...]
    h_ref[0:HALO, :] = _rmsnorm(xp_ref[...] * keep_prev, g2).astype(BF16)
    h_ref[HALO:HALO + tm, :] = _rmsnorm(x_ref[...], g2).astype(BF16)
    h_ref[HALO + tm:HALO + tm + HALO, :] = _rmsnorm(xn_ref[...] * keep_next, g2).astype(BF16)
    acc_ref[...] = jnp.zeros_like(acc_ref)

    def chunk_body(c, carry):
        u_ref[...] = _dot(h_ref[...], wu_ref[c])
        cw = cw_ref[c]
        uc = (cw[0:1, :] * u_ref[HALO - 1:HALO - 1 + tm, :]
              + cw[1:2, :] * u_ref[HALO:HALO + tm, :]
              + cw[2:3, :] * u_ref[HALO + 1:HALO + 1 + tm, :]
              + cb_ref[c])
        ug = uc[:, 0:FF_CHUNK]
        act = (ug * _sigmoid(ug)) * uc[:, FF_CHUNK:2 * FF_CHUNK]
        acc_ref[...] += _dot(act.astype(BF16), wd_ref[c])
        return carry

    lax.fori_loop(0, FF_NCHUNK, chunk_body, 0)
    o_ref[...] = _rmsnorm(x_ref[...] + acc_ref[...], gf_ref[...])


def _ffn(x1, g2, wu_p, cw_p, cb_p, wd_p, gf, seq_len):
    n = x1.shape[0]
    tm = min(FFN_ROW_TILE, seq_len)
    tiles_per_seq = seq_len // tm
    hb = tm // HALO
    nhb = n // HALO
    kern = functools.partial(_ffn_kernel, tm=tm, tiles_per_seq=tiles_per_seq)
    return pl.pallas_call(
        kern,
        grid=(n // tm,),
        in_specs=[
            pl.BlockSpec((HALO, D_MODEL), lambda i: (jnp.maximum(i * hb - 1, 0), 0)),
            pl.BlockSpec((tm, D_MODEL), lambda i: (i, 0)),
            pl.BlockSpec((HALO, D_MODEL), lambda i: (jnp.minimum((i + 1) * hb, nhb - 1), 0)),
            pl.BlockSpec((1, D_MODEL), lambda i: (0, 0)),
            pl.BlockSpec((FF_NCHUNK, D_MODEL, 2 * FF_CHUNK), lambda i: (0, 0, 0)),
            pl.BlockSpec((FF_NCHUNK, 3, 2 * FF_CHUNK), lambda i: (0, 0, 0)),
            pl.BlockSpec((FF_NCHUNK, 1, 2 * FF_CHUNK), lambda i: (0, 0, 0)),
            pl.BlockSpec((FF_NCHUNK, FF_CHUNK, D_MODEL), lambda i: (0, 0, 0)),
            pl.BlockSpec((1, D_MODEL), lambda i: (0, 0)),
        ],
        out_specs=pl.BlockSpec((tm, D_MODEL), lambda i: (i, 0)),
        out_shape=jax.ShapeDtypeStruct((n, D_MODEL), F32),
        scratch_shapes=[
            pltpu.VMEM((tm + 2 * HALO, D_MODEL), BF16),
            pltpu.VMEM((tm + 2 * HALO, 2 * FF_CHUNK), F32),
            pltpu.VMEM((tm, D_MODEL), F32),
        ],
        compiler_params=pltpu.CompilerParams(
            dimension_semantics=("arbitrary",),
            vmem_limit_bytes=VMEM_LIMIT),
        name="ffn",
    )(x1, x1, x1, g2, wu_p, cw_p, cb_p, wd_p, gf)


def _pack_w_in(w):
    gla_w = 2 * GLA_QK_W + 2 * GLA_V_W + 2 * GLA_GATE_RANK
    r0 = gla_w
    g0 = r0 + 3 * RWKV_W + RWKV_DECAY_LORA + RWKV_AAA_LORA + RWKV_GATE_LORA
    pad = jnp.zeros((w.shape[0], 256 - 2 * GLA_GATE_RANK), w.dtype)
    return jnp.concatenate([
        w[:, g0:g0 + 2 * D_MODEL],
        w[:, 0:2 * GLA_QK_W + 2 * GLA_V_W],
        w[:, r0:r0 + 3 * RWKV_W + 256],
        w[:, 2 * GLA_QK_W + 2 * GLA_V_W:gla_w],
        pad], axis=1)


def _pack_mu(mu):
    z = jnp.zeros((COL_R,), mu.dtype)
    z2 = jnp.zeros((P_COLS - COL_GS,), mu.dtype)
    return jnp.concatenate([z, mu, z2])[None, :]


def _pack_ff_cols(t):
    pad = [(0, 0)] * (t.ndim - 1) + [(0, FF_PAD - D_FF)]
    g = jnp.pad(t[..., :D_FF], pad)
    v = jnp.pad(t[..., D_FF:], pad)
    lead = t.shape[:-1]
    g = g.reshape(lead + (FF_NCHUNK, FF_CHUNK))
    v = v.reshape(lead + (FF_NCHUNK, FF_CHUNK))
    gv = jnp.concatenate([g, v], axis=-1)
    return jnp.moveaxis(gv, -2, 0)


def kernel(x, norm1_g, w_in, gla_wa2_f, gla_ba_f, gla_wa2_b, gla_ba_b, gla_norm_g, gla_proj, rwkv_mu_prev, rwkv_mu_next, rwkv_w0_f, rwkv_w2_f, rwkv_w0_b, rwkv_w2_b, rwkv_a0, rwkv_a2, rwkv_g2, rwkv_k_k, rwkv_k_a, rwkv_r_k, rwkv_ln_w, rwkv_ln_b, rwkv_proj, w_out, norm2_g, ffn_up, ffn_conv_w, ffn_conv_b, ffn_down, norm_f_g):
    batch, seq_len, d = x.shape
    assert w_in.shape[0] == 1 and d == D_MODEL and seq_len % CHUNK == 0
    x2 = x.reshape(batch * seq_len, d)
    row = lambda t: t.reshape(1, -1)
    for l in range(1):
        w_packed = _pack_w_in(w_in[l]).astype(BF16)
        wa2_f = jnp.pad(gla_wa2_f[l], ((0, 256 - GLA_GATE_RANK), (0, 0)))
        wa2_b = jnp.pad(gla_wa2_b[l], ((GLA_GATE_RANK, 256 - 2 * GLA_GATE_RANK), (0, 0)))
        zl = jnp.zeros((RWKV_AAA_LORA, RWKV_W), F32)
        w2_f = jnp.concatenate([rwkv_w2_f[l], zl], axis=0)
        w2_b = jnp.concatenate([rwkv_w2_b[l], zl], axis=0)
        a2p = jnp.concatenate([jnp.zeros((RWKV_DECAY_LORA, RWKV_W), F32), rwkv_a2[l]], axis=0).astype(BF16)

        p = _inproj(x2, row(norm1_g[l]), w_packed, _pack_mu(rwkv_mu_prev[l]),
                    _pack_mu(rwkv_mu_next[l]), seq_len)

        o_f = _gla(p, wa2_f, row(gla_ba_f[l]), batch, seq_len, reverse=False)
        oa = _gla(p, wa2_b, row(gla_ba_b[l]), batch, seq_len, reverse=True,
                  o_fwd=o_f, norm_g=row(gla_norm_g[l]))

        rw_common = (row(rwkv_a0[l]), a2p, row(rwkv_k_k[l]), row(rwkv_k_a[l]))
        y_f = _rwkv(p, row(rwkv_w0_f[l]), w2_f, *rw_common, batch, seq_len, reverse=False)
        ob = _rwkv(p, row(rwkv_w0_b[l]), w2_b, *rw_common, batch, seq_len, reverse=True,
                   y_fwd=y_f, g2=rwkv_g2[l].astype(BF16), r_k=row(rwkv_r_k[l]),
                   ln_w=row(rwkv_ln_w[l]), ln_b=row(rwkv_ln_b[l]))

        x1 = _merge(x2, oa, ob, p, gla_proj[l].astype(BF16), rwkv_proj[l].astype(BF16),
                    w_out[l].astype(BF16), seq_len)

        wu_p = _pack_ff_cols(ffn_up[l]).astype(BF16)
        cw_p = _pack_ff_cols(ffn_conv_w[l])
        cb_p = _pack_ff_cols(ffn_conv_b[l][None, :])
        wd_p = jnp.pad(ffn_down[l], ((0, FF_PAD - D_FF), (0, 0))).reshape(
            FF_NCHUNK, FF_CHUNK, D_MODEL).astype(BF16)
        x2 = _ffn(x1, row(norm2_g[l]), wu_p, cw_p, cb_p, wd_p, row(norm_f_g), seq_len)
    return x2.reshape(batch, seq_len, d)
```

```python
import functools

import jax
import jax.numpy as jnp
from jax import lax
from jax.experimental import pallas as pl
from jax.experimental.pallas import tpu as pltpu

F32 = jnp.float32
BF16 = jnp.bfloat16

D_MODEL = 1024
GLA_HEADS = 4
GLA_DK = 64
GLA_DV = 128
GLA_QK_W = GLA_HEADS * GLA_DK
GLA_V_W = GLA_HEADS * GLA_DV
GLA_GATE_RANK = 16
GLA_LOGIT_NORM = 16.0
CHUNK = 64
RWKV_HEAD = 64
RWKV_W = 512
RWKV_HEADS = RWKV_W // RWKV_HEAD
RWKV_DECAY_LORA = 64
RWKV_AAA_LORA = 64
RWKV_GATE_LORA = 128
RWKV_GN_EPS = RWKV_HEAD * 1e-5
D_FF = 2752
NORM_EPS = 1e-6
HEAD_NORM_EPS = 1e-5

COL_GA, COL_GB = 0, 1024
COL_QK, COL_V, COL_OG = 2048, 2560, 3072
COL_R, COL_RK, COL_RV = 3584, 4096, 4608
COL_RS, COL_GS = 5120, 5376
P_COLS = 5632
HALO = 16

FF_CHUNK = 256
FF_PAD = 2816
FF_NCHUNK = FF_PAD // FF_CHUNK

VMEM_LIMIT = 56 * 1024 * 1024
ROW_TILE = 1024
FFN_ROW_TILE = 512
SEQ_TILE = 256
RWKV_GROUP = 4


def _sigmoid(x):
    return 1.0 / (1.0 + jnp.exp(-x))


def _softplus(z):
    return jnp.maximum(z, 0.0) + jnp.log(1.0 + jnp.exp(-jnp.abs(z)))


def _split3(x):
    hi = x.astype(BF16)
    r1 = x - hi.astype(F32)
    mid = r1.astype(BF16)
    lo = (r1 - mid.astype(F32)).astype(BF16)
    return hi, mid, lo


def _dot(a, b):
    return jnp.dot(a, b, preferred_element_type=F32)


def _dot_nt(a, b):
    return lax.dot_general(a, b, (((1,), (1,)), ((), ())), preferred_element_type=F32)


def _dot_tn(a, b):
    return lax.dot_general(a, b, (((0,), (0,)), ((), ())), preferred_element_type=F32)


def _dot_exact_lhs(sel, x):
    hi = x.astype(BF16)
    lo = (x - hi.astype(F32)).astype(BF16)
    s = sel.astype(BF16)
    return _dot(s, hi) + _dot(s, lo)


def _dot_exact_rhs(x, sel):
    hi = x.astype(BF16)
    lo = (x - hi.astype(F32)).astype(BF16)
    s = sel.astype(BF16)
    return _dot(hi, s) + _dot(lo, s)


def _dot_x3(a, b):
    ah = a.astype(BF16)
    al = (a - ah.astype(F32)).astype(BF16)
    bh = b.astype(BF16)
    bl = (b - bh.astype(F32)).astype(BF16)
    return _dot(ah, bh) + _dot(al, bh) + _dot(ah, bl)


def _rmsnorm(x, g):
    ms = jnp.mean(x * x, axis=-1, keepdims=True)
    return x * lax.rsqrt(ms + NORM_EPS) * g


def _iota2(shape, dim):
    return lax.broadcasted_iota(jnp.int32, shape, dim)


def _div(x, size):
    return lax.shift_right_logical(x, size.bit_length() - 1)


def _mod(x, size):
    return lax.bitwise_and(x, size - 1)


def _chunk_tri(tb, reverse):
    row = _iota2((tb, tb), 0)
    col = _iota2((tb, tb), 1)
    same = _div(row, CHUNK) == _div(col, CHUNK)
    tri = jnp.where(same & ((col >= row) if reverse else (col <= row)), 1.0, 0.0)
    return tri, jnp.where(same, 1.0, 0.0)


def _blockdiag(z, head_masks):
    return jnp.concatenate([jnp.where(m, z, 0.0) for m in head_masks], axis=0).astype(BF16)


def _inproj_kernel(xp_ref, x_ref, xn_ref, g_ref, w_ref, mup_ref, mun_ref, o_ref,
                   h_ref, p_ref, *, tm, tiles_per_seq):
    i = pl.program_id(0)
    j = pl.program_id(1)

    @pl.when(j == 0)
    def _():
        pos = lax.rem(i, tiles_per_seq)
        keep_prev = jnp.where(pos == 0, 0.0, 1.0)
        keep_next = jnp.where(pos == tiles_per_seq - 1, 0.0, 1.0)
        g = g_ref[...]
        h_ref[0:HALO, :] = _rmsnorm(xp_ref[...] * keep_prev, g).astype(BF16)
        h_ref[HALO:HALO + tm, :] = _rmsnorm(x_ref[...], g).astype(BF16)
        h_ref[HALO + tm:HALO + tm + HALO, :] = _rmsnorm(xn_ref[...] * keep_next, g).astype(BF16)

    p_ref[...] = _dot(h_ref[...], w_ref[...])
    p = p_ref[HALO:HALO + tm, :]
    pp = p_ref[HALO - 1:HALO - 1 + tm, :]
    pn = p_ref[HALO + 1:HALO + 1 + tm, :]
    o_ref[...] = p + mup_ref[...] * (pp - p) + mun_ref[...] * (pn - p)


def _inproj(x2, g, w_packed, mu_prev, mu_next, seq_len):
    n = x2.shape[0]
    tm = min(ROW_TILE, seq_len)
    tn = 512
    tiles_per_seq = seq_len // tm
    hb = tm // HALO
    nhb = n // HALO
    kern = functools.partial(_inproj_kernel, tm=tm, tiles_per_seq=tiles_per_seq)
    return pl.pallas_call(
        kern,
        grid=(n // tm, P_COLS // tn),
        in_specs=[
            pl.BlockSpec((HALO, D_MODEL), lambda i, j: (jnp.maximum(i * hb - 1, 0), 0)),
            pl.BlockSpec((tm, D_MODEL), lambda i, j: (i, 0)),
            pl.BlockSpec((HALO, D_MODEL), lambda i, j: (jnp.minimum((i + 1) * hb, nhb - 1), 0)),
            pl.BlockSpec((1, D_MODEL), lambda i, j: (0, 0)),
            pl.BlockSpec((D_MODEL, tn), lambda i, j: (0, j)),
            pl.BlockSpec((1, tn), lambda i, j: (0, j)),
            pl.BlockSpec((1, tn), lambda i, j: (0, j)),
        ],
        out_specs=pl.BlockSpec((tm, tn), lambda i, j: (i, j)),
        out_shape=jax.ShapeDtypeStruct((n, P_COLS), F32),
        scratch_shapes=[
            pltpu.VMEM((tm + 2 * HALO, D_MODEL), BF16),
            pltpu.VMEM((tm + 2 * HALO, tn), F32),
        ],
        compiler_params=pltpu.CompilerParams(
            dimension_semantics=("arbitrary", "arbitrary"),
            vmem_limit_bytes=VMEM_LIMIT),
        name="inproj",
    )(x2, x2, x2, g, w_packed, mu_prev, mu_next)


def _gla_kernel(*refs, reverse, tb, final):
    if final:
        (qk_ref, v_ref, gs_ref, wa2_ref, ba_ref, of_ref, og_ref, ng_ref,
         o_ref, st_ref, b_ref, acc_ref) = refs
    else:
        qk_ref, v_ref, gs_ref, wa2_ref, ba_ref, o_ref, st_ref, b_ref = refs
    j = pl.program_id(1)
    nc = tb // CHUNK

    @pl.when(j == 0)
    def _():
        st_ref[...] = jnp.zeros_like(st_ref)

    logit = _dot_x3(gs_ref[...], wa2_ref[...]) + ba_ref[...]
    la = (jnp.minimum(logit, 0.0) - jnp.log(1.0 + jnp.exp(-jnp.abs(logit)))) * (1.0 / GLA_LOGIT_NORM)
    tri, _ = _chunk_tri(tb, reverse)
    b_ref[...] = _dot_exact_lhs(tri, la)

    lane_k = _div(_iota2((CHUNK, GLA_QK_W), 1), GLA_DK)
    lane_v = _div(_iota2((CHUNK, GLA_V_W), 1), GLA_DV)
    k_masks = [lane_k == h for h in range(GLA_HEADS)]
    v_masks = [lane_v == h for h in range(GLA_HEADS)]
    t_idx = _iota2((CHUNK, GLA_QK_W), 0)
    s_idx = _mod(_iota2((CHUNK, GLA_QK_W), 1), CHUNK)
    causal = (s_idx >= t_idx) if reverse else (s_idx <= t_idx)
    st_mask = (_div(_iota2((GLA_V_W, GLA_QK_W), 0), GLA_DV)
               == _div(_iota2((GLA_V_W, GLA_QK_W), 1), GLA_DK))
    i_ref = (CHUNK - 1 - CHUNK // 2) if reverse else CHUNK // 2
    i_last = 0 if reverse else CHUNK - 1

    QI, KD, QE, VB, KI_BD, V_BD, DEC = ([] for _ in range(7))
    for c in range(nc):
        rows = slice(c * CHUNK, (c + 1) * CHUNK)
        b = b_ref[rows, :]
        b_mid = b[i_ref:i_ref + 1, :]
        b_last = b[i_last:i_last + 1, :]
        q = qk_ref[rows, 0:GLA_QK_W] * (GLA_DK ** -0.5)
        k = qk_ref[rows, GLA_QK_W:2 * GLA_QK_W]
        v = v_ref[rows, :]
        QI.append((q * jnp.exp(b - b_mid)).astype(BF16))
        KD.append((k * jnp.exp(b_last - b)).astype(BF16))
        QE.append((q * jnp.exp(b)).astype(BF16))
        VB.append(v.astype(BF16))
        KI_BD.append(_blockdiag(k * jnp.exp(b_mid - b), k_masks))
        V_BD.append(_blockdiag(v, v_masks))
        DEC.append(jnp.exp(b_last))
    A = [jnp.where(causal, _dot_nt(qi, kb), 0.0) for qi, kb in zip(QI, KI_BD)]
    O_IN = [_dot(a.astype(BF16), vb) for a, vb in zip(A, V_BD)]
    KV = [jnp.where(st_mask, _dot_tn(vb, kd), 0.0) for vb, kd in zip(VB, KD)]
    for ci in range(nc):
        c = (nc - 1 - ci) if reverse else ci
        rows = slice(c * CHUNK, (c + 1) * CHUNK)
        st = st_ref[...]
        o = O_IN[c] + _dot_nt(QE[c], st.astype(BF16))
        st_ref[...] = st * DEC[c] + KV[c]
        if final:
            acc_ref[rows, :] = of_ref[rows, :] + o
        else:
            o_ref[rows, :] = o

    if final:
        o = acc_ref[...]
        parts = []
        for h in range(GLA_HEADS):
            oh = o[:, h * GLA_DV:(h + 1) * GLA_DV]
            ms = jnp.mean(oh * oh, axis=-1, keepdims=True)
            parts.append(oh * lax.rsqrt(ms + HEAD_NORM_EPS))
        on = jnp.concatenate(parts, axis=1) * ng_ref[...]
        og = og_ref[...]
        o_ref[...] = on * (og * _sigmoid(og))


def _gla(p, wa2p, ba, batch, seq_len, reverse, o_fwd=None, norm_g=None):
    final = o_fwd is not None
    tb = min(SEQ_TILE, seq_len)
    nt = seq_len // tb
    n = batch * seq_len

    def rowblk(b, j):
        return b * nt + ((nt - 1 - j) if reverse else j)

    in_specs = [
        pl.BlockSpec((tb, 512), lambda b, j: (rowblk(b, j), COL_QK // 512)),
        pl.BlockSpec((tb, 512), lambda b, j: (rowblk(b, j), COL_V // 512)),
        pl.BlockSpec((tb, 256), lambda b, j: (rowblk(b, j), COL_GS // 256)),
        pl.BlockSpec((256, GLA_QK_W), lambda b, j: (0, 0)),
        pl.BlockSpec((1, GLA_QK_W), lambda b, j: (0, 0)),
    ]
    args = [p, p, p, wa2p, ba]
    scratch = [pltpu.VMEM((GLA_V_W, GLA_QK_W), F32), pltpu.VMEM((tb, GLA_QK_W), F32)]
    if final:
        in_specs += [
            pl.BlockSpec((tb, GLA_V_W), lambda b, j: (rowblk(b, j), 0)),
            pl.BlockSpec((tb, 512), lambda b, j: (rowblk(b, j), COL_OG // 512)),
            pl.BlockSpec((1, GLA_V_W), lambda b, j: (0, 0)),
        ]
        args += [o_fwd, p, norm_g]
        scratch.append(pltpu.VMEM((tb, GLA_V_W), F32))
    kern = functools.partial(_gla_kernel, reverse=reverse, tb=tb, final=final)
    return pl.pallas_call(
        kern,
        grid=(batch, nt),
        in_specs=in_specs,
        out_specs=pl.BlockSpec((tb, GLA_V_W), lambda b, j: (rowblk(b, j), 0)),
        out_shape=jax.ShapeDtypeStruct((n, GLA_V_W), F32),
        scratch_shapes=scratch,
        compiler_params=pltpu.CompilerParams(
            dimension_semantics=("arbitrary", "arbitrary"),
            vmem_limit_bytes=VMEM_LIMIT),
        name="gla_bwd" if reverse else "gla_fwd",
    )(*args)


def _rwkv_kernel(*refs, reverse, tb, final):
    if final:
        (r_ref, k_ref, v_ref, rs_ref, w0_ref, w2_ref, a0_ref, a2_ref, kk_ref, ka_ref,
         yf_ref, g2_ref, rk_ref, lnw_ref, lnb_ref,
         o_ref, p_ref, cs_s, lw_s, kk_s, ba_s, k2_s, reff_s, oloc_s, gm_s, hm_s, y_s) = refs
    else:
        (r_ref, k_ref, v_ref, rs_ref, w0_ref, w2_ref, a0_ref, a2_ref, kk_ref, ka_ref,
         o_ref, p_ref, cs_s, lw_s, kk_s, ba_s, k2_s, reff_s, oloc_s, gm_s, hm_s) = refs
        y_s = o_ref
    j = pl.program_id(1)
    nc = tb // CHUNK
    G = RWKV_GROUP
    L = G * RWKV_HEAD
    ngrp = RWKV_W // L

    @pl.when(j == 0)
    def _():
        p_ref[...] = jnp.zeros_like(p_ref)

    r = r_ref[...]
    k = k_ref[...]
    lora_in = rs_ref[:, 0:RWKV_DECAY_LORA + RWKV_AAA_LORA]
    w_pre = w0_ref[...] + _dot_x3(jnp.tanh(lora_in), w2_ref[...])
    logw = -jnp.exp(-_softplus(-w_pre) - 0.5)
    a_lr = _sigmoid(a0_ref[...] + _dot(lora_in.astype(BF16), a2_ref[...]))
    hr = _div(_iota2((RWKV_W, RWKV_W), 0), RWKV_HEAD)
    hc = _div(_iota2((RWKV_W, RWKV_W), 1), RWKV_HEAD)
    head_ones = jnp.where(hr == hc, 1.0, 0.0)
    kk = k * kk_ref[...]
    ssq = _dot_exact_rhs(kk * kk, head_ones)
    kk = kk / jnp.maximum(jnp.sqrt(ssq), 1e-12)
    k2 = k * (1.0 + (a_lr - 1.0) * ka_ref[...])

    tri, _ = _chunk_tri(tb, reverse)
    cs_s[...] = _dot_exact_lhs(tri, logw)
    lw_s[...] = logw
    kk_s[...] = kk
    ba_s[...] = kk * a_lr
    k2_s[...] = k2
    i_last = 0 if reverse else CHUNK - 1

    lane_h = _div(_iota2((CHUNK, L), 1), RWKV_HEAD)
    h_masks = [lane_h == h for h in range(G)]
    t_idx = _iota2((CHUNK, L), 0)
    s_idx = _mod(_iota2((CHUNK, L), 1), CHUNK)
    strict = (s_idx > t_idx) if reverse else (s_idx < t_idx)
    incl = (s_idx >= t_idx) if reverse else (s_idx <= t_idx)
    eye_p = jnp.where(s_idx == t_idx, 1.0, 0.0)
    bd_mask = _div(_iota2((L, L), 0), RWKV_HEAD) == _div(_iota2((L, L), 1), RWKV_HEAD)
    eye_l = jnp.where(_iota2((L, L), 0) == _iota2((L, L), 1), 1.0, 0.0)

    def blockdiag(z):
        return _blockdiag(z, h_masks)

    chains = [(c, gi) for c in range(nc) for gi in range(ngrp)]
    rows_of = lambda c: slice(c * CHUNK, (c + 1) * CHUNK)
    lanes_of = lambda gi: slice(gi * L, (gi + 1) * L)
    AT, BT, KT, RT, BD, KD, GL, V = ([] for _ in range(8))
    for c in range(nc):
        rows = rows_of(c)
        cs = cs_s[rows, :]
        cs_tot = cs[i_last:i_last + 1, :]
        g_inv = jnp.exp(-cs)
        g_rem = jnp.exp(cs_tot - cs)
        at_c = (-kk_s[rows, :]) * jnp.exp(cs - lw_s[rows, :])
        bt_c = ba_s[rows, :] * g_inv
        kt_c = k2_s[rows, :] * g_inv
        rt_c = r_ref[rows, :] * jnp.exp(cs)
        bd_c = ba_s[rows, :] * g_rem
        kd_c = k2_s[rows, :] * g_rem
        gl_c = jnp.exp(cs_tot)
        for gi in range(ngrp):
            lanes = lanes_of(gi)
            AT.append(at_c[:, lanes])
            BT.append(bt_c[:, lanes])
            KT.append(kt_c[:, lanes])
            RT.append(rt_c[:, lanes])
            BD.append(bd_c[:, lanes])
            KD.append(kd_c[:, lanes])
            GL.append(gl_c[:, lanes])
            V.append(v_ref[rows, lanes])
    bf = lambda t: t.astype(BF16)
    cat0 = lambda a, b: jnp.concatenate([a, b], axis=0)
    cat1 = lambda a, b: jnp.concatenate([a, b], axis=1)
    X = [bf(cat0(a, r_)) for a, r_ in zip(AT, RT)]
    G1 = [_dot_nt(x, blockdiag(b)) for x, b in zip(X, BT)]
    G2 = [_dot_nt(x, blockdiag(k_)) for x, k_ in zip(X, KT)]
    LAB = [jnp.where(strict, g[0:CHUNK], 0.0) for g in G1]
    MRB = [jnp.where(incl, g[CHUNK:2 * CHUNK], 0.0) for g in G1]
    LAK = [jnp.where(strict, g[0:CHUNK], 0.0) for g in G2]
    MRK = [jnp.where(incl, g[CHUNK:2 * CHUNK], 0.0) for g in G2]
    TM = [eye_p + l for l in LAB]
    XP = [_dot(bf(l), blockdiag(l)) for l in LAB]
    for _ in range(4):
        Y = [_dot(bf(cat0(t, x)), blockdiag(x)) for t, x in zip(TM, XP)]
        TM = [t + y[0:CHUNK] for t, y in zip(TM, Y)]
        XP = [y[CHUNK:2 * CHUNK] for y in Y]
    TM = [t + _dot(bf(t), blockdiag(x)) for t, x in zip(TM, XP)]
    VBD = [blockdiag(v) for v in V]
    LAKV = [_dot(bf(l), vb) for l, vb in zip(LAK, VBD)]
    WU = [_dot(bf(t), cat1(blockdiag(a), blockdiag(lv))) for t, a, lv in zip(TM, AT, LAKV)]
    WA = [wu[:, 0:L] for wu in WU]
    UV = [wu[:, L:2 * L] for wu in WU]
    RO = [_dot(bf(m), cat1(blockdiag(wa), blockdiag(uv))) for m, wa, uv in zip(MRB, WA, UV)]
    MV = [_dot(bf(m), vb) for m, vb in zip(MRK, VBD)]
    GH = [_dot_tn(bf(bd), bf(cat1(wa, uv))) for bd, wa, uv in zip(BD, WA, UV)]
    KV = [_dot_tn(bf(kd), bf(v)) for kd, v in zip(KD, V)]
    for n_, (c, gi) in enumerate(chains):
        rows, lanes = rows_of(c), lanes_of(gi)
        reff_s[rows, lanes] = RT[n_] + RO[n_][:, 0:L]
        oloc_s[rows, lanes] = RO[n_][:, L:2 * L] + MV[n_]
        gm_s[c, gi] = bf(eye_l * GL[n_] + jnp.where(bd_mask, GH[n_][:, 0:L], 0.0))
        hm_s[c, gi] = jnp.where(bd_mask, GH[n_][:, L:2 * L] + KV[n_], 0.0)

    for ci in range(nc):
        c = (nc - 1 - ci) if reverse else ci
        rows = slice(c * CHUNK, (c + 1) * CHUNK)
        for gi in range(ngrp):
            lanes = slice(gi * L, (gi + 1) * L)
            pst = p_ref[gi].astype(BF16)
            y_s[rows, lanes] = _dot(reff_s[rows, lanes].astype(BF16), pst) + oloc_s[rows, lanes]
            p_ref[gi] = _dot(gm_s[c, gi], pst) + hm_s[c, gi]

    if final:
        y = y_s[...] + yf_ref[...]
        inv_n = 1.0 / RWKV_HEAD
        mu = _dot_exact_rhs(y, head_ones) * inv_n
        yc = y - mu
        var = _dot_exact_rhs(yc * yc, head_ones) * inv_n
        yn = yc * lax.rsqrt(var + RWKV_GN_EPS) * lnw_ref[...] + lnb_ref[...]
        bonus = _dot_exact_rhs(r * k2 * rk_ref[...], head_ones) * v_ref[...]
        gl = rs_ref[:, RWKV_DECAY_LORA + RWKV_AAA_LORA:]
        g = _dot(_sigmoid(gl).astype(BF16), g2_ref[...])
        o_ref[...] = (yn + bonus) * g


def _rwkv(p, w0, w2p, a0, a2p, k_k, k_a, batch, seq_len, reverse,
          y_fwd=None, g2=None, r_k=None, ln_w=None, ln_b=None):
    final = y_fwd is not None
    tb = min(SEQ_TILE, seq_len)
    nt = seq_len // tb
    n = batch * seq_len
    L = RWKV_GROUP * RWKV_HEAD
    ngrp = RWKV_W // L
    lora = RWKV_DECAY_LORA + RWKV_AAA_LORA

    def rowblk(b, j):
        return b * nt + ((nt - 1 - j) if reverse else j)

    vec = pl.BlockSpec((1, RWKV_W), lambda b, j: (0, 0))
    in_specs = [
        pl.BlockSpec((tb, 512), lambda b, j: (rowblk(b, j), COL_R // 512)),
        pl.BlockSpec((tb, 512), lambda b, j: (rowblk(b, j), COL_RK // 512)),
        pl.BlockSpec((tb, 512), lambda b, j: (rowblk(b, j), COL_RV // 512)),
        pl.BlockSpec((tb, 256), lambda b, j: (rowblk(b, j), COL_RS // 256)),
        vec,
        pl.BlockSpec((lora, RWKV_W), lambda b, j: (0, 0)),
        vec,
        pl.BlockSpec((lora, RWKV_W), lambda b, j: (0, 0)),
        vec, vec,
    ]
    args = [p, p, p, p, w0, w2p, a0, a2p, k_k, k_a]
    nc = tb // CHUNK
    scratch = ([pltpu.VMEM((ngrp, L, L), F32)] + [pltpu.VMEM((tb, RWKV_W), F32)] * 7
               + [pltpu.VMEM((nc, ngrp, L, L), BF16), pltpu.VMEM((nc, ngrp, L, L), F32)])
    if final:
        in_specs += [
            pl.BlockSpec((tb, RWKV_W), lambda b, j: (rowblk(b, j), 0)),
            pl.BlockSpec((RWKV_GATE_LORA, RWKV_W), lambda b, j: (0, 0)),
            vec, vec, vec,
        ]
        args += [y_fwd, g2, r_k, ln_w, ln_b]
        scratch.append(pltpu.VMEM((tb, RWKV_W), F32))
    kern = functools.partial(_rwkv_kernel, reverse=reverse, tb=tb, final=final)
    return pl.pallas_call(
        kern,
        grid=(batch, nt),
        in_specs=in_specs,
        out_specs=pl.BlockSpec((tb, RWKV_W), lambda b, j: (rowblk(b, j), 0)),
        out_shape=jax.ShapeDtypeStruct((n, RWKV_W), F32),
        scratch_shapes=scratch,
        compiler_params=pltpu.CompilerParams(
            dimension_semantics=("arbitrary", "arbitrary"),
            vmem_limit_bytes=VMEM_LIMIT),
        name="rwkv_bwd" if reverse else "rwkv_fwd",
    )(*args)


def _merge_kernel(x_ref, oa_ref, ob_ref, ga_ref, gb_ref, wa_ref, wb_ref, wo_ref, o_ref):
    ya = _dot(oa_ref[...].astype(BF16), wa_ref[...])
    yb = _dot(ob_ref[...].astype(BF16), wb_ref[...])
    merged = _sigmoid(ga_ref[...]) * ya + _sigmoid(gb_ref[...]) * yb
    o_ref[...] = x_ref[...] + _dot(merged.astype(BF16), wo_ref[...])


def _merge(x2, oa, ob, p, gla_proj, rwkv_proj, w_out, seq_len):
    n = x2.shape[0]
    tm = min(ROW_TILE, seq_len)
    return pl.pallas_call(
        _merge_kernel,
        grid=(n // tm,),
        in_specs=[
            pl.BlockSpec((tm, D_MODEL), lambda i: (i, 0)),
            pl.BlockSpec((tm, GLA_V_W), lambda i: (i, 0)),
            pl.BlockSpec((tm, RWKV_W), lambda i: (i, 0)),
            pl.BlockSpec((tm, D_MODEL), lambda i: (i, COL_GA // D_MODEL)),
            pl.BlockSpec((tm, D_MODEL), lambda i: (i, COL_GB // D_MODEL)),
            pl.BlockSpec((GLA_V_W, D_MODEL), lambda i: (0, 0)),
            pl.BlockSpec((RWKV_W, D_MODEL), lambda i: (0, 0)),
            pl.BlockSpec((D_MODEL, D_MODEL), lambda i: (0, 0)),
        ],
        out_specs=pl.BlockSpec((tm, D_MODEL), lambda i: (i, 0)),
        out_shape=jax.ShapeDtypeStruct((n, D_MODEL), F32),
        compiler_params=pltpu.CompilerParams(
            dimension_semantics=("arbitrary",),
            vmem_limit_bytes=VMEM_LIMIT),
        name="merge",
    )(x2, oa, ob, p, p, gla_proj, rwkv_proj, w_out)


def _ffn_kernel(xp_ref, x_ref, xn_ref, g2_ref, wu_ref, cw_ref, cb_ref, wd_ref, gf_ref, o_ref,
                h_ref, u_ref, acc_ref, *, tm, tiles_per_seq):
    i = pl.program_id(0)
    pos = lax.rem(i, tiles_per_seq)
    keep_prev = jnp.where(pos == 0, 0.0, 1.0)
    keep_next = jnp.where(pos == tiles_per_seq - 1, 0.0, 1.0)
    g2 = g2_ref[...]
    h_ref[0:HALO, :] = _rmsnorm(xp_ref[...] * keep_prev, g2).astype(BF16)
    h_ref[HALO:HALO + tm, :] = _rmsnorm(x_ref[...], g2).astype(BF16)
    h_ref[HALO + tm:HALO + tm + HALO, :] = _rmsnorm(xn_ref[...] * keep_next, g2).astype(BF16)
    acc_ref[...] = jnp.zeros_like(acc_ref)

    def chunk_body(c, carry):
        u_ref[...] = _dot(h_ref[...], wu_ref[c])
        cw = cw_ref[c]
        uc = (cw[0:1, :] * u_ref[HALO - 1:HALO - 1 + tm, :]
              + cw[1:2, :] * u_ref[HALO:HALO + tm, :]
              + cw[2:3, :] * u_ref[HALO + 1:HALO + 1 + tm, :]
              + cb_ref[c])
        ug = uc[:, 0:FF_CHUNK]
        act = (ug * _sigmoid(ug)) * uc[:, FF_CHUNK:2 * FF_CHUNK]
        acc_ref[...] += _dot(act.astype(BF16), wd_ref[c])
        return carry

    lax.fori_loop(0, FF_NCHUNK, chunk_body, 0)
    o_ref[...] = _rmsnorm(x_ref[...] + acc_ref[...], gf_ref[...])


def _ffn(x1, g2, wu_p, cw_p, cb_p, wd_p, gf, seq_len):
    n = x1.shape[0]
    tm = min(FFN_ROW_TILE, seq_len)
    tiles_per_seq = seq_len // tm
    hb = tm // HALO
    nhb = n // HALO
    kern = functools.partial(_ffn_kernel, tm=tm, tiles_per_seq=tiles_per_seq)
    return pl.pallas_call(
        kern,
        grid=(n // tm,),
        in_specs=[
            pl.BlockSpec((HALO, D_MODEL), lambda i: (jnp.maximum(i * hb - 1, 0), 0)),
            pl.BlockSpec((tm, D_MODEL), lambda i: (i, 0)),
            pl.BlockSpec((HALO, D_MODEL), lambda i: (jnp.minimum((i + 1) * hb, nhb - 1), 0)),
            pl.BlockSpec((1, D_MODEL), lambda i: (0, 0)),
            pl.BlockSpec((FF_NCHUNK, D_MODEL, 2 * FF_CHUNK), lambda i: (0, 0, 0)),
            pl.BlockSpec((FF_NCHUNK, 3, 2 * FF_CHUNK), lambda i: (0, 0, 0)),
            pl.BlockSpec((FF_NCHUNK, 1, 2 * FF_CHUNK), lambda i: (0, 0, 0)),
            pl.BlockSpec((FF_NCHUNK, FF_CHUNK, D_MODEL), lambda i: (0, 0, 0)),
            pl.BlockSpec((1, D_MODEL), lambda i: (0, 0)),
        ],
        out_specs=pl.BlockSpec((tm, D_MODEL), lambda i: (i, 0)),
        out_shape=jax.ShapeDtypeStruct((n, D_MODEL), F32),
        scratch_shapes=[
            pltpu.VMEM((tm + 2 * HALO, D_MODEL), BF16),
            pltpu.VMEM((tm + 2 * HALO, 2 * FF_CHUNK), F32),
            pltpu.VMEM((tm, D_MODEL), F32),
        ],
        compiler_params=pltpu.CompilerParams(
            dimension_semantics=("arbitrary",),
            vmem_limit_bytes=VMEM_LIMIT),
        name="ffn",
    )(x1, x1, x1, g2, wu_p, cw_p, cb_p, wd_p, gf)


def _pack_w_in(w):
    gla_w = 2 * GLA_QK_W + 2 * GLA_V_W + 2 * GLA_GATE_RANK
    r0 = gla_w
    g0 = r0 + 3 * RWKV_W + RWKV_DECAY_LORA + RWKV_AAA_LORA + RWKV_GATE_LORA
    pad = jnp.zeros((w.shape[0], 256 - 2 * GLA_GATE_RANK), w.dtype)
    return jnp.concatenate([
        w[:, g0:g0 + 2 * D_MODEL],
        w[:, 0:2 * GLA_QK_W + 2 * GLA_V_W],
        w[:, r0:r0 + 3 * RWKV_W + 256],
        w[:, 2 * GLA_QK_W + 2 * GLA_V_W:gla_w],
        pad], axis=1)


def _pack_mu(mu):
    z = jnp.zeros((COL_R,), mu.dtype)
    z2 = jnp.zeros((P_COLS - COL_GS,), mu.dtype)
    return jnp.concatenate([z, mu, z2])[None, :]


def _pack_ff_cols(t):
    pad = [(0, 0)] * (t.ndim - 1) + [(0, FF_PAD - D_FF)]
    g = jnp.pad(t[..., :D_FF], pad)
    v = jnp.pad(t[..., D_FF:], pad)
    lead = t.shape[:-1]
    g = g.reshape(lead + (FF_NCHUNK, FF_CHUNK))
    v = v.reshape(lead + (FF_NCHUNK, FF_CHUNK))
    gv = jnp.concatenate([g, v], axis=-1)
    return jnp.moveaxis(gv, -2, 0)


def kernel(x, norm1_g, w_in, gla_wa2_f, gla_ba_f, gla_wa2_b, gla_ba_b, gla_norm_g, gla_proj, rwkv_mu_prev, rwkv_mu_next, rwkv_w0_f, rwkv_w2_f, rwkv_w0_b, rwkv_w2_b, rwkv_a0, rwkv_a2, rwkv_g2, rwkv_k_k, rwkv_k_a, rwkv_r_k, rwkv_ln_w, rwkv_ln_b, rwkv_proj, w_out, norm2_g, ffn_up, ffn_conv_w, ffn_conv_b, ffn_down, norm_f_g):
    batch, seq_len, d = x.shape
    assert w_in.shape[0] == 1 and d == D_MODEL and seq_len % CHUNK == 0
    x2 = x.reshape(batch * seq_len, d)
    row = lambda t: t.reshape(1, -1)
    for l in range(1):
        w_packed = _pack_w_in(w_in[l]).astype(BF16)
        wa2_f = jnp.pad(gla_wa2_f[l], ((0, 256 - GLA_GATE_RANK), (0, 0)))
        wa2_b = jnp.pad(gla_wa2_b[l], ((GLA_GATE_RANK, 256 - 2 * GLA_GATE_RANK), (0, 0)))
        zl = jnp.zeros((RWKV_AAA_LORA, RWKV_W), F32)
        w2_f = jnp.concatenate([rwkv_w2_f[l], zl], axis=0)
        w2_b = jnp.concatenate([rwkv_w2_b[l], zl], axis=0)
        a2p = jnp.concatenate([jnp.zeros((RWKV_DECAY_LORA, RWKV_W), F32), rwkv_a2[l]], axis=0).astype(BF16)

        p = _inproj(x2, row(norm1_g[l]), w_packed, _pack_mu(rwkv_mu_prev[l]),
                    _pack_mu(rwkv_mu_next[l]), seq_len)

        o_f = _gla(p, wa2_f, row(gla_ba_f[l]), batch, seq_len, reverse=False)
        oa = _gla(p, wa2_b, row(gla_ba_b[l]), batch, seq_len, reverse=True,
                  o_fwd=o_f, norm_g=row(gla_norm_g[l]))

        rw_common = (row(rwkv_a0[l]), a2p, row(rwkv_k_k[l]), row(rwkv_k_a[l]))
        y_f = _rwkv(p, row(rwkv_w0_f[l]), w2_f, *rw_common, batch, seq_len, reverse=False)
        ob = _rwkv(p, row(rwkv_w0_b[l]), w2_b, *rw_common, batch, seq_len, reverse=True,
                   y_fwd=y_f, g2=rwkv_g2[l].astype(BF16), r_k=row(rwkv_r_k[l]),
                   ln_w=row(rwkv_ln_w[l]), ln_b=row(rwkv_ln_b[l]))

        x1 = _merge(x2, oa, ob, p, gla_proj[l].astype(BF16), rwkv_proj[l].astype(BF16),
                    w_out[l].astype(BF16), seq_len)

        wu_p = _pack_ff_cols(ffn_up[l]).astype(BF16)
        cw_p = _pack_ff_cols(ffn_conv_w[l])
        cb_p = _pack_ff_cols(ffn_conv_b[l][None, :])
        wd_p = jnp.pad(ffn_down[l], ((0, FF_PAD - D_FF), (0, 0))).reshape(
            FF_NCHUNK, FF_CHUNK, D_MODEL).astype(BF16)
        x2 = _ffn(x1, row(norm2_g[l]), wu_p, cw_p, cb_p, wd_p, row(norm_f_g), seq_len)
    return x2.reshape(batch, seq_len, d)
```

```python
import functools

import jax
import jax.numpy as jnp
from jax import lax
from jax.experimental import pallas as pl
from jax.experimental.pallas import tpu as pltpu

F32 = jnp.float32
BF16 = jnp.bfloat16

D_MODEL = 1024
GLA_HEADS = 4
GLA_DK = 64
GLA_DV = 128
GLA_QK_W = GLA_HEADS * GLA_DK
GLA_V_W = GLA_HEADS * GLA_DV
GLA_GATE_RANK = 16
GLA_LOGIT_NORM = 16.0
CHUNK = 64
RWKV_HEAD = 64
RWKV_W = 512
RWKV_HEADS = RWKV_W // RWKV_HEAD
RWKV_DECAY_LORA = 64
RWKV_AAA_LORA = 64
RWKV_GATE_LORA = 128
RWKV_GN_EPS = RWKV_HEAD * 1e-5
D_FF = 2752
NORM_EPS = 1e-6
HEAD_NORM_EPS = 1e-5

COL_GA, COL_GB = 0, 1024
COL_QK, COL_V, COL_OG = 2048, 2560, 3072
COL_R, COL_RK, COL_RV = 3584, 4096, 4608
COL_RS, COL_GS = 5120, 5376
P_COLS = 5632
HALO = 16

FF_CHUNK = 256
FF_PAD = 2816
FF_NCHUNK = FF_PAD // FF_CHUNK

VMEM_LIMIT = 56 * 1024 * 1024
ROW_TILE = 1024
FFN_ROW_TILE = 512
SEQ_TILE = 256
RWKV_GROUP = 4


def _sigmoid(x):
    return 1.0 / (1.0 + jnp.exp(-x))


def _softplus(z):
    return jnp.maximum(z, 0.0) + jnp.log(1.0 + jnp.exp(-jnp.abs(z)))


def _split3(x):
    hi = x.astype(BF16)
    r1 = x - hi.astype(F32)
    mid = r1.astype(BF16)
    lo = (r1 - mid.astype(F32)).astype(BF16)
    return hi, mid, lo


def _dot(a, b):
    return jnp.dot(a, b, preferred_element_type=F32)


def _dot_nt(a, b):
    return lax.dot_general(a, b, (((1,), (1,)), ((), ())), preferred_element_type=F32)


def _dot_tn(a, b):
    return lax.dot_general(a, b, (((0,), (0,)), ((), ())), preferred_element_type=F32)


def _dot_exact_lhs(sel, x):
    hi = x.astype(BF16)
    lo = (x - hi.astype(F32)).astype(BF16)
    s = sel.astype(BF16)
    return _dot(s, hi) + _dot(s, lo)


def _dot_exact_rhs(x, sel):
    hi = x.astype(BF16)
    lo = (x - hi.astype(F32)).astype(BF16)
    s = sel.astype(BF16)
    return _dot(hi, s) + _dot(lo, s)


def _dot_x3(a, b):
    ah = a.astype(BF16)
    al = (a - ah.astype(F32)).astype(BF16)
    bh = b.astype(BF16)
    bl = (b - bh.astype(F32)).astype(BF16)
    return _dot(ah, bh) + _dot(al, bh) + _dot(ah, bl)


def _rmsnorm(x, g):
    ms = jnp.mean(x * x, axis=-1, keepdims=True)
    return x * lax.rsqrt(ms + NORM_EPS) * g


def _iota2(shape, dim):
    return lax.broadcasted_iota(jnp.int32, shape, dim)


def _div(x, size):
    return lax.shift_right_logical(x, size.bit_length() - 1)


def _mod(x, size):
    return lax.bitwise_and(x, size - 1)


def _chunk_tri(tb, reverse):
    row = _iota2((tb, tb), 0)
    col = _iota2((tb, tb), 1)
    same = _div(row, CHUNK) == _div(col, CHUNK)
    tri = jnp.where(same & ((col >= row) if reverse else (col <= row)), 1.0, 0.0)
    return tri, jnp.where(same, 1.0, 0.0)


def _blockdiag(z, head_masks):
    return jnp.concatenate([jnp.where(m, z, 0.0) for m in head_masks], axis=0).astype(BF16)


def _inproj_kernel(xp_ref, x_ref, xn_ref, g_ref, w_ref, mup_ref, mun_ref, o_ref,
                   h_ref, p_ref, *, tm, tiles_per_seq, first_shift_tile, n_sub):
    i = pl.program_id(0)
    j = pl.program_id(1)

    @pl.when(j == 0)
    def _():
        pos = lax.rem(i, tiles_per_seq)
        keep_prev = jnp.where(pos == 0, 0.0, 1.0)
        keep_next = jnp.where(pos == tiles_per_seq - 1, 0.0, 1.0)
        g = g_ref[...]
        h_ref[0:HALO, :] = _rmsnorm(xp_ref[...] * keep_prev, g).astype(BF16)
        h_ref[HALO:HALO + tm, :] = _rmsnorm(x_ref[...], g).astype(BF16)
        h_ref[HALO + tm:HALO + tm + HALO, :] = _rmsnorm(xn_ref[...] * keep_next, g).astype(BF16)

    @pl.when(j < first_shift_tile)
    def _():
        o_ref[...] = _dot(h_ref[HALO:HALO + tm, :], w_ref[...])

    @pl.when(j >= first_shift_tile)
    def _():
        sub = tm // n_sub
        mm = lambda s: _dot(h_ref[s * sub:(s + 1) * sub + 2 * HALO, :], w_ref[...])
        p_next = mm(0)
        for s in range(n_sub):
            slot = s % 2
            p_ref[slot] = p_next
            if s + 1 < n_sub:
                p_next = mm(s + 1)
            p = p_ref[slot, HALO:HALO + sub, :]
            pp = p_ref[slot, HALO - 1:HALO - 1 + sub, :]
            pn = p_ref[slot, HALO + 1:HALO + 1 + sub, :]
            o_ref[s * sub:(s + 1) * sub, :] = (
                p + mup_ref[...] * (pp - p) + mun_ref[...] * (pn - p))


def _inproj(x2, g, w_packed, mu_prev, mu_next, seq_len):
    n = x2.shape[0]
    tm = min(ROW_TILE, seq_len)
    tn = 512
    tiles_per_seq = seq_len // tm
    hb = tm // HALO
    nhb = n // HALO
    n_sub = max(1, tm // 256)
    sub = tm // n_sub
    kern = functools.partial(_inproj_kernel, tm=tm, tiles_per_seq=tiles_per_seq,
                             first_shift_tile=COL_R // tn, n_sub=n_sub)
    return pl.pallas_call(
        kern,
        grid=(n // tm, P_COLS // tn),
        in_specs=[
            pl.BlockSpec((HALO, D_MODEL), lambda i, j: (jnp.maximum(i * hb - 1, 0), 0)),
            pl.BlockSpec((tm, D_MODEL), lambda i, j: (i, 0)),
            pl.BlockSpec((HALO, D_MODEL), lambda i, j: (jnp.minimum((i + 1) * hb, nhb - 1), 0)),
            pl.BlockSpec((1, D_MODEL), lambda i, j: (0, 0)),
            pl.BlockSpec((D_MODEL, tn), lambda i, j: (0, j)),
            pl.BlockSpec((1, tn), lambda i, j: (0, j)),
            pl.BlockSpec((1, tn), lambda i, j: (0, j)),
        ],
        out_specs=pl.BlockSpec((tm, tn), lambda i, j: (i, j)),
        out_shape=jax.ShapeDtypeStruct((n, P_COLS), F32),
        scratch_shapes=[
            pltpu.VMEM((tm + 2 * HALO, D_MODEL), BF16),
            pltpu.VMEM((2, sub + 2 * HALO, tn), F32),
        ],
        compiler_params=pltpu.CompilerParams(
            dimension_semantics=("arbitrary", "arbitrary"),
            vmem_limit_bytes=VMEM_LIMIT),
        name="inproj",
    )(x2, x2, x2, g, w_packed, mu_prev, mu_next)


def _gla_kernel(*refs, reverse, tb, final):
    if final:
        (qk_ref, v_ref, gs_ref, wa2_ref, ba_ref, of_ref, og_ref, ng_ref,
         o_ref, st_ref, b_ref, acc_ref) = refs
    else:
        qk_ref, v_ref, gs_ref, wa2_ref, ba_ref, o_ref, st_ref, b_ref = refs
    j = pl.program_id(1)
    nc = tb // CHUNK

    @pl.when(j == 0)
    def _():
        st_ref[...] = jnp.zeros_like(st_ref)

    logit = _dot_x3(gs_ref[...], wa2_ref[...]) + ba_ref[...]
    la = (jnp.minimum(logit, 0.0) - jnp.log(1.0 + jnp.exp(-jnp.abs(logit)))) * (1.0 / GLA_LOGIT_NORM)
    tri, _ = _chunk_tri(tb, reverse)
    b_ref[...] = _dot_exact_lhs(tri, la)

    lane_k = _div(_iota2((CHUNK, GLA_QK_W), 1), GLA_DK)
    lane_v = _div(_iota2((CHUNK, GLA_V_W), 1), GLA_DV)
    k_masks = [lane_k == h for h in range(GLA_HEADS)]
    v_masks = [lane_v == h for h in range(GLA_HEADS)]
    t_idx = _iota2((CHUNK, GLA_QK_W), 0)
    s_idx = _mod(_iota2((CHUNK, GLA_QK_W), 1), CHUNK)
    causal = (s_idx >= t_idx) if reverse else (s_idx <= t_idx)
    st_mask = (_div(_iota2((GLA_V_W, GLA_QK_W), 0), GLA_DV)
               == _div(_iota2((GLA_V_W, GLA_QK_W), 1), GLA_DK))
    i_ref = (CHUNK - 1 - CHUNK // 2) if reverse else CHUNK // 2
    i_last = 0 if reverse else CHUNK - 1

    QI, KD, QE, VB, KI_BD, V_BD, DEC = ([] for _ in range(7))
    for c in range(nc):
        rows = slice(c * CHUNK, (c + 1) * CHUNK)
        b = b_ref[rows, :]
        b_mid = b[i_ref:i_ref + 1, :]
        b_last = b[i_last:i_last + 1, :]
        q = qk_ref[rows, 0:GLA_QK_W] * (GLA_DK ** -0.5)
        k = qk_ref[rows, GLA_QK_W:2 * GLA_QK_W]
        v = v_ref[rows, :]
        QI.append((q * jnp.exp(b - b_mid)).astype(BF16))
        KD.append((k * jnp.exp(b_last - b)).astype(BF16))
        QE.append((q * jnp.exp(b)).astype(BF16))
        VB.append(v.astype(BF16))
        KI_BD.append(_blockdiag(k * jnp.exp(b_mid - b), k_masks))
        V_BD.append(_blockdiag(v, v_masks))
        DEC.append(jnp.exp(b_last))
    A = [jnp.where(causal, _dot_nt(qi, kb), 0.0) for qi, kb in zip(QI, KI_BD)]
    O_IN = [_dot(a.astype(BF16), vb) for a, vb in zip(A, V_BD)]
    KV = [jnp.where(st_mask, _dot_tn(vb, kd), 0.0) for vb, kd in zip(VB, KD)]
    for ci in range(nc):
        c = (nc - 1 - ci) if reverse else ci
        rows = slice(c * CHUNK, (c + 1) * CHUNK)
        st = st_ref[...]
        o = O_IN[c] + _dot_nt(QE[c], st.astype(BF16))
        st_ref[...] = st * DEC[c] + KV[c]
        if final:
            acc_ref[rows, :] = of_ref[rows, :] + o
        else:
            o_ref[rows, :] = o

    if final:
        o = acc_ref[...]
        parts = []
        for h in range(GLA_HEADS):
            oh = o[:, h * GLA_DV:(h + 1) * GLA_DV]
            ms = jnp.mean(oh * oh, axis=-1, keepdims=True)
            parts.append(oh * lax.rsqrt(ms + HEAD_NORM_EPS))
        on = jnp.concatenate(parts, axis=1) * ng_ref[...]
        og = og_ref[...]
        o_ref[...] = on * (og * _sigmoid(og))


def _gla(p, wa2p, ba, batch, seq_len, reverse, o_fwd=None, norm_g=None):
    final = o_fwd is not None
    tb = min(SEQ_TILE, seq_len)
    nt = seq_len // tb
    n = batch * seq_len

    def rowblk(b, j):
        return b * nt + ((nt - 1 - j) if reverse else j)

    in_specs = [
        pl.BlockSpec((tb, 512), lambda b, j: (rowblk(b, j), COL_QK // 512)),
        pl.BlockSpec((tb, 512), lambda b, j: (rowblk(b, j), COL_V // 512)),
        pl.BlockSpec((tb, 256), lambda b, j: (rowblk(b, j), COL_GS // 256)),
        pl.BlockSpec((256, GLA_QK_W), lambda b, j: (0, 0)),
        pl.BlockSpec((1, GLA_QK_W), lambda b, j: (0, 0)),
    ]
    args = [p, p, p, wa2p, ba]
    scratch = [pltpu.VMEM((GLA_V_W, GLA_QK_W), F32), pltpu.VMEM((tb, GLA_QK_W), F32)]
    if final:
        in_specs += [
            pl.BlockSpec((tb, GLA_V_W), lambda b, j: (rowblk(b, j), 0)),
            pl.BlockSpec((tb, 512), lambda b, j: (rowblk(b, j), COL_OG // 512)),
            pl.BlockSpec((1, GLA_V_W), lambda b, j: (0, 0)),
        ]
        args += [o_fwd, p, norm_g]
        scratch.append(pltpu.VMEM((tb, GLA_V_W), F32))
    kern = functools.partial(_gla_kernel, reverse=reverse, tb=tb, final=final)
    return pl.pallas_call(
        kern,
        grid=(batch, nt),
        in_specs=in_specs,
        out_specs=pl.BlockSpec((tb, GLA_V_W), lambda b, j: (rowblk(b, j), 0)),
        out_shape=jax.ShapeDtypeStruct((n, GLA_V_W), F32),
        scratch_shapes=scratch,
        compiler_params=pltpu.CompilerParams(
            dimension_semantics=("arbitrary", "arbitrary"),
            vmem_limit_bytes=VMEM_LIMIT),
        name="gla_bwd" if reverse else "gla_fwd",
    )(*args)


def _rwkv_kernel(*refs, reverse, tb, final):
    if final:
        (r_ref, k_ref, v_ref, rs_ref, w0_ref, w2_ref, a0_ref, a2_ref, kk_ref, ka_ref,
         yf_ref, g2_ref, rk_ref, lnw_ref, lnb_ref,
         o_ref, p_ref, cs_s, lw_s, kk_s, ba_s, k2_s, reff_s, oloc_s, gm_s, hm_s, y_s) = refs
    else:
        (r_ref, k_ref, v_ref, rs_ref, w0_ref, w2_ref, a0_ref, a2_ref, kk_ref, ka_ref,
         o_ref, p_ref, cs_s, lw_s, kk_s, ba_s, k2_s, reff_s, oloc_s, gm_s, hm_s) = refs
        y_s = o_ref
    j = pl.program_id(1)
    nc = tb // CHUNK
    G = RWKV_GROUP
    L = G * RWKV_HEAD
    ngrp = RWKV_W // L

    @pl.when(j == 0)
    def _():
        p_ref[...] = jnp.zeros_like(p_ref)

    r = r_ref[...]
    k = k_ref[...]
    lora_in = rs_ref[:, 0:RWKV_DECAY_LORA + RWKV_AAA_LORA]
    w_pre = w0_ref[...] + _dot_x3(jnp.tanh(lora_in), w2_ref[...])
    logw = -jnp.exp(-_softplus(-w_pre) - 0.5)
    a_lr = _sigmoid(a0_ref[...] + _dot(lora_in.astype(BF16), a2_ref[...]))
    hr = _div(_iota2((RWKV_W, RWKV_W), 0), RWKV_HEAD)
    hc = _div(_iota2((RWKV_W, RWKV_W), 1), RWKV_HEAD)
    head_ones = jnp.where(hr == hc, 1.0, 0.0)
    kk = k * kk_ref[...]
    ssq = _dot_exact_rhs(kk * kk, head_ones)
    kk = kk / jnp.maximum(jnp.sqrt(ssq), 1e-12)
    k2 = k * (1.0 + (a_lr - 1.0) * ka_ref[...])

    tri, _ = _chunk_tri(tb, reverse)
    cs_s[...] = _dot_exact_lhs(tri, logw)
    lw_s[...] = logw
    kk_s[...] = kk
    ba_s[...] = kk * a_lr
    k2_s[...] = k2
    i_last = 0 if reverse else CHUNK - 1

    lane_h = _div(_iota2((CHUNK, L), 1), RWKV_HEAD)
    h_masks = [lane_h == h for h in range(G)]
    t_idx = _iota2((CHUNK, L), 0)
    s_idx = _mod(_iota2((CHUNK, L), 1), CHUNK)
    strict = (s_idx > t_idx) if reverse else (s_idx < t_idx)
    incl = (s_idx >= t_idx) if reverse else (s_idx <= t_idx)
    eye_p = jnp.where(s_idx == t_idx, 1.0, 0.0)
    bd_mask = _div(_iota2((L, L), 0), RWKV_HEAD) == _div(_iota2((L, L), 1), RWKV_HEAD)
    eye_l = jnp.where(_iota2((L, L), 0) == _iota2((L, L), 1), 1.0, 0.0)

    def blockdiag(z):
        return _blockdiag(z, h_masks)

    chains = [(c, gi) for c in range(nc) for gi in range(ngrp)]
    rows_of = lambda c: slice(c * CHUNK, (c + 1) * CHUNK)
    lanes_of = lambda gi: slice(gi * L, (gi + 1) * L)
    AT, BT, KT, RT, BD, KD, GL, V = ([] for _ in range(8))
    for c in range(nc):
        rows = rows_of(c)
        cs = cs_s[rows, :]
        cs_tot = cs[i_last:i_last + 1, :]
        g_inv = jnp.exp(-cs)
        g_rem = jnp.exp(cs_tot - cs)
        at_c = (-kk_s[rows, :]) * jnp.exp(cs - lw_s[rows, :])
        bt_c = ba_s[rows, :] * g_inv
        kt_c = k2_s[rows, :] * g_inv
        rt_c = r_ref[rows, :] * jnp.exp(cs)
        bd_c = ba_s[rows, :] * g_rem
        kd_c = k2_s[rows, :] * g_rem
        gl_c = jnp.exp(cs_tot)
        for gi in range(ngrp):
            lanes = lanes_of(gi)
            AT.append(at_c[:, lanes])
            BT.append(bt_c[:, lanes])
            KT.append(kt_c[:, lanes])
            RT.append(rt_c[:, lanes])
            BD.append(bd_c[:, lanes])
            KD.append(kd_c[:, lanes])
            GL.append(gl_c[:, lanes])
            V.append(v_ref[rows, lanes])
    bf = lambda t: t.astype(BF16)
    cat0 = lambda a, b: jnp.concatenate([a, b], axis=0)
    cat1 = lambda a, b: jnp.concatenate([a, b], axis=1)
    X = [bf(cat0(a, r_)) for a, r_ in zip(AT, RT)]
    G1 = [_dot_nt(x, blockdiag(b)) for x, b in zip(X, BT)]
    G2 = [_dot_nt(x, blockdiag(k_)) for x, k_ in zip(X, KT)]
    LAB = [jnp.where(strict, g[0:CHUNK], 0.0) for g in G1]
    MRB = [jnp.where(incl, g[CHUNK:2 * CHUNK], 0.0) for g in G1]
    LAK = [jnp.where(strict, g[0:CHUNK], 0.0) for g in G2]
    MRK = [jnp.where(incl, g[CHUNK:2 * CHUNK], 0.0) for g in G2]
    TM = [eye_p + l for l in LAB]
    XP = [_dot(bf(l), blockdiag(l)) for l in LAB]
    for _ in range(4):
        Y = [_dot(bf(cat0(t, x)), blockdiag(x)) for t, x in zip(TM, XP)]
        TM = [t + y[0:CHUNK] for t, y in zip(TM, Y)]
        XP = [y[CHUNK:2 * CHUNK] for y in Y]
    TM = [t + _dot(bf(t), blockdiag(x)) for t, x in zip(TM, XP)]
    VBD = [blockdiag(v) for v in V]
    LAKV = [_dot(bf(l), vb) for l, vb in zip(LAK, VBD)]
    WU = [_dot(bf(t), cat1(blockdiag(a), blockdiag(lv))) for t, a, lv in zip(TM, AT, LAKV)]
    WA = [wu[:, 0:L] for wu in WU]
    UV = [wu[:, L:2 * L] for wu in WU]
    RO = [_dot(bf(m), cat1(blockdiag(wa), blockdiag(uv))) for m, wa, uv in zip(MRB, WA, UV)]
    MV = [_dot(bf(m), vb) for m, vb in zip(MRK, VBD)]
    GH = [_dot_tn(bf(bd), bf(cat1(wa, uv))) for bd, wa, uv in zip(BD, WA, UV)]
    KV = [_dot_tn(bf(kd), bf(v)) for kd, v in zip(KD, V)]
    for n_, (c, gi) in enumerate(chains):
        rows, lanes = rows_of(c), lanes_of(gi)
        reff_s[rows, lanes] = RT[n_] + RO[n_][:, 0:L]
        oloc_s[rows, lanes] = RO[n_][:, L:2 * L] + MV[n_]
        gm_s[c, gi] = bf(eye_l * GL[n_] + jnp.where(bd_mask, GH[n_][:, 0:L], 0.0))
        hm_s[c, gi] = jnp.where(bd_mask, GH[n_][:, L:2 * L] + KV[n_], 0.0)

    for ci in range(nc):
        c = (nc - 1 - ci) if reverse else ci
        rows = slice(c * CHUNK, (c + 1) * CHUNK)
        for gi in range(ngrp):
            lanes = slice(gi * L, (gi + 1) * L)
            pst = p_ref[gi].astype(BF16)
            y_s[rows, lanes] = _dot(reff_s[rows, lanes].astype(BF16), pst) + oloc_s[rows, lanes]
            p_ref[gi] = _dot(gm_s[c, gi], pst) + hm_s[c, gi]

    if final:
        y = y_s[...] + yf_ref[...]
        inv_n = 1.0 / RWKV_HEAD
        mu = _dot_exact_rhs(y, head_ones) * inv_n
        yc = y - mu
        var = _dot_exact_rhs(yc * yc, head_ones) * inv_n
        yn = yc * lax.rsqrt(var + RWKV_GN_EPS) * lnw_ref[...] + lnb_ref[...]
        bonus = _dot_exact_rhs(r * k2 * rk_ref[...], head_ones) * v_ref[...]
        gl = rs_ref[:, RWKV_DECAY_LORA + RWKV_AAA_LORA:]
        g = _dot(_sigmoid(gl).astype(BF16), g2_ref[...])
        o_ref[...] = (yn + bonus) * g


def _rwkv(p, w0, w2p, a0, a2p, k_k, k_a, batch, seq_len, reverse,
          y_fwd=None, g2=None, r_k=None, ln_w=None, ln_b=None):
    final = y_fwd is not None
    tb = min(SEQ_TILE, seq_len)
    nt = seq_len // tb
    n = batch * seq_len
    L = RWKV_GROUP * RWKV_HEAD
    ngrp = RWKV_W // L
    lora = RWKV_DECAY_LORA + RWKV_AAA_LORA

    def rowblk(b, j):
        return b * nt + ((nt - 1 - j) if reverse else j)

    vec = pl.BlockSpec((1, RWKV_W), lambda b, j: (0, 0))
    in_specs = [
        pl.BlockSpec((tb, 512), lambda b, j: (rowblk(b, j), COL_R // 512)),
        pl.BlockSpec((tb, 512), lambda b, j: (rowblk(b, j), COL_RK // 512)),
        pl.BlockSpec((tb, 512), lambda b, j: (rowblk(b, j), COL_RV // 512)),
        pl.BlockSpec((tb, 256), lambda b, j: (rowblk(b, j), COL_RS // 256)),
        vec,
        pl.BlockSpec((lora, RWKV_W), lambda b, j: (0, 0)),
        vec,
        pl.BlockSpec((lora, RWKV_W), lambda b, j: (0, 0)),
        vec, vec,
    ]
    args = [p, p, p, p, w0, w2p, a0, a2p, k_k, k_a]
    nc = tb // CHUNK
    scratch = ([pltpu.VMEM((ngrp, L, L), F32)] + [pltpu.VMEM((tb, RWKV_W), F32)] * 7
               + [pltpu.VMEM((nc, ngrp, L, L), BF16), pltpu.VMEM((nc, ngrp, L, L), F32)])
    if final:
        in_specs += [
            pl.BlockSpec((tb, RWKV_W), lambda b, j: (rowblk(b, j), 0)),
            pl.BlockSpec((RWKV_GATE_LORA, RWKV_W), lambda b, j: (0, 0)),
            vec, vec, vec,
        ]
        args += [y_fwd, g2, r_k, ln_w, ln_b]
        scratch.append(pltpu.VMEM((tb, RWKV_W), F32))
    kern = functools.partial(_rwkv_kernel, reverse=reverse, tb=tb, final=final)
    return pl.pallas_call(
        kern,
        grid=(batch, nt),
        in_specs=in_specs,
        out_specs=pl.BlockSpec((tb, RWKV_W), lambda b, j: (rowblk(b, j), 0)),
        out_shape=jax.ShapeDtypeStruct((n, RWKV_W), F32),
        scratch_shapes=scratch,
        compiler_params=pltpu.CompilerParams(
            dimension_semantics=("arbitrary", "arbitrary"),
            vmem_limit_bytes=VMEM_LIMIT),
        name="rwkv_bwd" if reverse else "rwkv_fwd",
    )(*args)


def _merge_kernel(x_ref, oa_ref, ob_ref, ga_ref, gb_ref, wa_ref, wb_ref, wo_ref, o_ref):
    ya = _dot(oa_ref[...].astype(BF16), wa_ref[...])
    yb = _dot(ob_ref[...].astype(BF16), wb_ref[...])
    merged = _sigmoid(ga_ref[...]) * ya + _sigmoid(gb_ref[...]) * yb
    o_ref[...] = x_ref[...] + _dot(merged.astype(BF16), wo_ref[...])


def _merge(x2, oa, ob, p, gla_proj, rwkv_proj, w_out, seq_len):
    n = x2.shape[0]
    tm = min(ROW_TILE, seq_len)
    return pl.pallas_call(
        _merge_kernel,
        grid=(n // tm,),
        in_specs=[
            pl.BlockSpec((tm, D_MODEL), lambda i: (i, 0)),
            pl.BlockSpec((tm, GLA_V_W), lambda i: (i, 0)),
            pl.BlockSpec((tm, RWKV_W), lambda i: (i, 0)),
            pl.BlockSpec((tm, D_MODEL), lambda i: (i, COL_GA // D_MODEL)),
            pl.BlockSpec((tm, D_MODEL), lambda i: (i, COL_GB // D_MODEL)),
            pl.BlockSpec((GLA_V_W, D_MODEL), lambda i: (0, 0)),
            pl.BlockSpec((RWKV_W, D_MODEL), lambda i: (0, 0)),
            pl.BlockSpec((D_MODEL, D_MODEL), lambda i: (0, 0)),
        ],
        out_specs=pl.BlockSpec((tm, D_MODEL), lambda i: (i, 0)),
        out_shape=jax.ShapeDtypeStruct((n, D_MODEL), F32),
        compiler_params=pltpu.CompilerParams(
            dimension_semantics=("arbitrary",),
            vmem_limit_bytes=VMEM_LIMIT),
        name="merge",
    )(x2, oa, ob, p, p, gla_proj, rwkv_proj, w_out)


def _ffn_kernel(xp_ref, x_ref, xn_ref, g2_ref, wu_ref, cw_ref, cb_ref, wd_ref, gf_ref, o_ref,
                h_ref, u_ref, acc_ref, *, tm, tiles_per_seq):
    i = pl.program_id(0)
    pos = lax.rem(i, tiles_per_seq)
    keep_prev = jnp.where(pos == 0, 0.0, 1.0)
    keep_next = jnp.where(pos == tiles_per_seq - 1, 0.0, 1.0)
    g2 = g2_ref[...]
    h_ref[0:HALO, :] = _rmsnorm(xp_ref[...] * keep_prev, g2).astype(BF16)
    h_ref[HALO:HALO + tm, :] = _rmsnorm(x_ref[...], g2).astype(BF16)
    h_ref[HALO + tm:HALO + tm + HALO, :] = _rmsnorm(xn_ref[...] * keep_next, g2).astype(BF16)

    up = lambda c: _dot(h_ref[...], wu_ref[c])
    u_next = up(0)
    for c in range(FF_NCHUNK):
        slot = c % 2
        u_ref[slot] = u_next
        if c + 1 < FF_NCHUNK:
            u_next = up(c + 1)
        cw = cw_ref[c]
        uc = (cw[0:1, :] * u_ref[slot, HALO - 1:HALO - 1 + tm, :]
              + cw[1:2, :] * u_ref[slot, HALO:HALO + tm, :]
              + cw[2:3, :] * u_ref[slot, HALO + 1:HALO + 1 + tm, :]
              + cb_ref[c])
        ug = uc[:, 0:FF_CHUNK]
        act = (ug * _sigmoid(ug)) * uc[:, FF_CHUNK:2 * FF_CHUNK]
        down = _dot(act.astype(BF16), wd_ref[c])
        if c == 0:
            acc_ref[...] = down
        else:
            acc_ref[...] += down
    o_ref[...] = _rmsnorm(x_ref[...] + acc_ref[...], gf_ref[...])


def _ffn(x1, g2, wu_p, cw_p, cb_p, wd_p, gf, seq_len):
    n = x1.shape[0]
    tm = min(FFN_ROW_TILE, seq_len)
    tiles_per_seq = seq_len // tm
    hb = tm // HALO
    nhb = n // HALO
    kern = functools.partial(_ffn_kernel, tm=tm, tiles_per_seq=tiles_per_seq)
    return pl.pallas_call(
        kern,
        grid=(n // tm,),
        in_specs=[
            pl.BlockSpec((HALO, D_MODEL), lambda i: (jnp.maximum(i * hb - 1, 0), 0)),
            pl.BlockSpec((tm, D_MODEL), lambda i: (i, 0)),
            pl.BlockSpec((HALO, D_MODEL), lambda i: (jnp.minimum((i + 1) * hb, nhb - 1), 0)),
            pl.BlockSpec((1, D_MODEL), lambda i: (0, 0)),
            pl.BlockSpec((FF_NCHUNK, D_MODEL, 2 * FF_CHUNK), lambda i: (0, 0, 0)),
            pl.BlockSpec((FF_NCHUNK, 3, 2 * FF_CHUNK), lambda i: (0, 0, 0)),
            pl.BlockSpec((FF_NCHUNK, 1, 2 * FF_CHUNK), lambda i: (0, 0, 0)),
            pl.BlockSpec((FF_NCHUNK, FF_CHUNK, D_MODEL), lambda i: (0, 0, 0)),
            pl.BlockSpec((1, D_MODEL), lambda i: (0, 0)),
        ],
        out_specs=pl.BlockSpec((tm, D_MODEL), lambda i: (i, 0)),
        out_shape=jax.ShapeDtypeStruct((n, D_MODEL), F32),
        scratch_shapes=[
            pltpu.VMEM((tm + 2 * HALO, D_MODEL), BF16),
            pltpu.VMEM((2, tm + 2 * HALO, 2 * FF_CHUNK), F32),
            pltpu.VMEM((tm, D_MODEL), F32),
        ],
        compiler_params=pltpu.CompilerParams(
            dimension_semantics=("arbitrary",),
            vmem_limit_bytes=VMEM_LIMIT),
        name="ffn",
    )(x1, x1, x1, g2, wu_p, cw_p, cb_p, wd_p, gf)


def _pack_w_in(w):
    gla_w = 2 * GLA_QK_W + 2 * GLA_V_W + 2 * GLA_GATE_RANK
    r0 = gla_w
    g0 = r0 + 3 * RWKV_W + RWKV_DECAY_LORA + RWKV_AAA_LORA + RWKV_GATE_LORA
    pad = jnp.zeros((w.shape[0], 256 - 2 * GLA_GATE_RANK), w.dtype)
    return jnp.concatenate([
        w[:, g0:g0 + 2 * D_MODEL],
        w[:, 0:2 * GLA_QK_W + 2 * GLA_V_W],
        w[:, r0:r0 + 3 * RWKV_W + 256],
        w[:, 2 * GLA_QK_W + 2 * GLA_V_W:gla_w],
        pad], axis=1)


def _pack_mu(mu):
    z = jnp.zeros((COL_R,), mu.dtype)
    z2 = jnp.zeros((P_COLS - COL_GS,), mu.dtype)
    return jnp.concatenate([z, mu, z2])[None, :]


def _pack_ff_cols(t):
    pad = [(0, 0)] * (t.ndim - 1) + [(0, FF_PAD - D_FF)]
    g = jnp.pad(t[..., :D_FF], pad)
    v = jnp.pad(t[..., D_FF:], pad)
    lead = t.shape[:-1]
    g = g.reshape(lead + (FF_NCHUNK, FF_CHUNK))
    v = v.reshape(lead + (FF_NCHUNK, FF_CHUNK))
    gv = jnp.concatenate([g, v], axis=-1)
    return jnp.moveaxis(gv, -2, 0)


def kernel(x, norm1_g, w_in, gla_wa2_f, gla_ba_f, gla_wa2_b, gla_ba_b, gla_norm_g, gla_proj, rwkv_mu_prev, rwkv_mu_next, rwkv_w0_f, rwkv_w2_f, rwkv_w0_b, rwkv_w2_b, rwkv_a0, rwkv_a2, rwkv_g2, rwkv_k_k, rwkv_k_a, rwkv_r_k, rwkv_ln_w, rwkv_ln_b, rwkv_proj, w_out, norm2_g, ffn_up, ffn_conv_w, ffn_conv_b, ffn_down, norm_f_g):
    batch, seq_len, d = x.shape
    assert w_in.shape[0] == 1 and d == D_MODEL and seq_len % CHUNK == 0
    x2 = x.reshape(batch * seq_len, d)
    row = lambda t: t.reshape(1, -1)
    for l in range(1):
        w_packed = _pack_w_in(w_in[l]).astype(BF16)
        wa2_f = jnp.pad(gla_wa2_f[l], ((0, 256 - GLA_GATE_RANK), (0, 0)))
        wa2_b = jnp.pad(gla_wa2_b[l], ((GLA_GATE_RANK, 256 - 2 * GLA_GATE_RANK), (0, 0)))
        zl = jnp.zeros((RWKV_AAA_LORA, RWKV_W), F32)
        w2_f = jnp.concatenate([rwkv_w2_f[l], zl], axis=0)
        w2_b = jnp.concatenate([rwkv_w2_b[l], zl], axis=0)
        a2p = jnp.concatenate([jnp.zeros((RWKV_DECAY_LORA, RWKV_W), F32), rwkv_a2[l]], axis=0).astype(BF16)

        p = _inproj(x2, row(norm1_g[l]), w_packed, _pack_mu(rwkv_mu_prev[l]),
                    _pack_mu(rwkv_mu_next[l]), seq_len)

        o_f = _gla(p, wa2_f, row(gla_ba_f[l]), batch, seq_len, reverse=False)
        oa = _gla(p, wa2_b, row(gla_ba_b[l]), batch, seq_len, reverse=True,
                  o_fwd=o_f, norm_g=row(gla_norm_g[l]))

        rw_common = (row(rwkv_a0[l]), a2p, row(rwkv_k_k[l]), row(rwkv_k_a[l]))
        y_f = _rwkv(p, row(rwkv_w0_f[l]), w2_f, *rw_common, batch, seq_len, reverse=False)
        ob = _rwkv(p, row(rwkv_w0_b[l]), w2_b, *rw_common, batch, seq_len, reverse=True,
                   y_fwd=y_f, g2=rwkv_g2[l].astype(BF16), r_k=row(rwkv_r_k[l]),
                   ln_w=row(rwkv_ln_w[l]), ln_b=row(rwkv_ln_b[l]))

        x1 = _merge(x2, oa, ob, p, gla_proj[l].astype(BF16), rwkv_proj[l].astype(BF16),
                    w_out[l].astype(BF16), seq_len)

        wu_p = _pack_ff_cols(ffn_up[l]).astype(BF16)
        cw_p = _pack_ff_cols(ffn_conv_w[l])
        cb_p = _pack_ff_cols(ffn_conv_b[l][None, :])
        wd_p = jnp.pad(ffn_down[l], ((0, FF_PAD - D_FF), (0, 0))).reshape(
            FF_NCHUNK, FF_CHUNK, D_MODEL).astype(BF16)
        x2 = _ffn(x1, row(norm2_g[l]), wu_p, cw_p, cb_p, wd_p, row(norm_f_g), seq_len)
    return x2.reshape(batch, seq_len, d)
```

```python
import functools

import jax
import jax.numpy as jnp
from jax import lax
from jax.experimental import pallas as pl
from jax.experimental.pallas import tpu as pltpu

F32 = jnp.float32
BF16 = jnp.bfloat16

D_MODEL = 1024
GLA_HEADS = 4
GLA_DK = 64
GLA_DV = 128
GLA_QK_W = GLA_HEADS * GLA_DK
GLA_V_W = GLA_HEADS * GLA_DV
GLA_GATE_RANK = 16
GLA_LOGIT_NORM = 16.0
CHUNK = 64
RWKV_HEAD = 64
RWKV_W = 512
RWKV_HEADS = RWKV_W // RWKV_HEAD
RWKV_DECAY_LORA = 64
RWKV_AAA_LORA = 64
RWKV_GATE_LORA = 128
RWKV_GN_EPS = RWKV_HEAD * 1e-5
D_FF = 2752
NORM_EPS = 1e-6
HEAD_NORM_EPS = 1e-5

COL_GA, COL_GB = 0, 1024
COL_QK, COL_V, COL_OG = 2048, 2560, 3072
COL_R, COL_RK, COL_RV = 3584, 4096, 4608
COL_RS, COL_GS = 5120, 5376
P_COLS = 5632
HALO = 16

FF_CHUNK = 256
FF_PAD = 2816
FF_NCHUNK = FF_PAD // FF_CHUNK

VMEM_LIMIT = 56 * 1024 * 1024
ROW_TILE = 1024
FFN_ROW_TILE = 512
GLA_SEQ_TILE = 1024
RWKV_SEQ_TILE = 256
RWKV_GROUP = 4


def _sigmoid(x):
    return 1.0 / (1.0 + jnp.exp(-x))


def _softplus(z):
    return jnp.maximum(z, 0.0) + jnp.log(1.0 + jnp.exp(-jnp.abs(z)))


def _split3(x):
    hi = x.astype(BF16)
    r1 = x - hi.astype(F32)
    mid = r1.astype(BF16)
    lo = (r1 - mid.astype(F32)).astype(BF16)
    return hi, mid, lo


def _dot(a, b):
    return jnp.dot(a, b, preferred_element_type=F32)


def _dot_nt(a, b):
    return lax.dot_general(a, b, (((1,), (1,)), ((), ())), preferred_element_type=F32)


def _dot_tn(a, b):
    return lax.dot_general(a, b, (((0,), (0,)), ((), ())), preferred_element_type=F32)


def _dot_exact_lhs(sel, x):
    hi = x.astype(BF16)
    lo = (x - hi.astype(F32)).astype(BF16)
    s = sel.astype(BF16)
    return _dot(s, hi) + _dot(s, lo)


def _dot_exact_rhs(x, sel):
    hi = x.astype(BF16)
    lo = (x - hi.astype(F32)).astype(BF16)
    s = sel.astype(BF16)
    return _dot(hi, s) + _dot(lo, s)


def _dot_x3(a, b):
    ah = a.astype(BF16)
    al = (a - ah.astype(F32)).astype(BF16)
    bh = b.astype(BF16)
    bl = (b - bh.astype(F32)).astype(BF16)
    return _dot(ah, bh) + _dot(al, bh) + _dot(ah, bl)


def _rmsnorm(x, g):
    ms = jnp.mean(x * x, axis=-1, keepdims=True)
    return x * lax.rsqrt(ms + NORM_EPS) * g


def _iota2(shape, dim):
    return lax.broadcasted_iota(jnp.int32, shape, dim)


def _div(x, size):
    return lax.shift_right_logical(x, size.bit_length() - 1)


def _mod(x, size):
    return lax.bitwise_and(x, size - 1)


def _chunk_cumsum(x, reverse):
    rows = x.shape[0]
    blk = min(rows, 256)
    row = _iota2((blk, blk), 0)
    col = _iota2((blk, blk), 1)
    same = _div(row, CHUNK) == _div(col, CHUNK)
    tri = jnp.where(same & ((col >= row) if reverse else (col <= row)), 1.0, 0.0)
    parts = [_dot_exact_lhs(tri, x[i:i + blk]) for i in range(0, rows, blk)]
    return parts[0] if len(parts) == 1 else jnp.concatenate(parts, axis=0)


def _blockdiag(z, head_masks):
    return jnp.concatenate([jnp.where(m, z, 0.0) for m in head_masks], axis=0).astype(BF16)


def _inproj_kernel(xp_ref, x_ref, xn_ref, g_ref, w_ref, mup_ref, mun_ref, o_ref,
                   h_ref, p_ref, *, tm, tiles_per_seq, first_shift_tile, n_sub):
    i = pl.program_id(0)
    j = pl.program_id(1)

    @pl.when(j == 0)
    def _():
        pos = lax.rem(i, tiles_per_seq)
        keep_prev = jnp.where(pos == 0, 0.0, 1.0)
        keep_next = jnp.where(pos == tiles_per_seq - 1, 0.0, 1.0)
        g = g_ref[...]
        h_ref[0:HALO, :] = _rmsnorm(xp_ref[...] * keep_prev, g).astype(BF16)
        h_ref[HALO:HALO + tm, :] = _rmsnorm(x_ref[...], g).astype(BF16)
        h_ref[HALO + tm:HALO + tm + HALO, :] = _rmsnorm(xn_ref[...] * keep_next, g).astype(BF16)

    @pl.when(j < first_shift_tile)
    def _():
        o_ref[...] = _dot(h_ref[HALO:HALO + tm, :], w_ref[...])

    @pl.when(j >= first_shift_tile)
    def _():
        sub = tm // n_sub
        mm = lambda s: _dot(h_ref[s * sub:(s + 1) * sub + 2 * HALO, :], w_ref[...])
        p_next = mm(0)
        for s in range(n_sub):
            slot = s % 2
            p_ref[slot] = p_next
            if s + 1 < n_sub:
                p_next = mm(s + 1)
            p = p_ref[slot, HALO:HALO + sub, :]
            pp = p_ref[slot, HALO - 1:HALO - 1 + sub, :]
            pn = p_ref[slot, HALO + 1:HALO + 1 + sub, :]
            o_ref[s * sub:(s + 1) * sub, :] = (
                p + mup_ref[...] * (pp - p) + mun_ref[...] * (pn - p))


def _inproj(x2, g, w_packed, mu_prev, mu_next, seq_len):
    n = x2.shape[0]
    tm = min(ROW_TILE, seq_len)
    tn = 512
    tiles_per_seq = seq_len // tm
    hb = tm // HALO
    nhb = n // HALO
    n_sub = max(1, tm // 256)
    sub = tm // n_sub
    kern = functools.partial(_inproj_kernel, tm=tm, tiles_per_seq=tiles_per_seq,
                             first_shift_tile=COL_R // tn, n_sub=n_sub)
    return pl.pallas_call(
        kern,
        grid=(n // tm, P_COLS // tn),
        in_specs=[
            pl.BlockSpec((HALO, D_MODEL), lambda i, j: (jnp.maximum(i * hb - 1, 0), 0)),
            pl.BlockSpec((tm, D_MODEL), lambda i, j: (i, 0)),
            pl.BlockSpec((HALO, D_MODEL), lambda i, j: (jnp.minimum((i + 1) * hb, nhb - 1), 0)),
            pl.BlockSpec((1, D_MODEL), lambda i, j: (0, 0)),
            pl.BlockSpec((D_MODEL, tn), lambda i, j: (0, j)),
            pl.BlockSpec((1, tn), lambda i, j: (0, j)),
            pl.BlockSpec((1, tn), lambda i, j: (0, j)),
        ],
        out_specs=pl.BlockSpec((tm, tn), lambda i, j: (i, j)),
        out_shape=jax.ShapeDtypeStruct((n, P_COLS), F32),
        scratch_shapes=[
            pltpu.VMEM((tm + 2 * HALO, D_MODEL), BF16),
            pltpu.VMEM((2, sub + 2 * HALO, tn), F32),
        ],
        compiler_params=pltpu.CompilerParams(
            dimension_semantics=("arbitrary", "arbitrary"),
            vmem_limit_bytes=VMEM_LIMIT),
        name="inproj",
    )(x2, x2, x2, g, w_packed, mu_prev, mu_next)


def _gla_kernel(*refs, reverse, tb, final):
    if final:
        (qk_ref, v_ref, gs_ref, wa2_ref, ba_ref, of_ref, og_ref, ng_ref,
         o_ref, st_ref, b_ref, acc_ref) = refs
    else:
        qk_ref, v_ref, gs_ref, wa2_ref, ba_ref, o_ref, st_ref, b_ref = refs
    j = pl.program_id(1)
    nc = tb // CHUNK

    @pl.when(j == 0)
    def _():
        st_ref[...] = jnp.zeros_like(st_ref)

    logit = _dot_x3(gs_ref[...], wa2_ref[...]) + ba_ref[...]
    la = (jnp.minimum(logit, 0.0) - jnp.log(1.0 + jnp.exp(-jnp.abs(logit)))) * (1.0 / GLA_LOGIT_NORM)
    b_ref[...] = _chunk_cumsum(la, reverse)

    lane_k = _div(_iota2((CHUNK, GLA_QK_W), 1), GLA_DK)
    lane_v = _div(_iota2((CHUNK, GLA_V_W), 1), GLA_DV)
    k_masks = [lane_k == h for h in range(GLA_HEADS)]
    v_masks = [lane_v == h for h in range(GLA_HEADS)]
    t_idx = _iota2((CHUNK, GLA_QK_W), 0)
    s_idx = _mod(_iota2((CHUNK, GLA_QK_W), 1), CHUNK)
    causal = (s_idx >= t_idx) if reverse else (s_idx <= t_idx)
    st_mask = (_div(_iota2((GLA_V_W, GLA_QK_W), 0), GLA_DV)
               == _div(_iota2((GLA_V_W, GLA_QK_W), 1), GLA_DK))
    i_ref = (CHUNK - 1 - CHUNK // 2) if reverse else CHUNK // 2
    i_last = 0 if reverse else CHUNK - 1

    QI, KD, QE, VB, KI_BD, V_BD, DEC = ([] for _ in range(7))
    for c in range(nc):
        rows = slice(c * CHUNK, (c + 1) * CHUNK)
        b = b_ref[rows, :]
        b_mid = b[i_ref:i_ref + 1, :]
        b_last = b[i_last:i_last + 1, :]
        q = qk_ref[rows, 0:GLA_QK_W] * (GLA_DK ** -0.5)
        k = qk_ref[rows, GLA_QK_W:2 * GLA_QK_W]
        v = v_ref[rows, :]
        QI.append((q * jnp.exp(b - b_mid)).astype(BF16))
        KD.append((k * jnp.exp(b_last - b)).astype(BF16))
        QE.append((q * jnp.exp(b)).astype(BF16))
        VB.append(v.astype(BF16))
        KI_BD.append(_blockdiag(k * jnp.exp(b_mid - b), k_masks))
        V_BD.append(_blockdiag(v, v_masks))
        DEC.append(jnp.exp(b_last))
    A = [jnp.where(causal, _dot_nt(qi, kb), 0.0) for qi, kb in zip(QI, KI_BD)]
    O_IN = [_dot(a.astype(BF16), vb) for a, vb in zip(A, V_BD)]
    KV = [jnp.where(st_mask, _dot_tn(vb, kd), 0.0) for vb, kd in zip(VB, KD)]
    for ci in range(nc):
        c = (nc - 1 - ci) if reverse else ci
        rows = slice(c * CHUNK, (c + 1) * CHUNK)
        st = st_ref[...]
        o = O_IN[c] + _dot_nt(QE[c], st.astype(BF16))
        st_ref[...] = st * DEC[c] + KV[c]
        if final:
            acc_ref[rows, :] = of_ref[rows, :] + o
        else:
            o_ref[rows, :] = o

    if final:
        o = acc_ref[...]
        parts = []
        for h in range(GLA_HEADS):
            oh = o[:, h * GLA_DV:(h + 1) * GLA_DV]
            ms = jnp.mean(oh * oh, axis=-1, keepdims=True)
            parts.append(oh * lax.rsqrt(ms + HEAD_NORM_EPS))
        on = jnp.concatenate(parts, axis=1) * ng_ref[...]
        og = og_ref[...]
        o_ref[...] = on * (og * _sigmoid(og))


def _gla(p, wa2p, ba, batch, seq_len, reverse, o_fwd=None, norm_g=None):
    final = o_fwd is not None
    tb = min(GLA_SEQ_TILE, seq_len)
    nt = seq_len // tb
    n = batch * seq_len

    def rowblk(b, j):
        return b * nt + ((nt - 1 - j) if reverse else j)

    in_specs = [
        pl.BlockSpec((tb, 512), lambda b, j: (rowblk(b, j), COL_QK // 512)),
        pl.BlockSpec((tb, 512), lambda b, j: (rowblk(b, j), COL_V // 512)),
        pl.BlockSpec((tb, 256), lambda b, j: (rowblk(b, j), COL_GS // 256)),
        pl.BlockSpec((256, GLA_QK_W), lambda b, j: (0, 0)),
        pl.BlockSpec((1, GLA_QK_W), lambda b, j: (0, 0)),
    ]
    args = [p, p, p, wa2p, ba]
    scratch = [pltpu.VMEM((GLA_V_W, GLA_QK_W), F32), pltpu.VMEM((tb, GLA_QK_W), F32)]
    if final:
        in_specs += [
            pl.BlockSpec((tb, GLA_V_W), lambda b, j: (rowblk(b, j), 0)),
            pl.BlockSpec((tb, 512), lambda b, j: (rowblk(b, j), COL_OG // 512)),
            pl.BlockSpec((1, GLA_V_W), lambda b, j: (0, 0)),
        ]
        args += [o_fwd, p, norm_g]
        scratch.append(pltpu.VMEM((tb, GLA_V_W), F32))
    kern = functools.partial(_gla_kernel, reverse=reverse, tb=tb, final=final)
    return pl.pallas_call(
        kern,
        grid=(batch, nt),
        in_specs=in_specs,
        out_specs=pl.BlockSpec((tb, GLA_V_W), lambda b, j: (rowblk(b, j), 0)),
        out_shape=jax.ShapeDtypeStruct((n, GLA_V_W), F32),
        scratch_shapes=scratch,
        compiler_params=pltpu.CompilerParams(
            dimension_semantics=("arbitrary", "arbitrary"),
            vmem_limit_bytes=VMEM_LIMIT),
        name="gla_bwd" if reverse else "gla_fwd",
    )(*args)


def _rwkv_kernel(*refs, reverse, tb, final):
    if final:
        (r_ref, k_ref, v_ref, rs_ref, w0_ref, w2_ref, a0_ref, a2_ref, kk_ref, ka_ref,
         yf_ref, g2_ref, rk_ref, lnw_ref, lnb_ref,
         o_ref, p_ref, cs_s, lw_s, kk_s, ba_s, k2_s, reff_s, oloc_s, gm_s, hm_s, y_s) = refs
    else:
        (r_ref, k_ref, v_ref, rs_ref, w0_ref, w2_ref, a0_ref, a2_ref, kk_ref, ka_ref,
         o_ref, p_ref, cs_s, lw_s, kk_s, ba_s, k2_s, reff_s, oloc_s, gm_s, hm_s) = refs
        y_s = o_ref
    j = pl.program_id(1)
    nc = tb // CHUNK
    G = RWKV_GROUP
    L = G * RWKV_HEAD
    ngrp = RWKV_W // L

    @pl.when(j == 0)
    def _():
        p_ref[...] = jnp.zeros_like(p_ref)

    r = r_ref[...]
    k = k_ref[...]
    lora_in = rs_ref[:, 0:RWKV_DECAY_LORA + RWKV_AAA_LORA]
    w_pre = w0_ref[...] + _dot_x3(jnp.tanh(lora_in), w2_ref[...])
    logw = -jnp.exp(-_softplus(-w_pre) - 0.5)
    a_lr = _sigmoid(a0_ref[...] + _dot(lora_in.astype(BF16), a2_ref[...]))
    hr = _div(_iota2((RWKV_W, RWKV_W), 0), RWKV_HEAD)
    hc = _div(_iota2((RWKV_W, RWKV_W), 1), RWKV_HEAD)
    head_ones = jnp.where(hr == hc, 1.0, 0.0).astype(BF16)
    head_sum = lambda t: _dot(t.astype(BF16), head_ones)
    kk = k * kk_ref[...]
    ssq = head_sum(kk * kk)
    kk = kk / jnp.maximum(jnp.sqrt(ssq), 1e-12)
    k2 = k * (1.0 + (a_lr - 1.0) * ka_ref[...])

    cs_s[...] = _chunk_cumsum(logw, reverse)
    lw_s[...] = logw
    kk_s[...] = kk
    ba_s[...] = kk * a_lr
    k2_s[...] = k2
    i_last = 0 if reverse else CHUNK - 1

    lane_h = _div(_iota2((CHUNK, L), 1), RWKV_HEAD)
    h_masks = [lane_h == h for h in range(G)]
    t_idx = _iota2((CHUNK, L), 0)
    s_idx = _mod(_iota2((CHUNK, L), 1), CHUNK)
    strict = (s_idx > t_idx) if reverse else (s_idx < t_idx)
    incl = (s_idx >= t_idx) if reverse else (s_idx <= t_idx)
    eye_p = jnp.where(s_idx == t_idx, 1.0, 0.0)
    bd_mask = _div(_iota2((L, L), 0), RWKV_HEAD) == _div(_iota2((L, L), 1), RWKV_HEAD)
    eye_l = jnp.where(_iota2((L, L), 0) == _iota2((L, L), 1), 1.0, 0.0)

    def blockdiag(z):
        return _blockdiag(z, h_masks)

    chains = [(c, gi) for c in range(nc) for gi in range(ngrp)]
    rows_of = lambda c: slice(c * CHUNK, (c + 1) * CHUNK)
    lanes_of = lambda gi: slice(gi * L, (gi + 1) * L)
    AT, BT, KT, RT, BD, KD, GL, V = ([] for _ in range(8))
    for c in range(nc):
        rows = rows_of(c)
        cs = cs_s[rows, :]
        cs_tot = cs[i_last:i_last + 1, :]
        g_inv = jnp.exp(-cs)
        g_rem = jnp.exp(cs_tot - cs)
        at_c = (-kk_s[rows, :]) * jnp.exp(cs - lw_s[rows, :])
        bt_c = ba_s[rows, :] * g_inv
        kt_c = k2_s[rows, :] * g_inv
        rt_c = r_ref[rows, :] * jnp.exp(cs)
        bd_c = ba_s[rows, :] * g_rem
        kd_c = k2_s[rows, :] * g_rem
        gl_c = jnp.exp(cs_tot)
        for gi in range(ngrp):
            lanes = lanes_of(gi)
            AT.append(at_c[:, lanes])
            BT.append(bt_c[:, lanes])
            KT.append(kt_c[:, lanes])
            RT.append(rt_c[:, lanes])
            BD.append(bd_c[:, lanes])
            KD.append(kd_c[:, lanes])
            GL.append(gl_c[:, lanes])
            V.append(v_ref[rows, lanes])
    bf = lambda t: t.astype(BF16)
    cat0 = lambda a, b: jnp.concatenate([a, b], axis=0)
    cat1 = lambda a, b: jnp.concatenate([a, b], axis=1)
    X = [bf(cat0(a, r_)) for a, r_ in zip(AT, RT)]
    G1 = [_dot_nt(x, blockdiag(b)) for x, b in zip(X, BT)]
    G2 = [_dot_nt(x, blockdiag(k_)) for x, k_ in zip(X, KT)]
    LAB = [jnp.where(strict, g[0:CHUNK], 0.0) for g in G1]
    MRB = [jnp.where(incl, g[CHUNK:2 * CHUNK], 0.0) for g in G1]
    LAK = [jnp.where(strict, g[0:CHUNK], 0.0) for g in G2]
    MRK = [jnp.where(incl, g[CHUNK:2 * CHUNK], 0.0) for g in G2]
    TM = [eye_p + l for l in LAB]
    XP = [_dot(bf(l), blockdiag(l)) for l in LAB]
    for _ in range(4):
        Y = [_dot(bf(cat0(t, x)), blockdiag(x)) for t, x in zip(TM, XP)]
        TM = [t + y[0:CHUNK] for t, y in zip(TM, Y)]
        XP = [y[CHUNK:2 * CHUNK] for y in Y]
    TM = [t + _dot(bf(t), blockdiag(x)) for t, x in zip(TM, XP)]
    VBD = [blockdiag(v) for v in V]
    LMV = [_dot(bf(cat0(l, m)), vb) for l, m, vb in zip(LAK, MRK, VBD)]
    LAKV = [t[0:CHUNK] for t in LMV]
    MV = [t[CHUNK:2 * CHUNK] for t in LMV]
    WU =[_dot(bf(t), cat1(blockdiag(a), blockdiag(lv))) for t, a, lv in zip(TM, AT, LAKV)]
    WA = [wu[:, 0:L] for wu in WU]
    UV = [wu[:, L:2 * L] for wu in WU]
    RO = [_dot(bf(m), cat1(blockdiag(wa), blockdiag(uv))) for m, wa, uv in zip(MRB, WA, UV)]
    GH = [_dot_tn(bf(bd), bf(cat1(wa, uv))) for bd, wa, uv in zip(BD, WA, UV)]
    KV = [_dot_tn(bf(kd), bf(v)) for kd, v in zip(KD, V)]
    for n_, (c, gi) in enumerate(chains):
        rows, lanes = rows_of(c), lanes_of(gi)
        reff_s[rows, lanes] = RT[n_] + RO[n_][:, 0:L]
        oloc_s[rows, lanes] = RO[n_][:, L:2 * L] + MV[n_]
        gm_s[c, gi] = bf(eye_l * GL[n_] + jnp.where(bd_mask, GH[n_][:, 0:L], 0.0))
        hm_s[c, gi] = jnp.where(bd_mask, GH[n_][:, L:2 * L] + KV[n_], 0.0)

    for ci in range(nc):
        c = (nc - 1 - ci) if reverse else ci
        rows = slice(c * CHUNK, (c + 1) * CHUNK)
        for gi in range(ngrp):
            lanes = slice(gi * L, (gi + 1) * L)
            pst = p_ref[gi].astype(BF16)
            lhs = jnp.concatenate([reff_s[rows, lanes].astype(BF16), gm_s[c, gi]], axis=0)
            yp = _dot(lhs, pst)
            y_s[rows, lanes] = yp[0:CHUNK] + oloc_s[rows, lanes]
            p_ref[gi] = yp[CHUNK:CHUNK + L] + hm_s[c, gi]

    if final:
        y = y_s[...] + yf_ref[...]
        inv_n = 1.0 / RWKV_HEAD
        mu = head_sum(y) * inv_n
        yc = y - mu
        var = head_sum(yc * yc) * inv_n
        yn = yc * lax.rsqrt(var + RWKV_GN_EPS) * lnw_ref[...] + lnb_ref[...]
        bonus = head_sum(r * k2 * rk_ref[...]) * v_ref[...]
        gl = rs_ref[:, RWKV_DECAY_LORA + RWKV_AAA_LORA:]
        g = _dot(_sigmoid(gl).astype(BF16), g2_ref[...])
        o_ref[...] = (yn + bonus) * g


def _rwkv(p, w0, w2p, a0, a2p, k_k, k_a, batch, seq_len, reverse,
          y_fwd=None, g2=None, r_k=None, ln_w=None, ln_b=None):
    final = y_fwd is not None
    tb = min(RWKV_SEQ_TILE, seq_len)
    nt = seq_len // tb
    n = batch * seq_len
    L = RWKV_GROUP * RWKV_HEAD
    ngrp = RWKV_W // L
    lora = RWKV_DECAY_LORA + RWKV_AAA_LORA

    def rowblk(b, j):
        return b * nt + ((nt - 1 - j) if reverse else j)

    vec = pl.BlockSpec((1, RWKV_W), lambda b, j: (0, 0))
    in_specs = [
        pl.BlockSpec((tb, 512), lambda b, j: (rowblk(b, j), COL_R // 512)),
        pl.BlockSpec((tb, 512), lambda b, j: (rowblk(b, j), COL_RK // 512)),
        pl.BlockSpec((tb, 512), lambda b, j: (rowblk(b, j), COL_RV // 512)),
        pl.BlockSpec((tb, 256), lambda b, j: (rowblk(b, j), COL_RS // 256)),
        vec,
        pl.BlockSpec((lora, RWKV_W), lambda b, j: (0, 0)),
        vec,
        pl.BlockSpec((lora, RWKV_W), lambda b, j: (0, 0)),
        vec, vec,
    ]
    args = [p, p, p, p, w0, w2p, a0, a2p, k_k, k_a]
    nc = tb // CHUNK
    scratch = ([pltpu.VMEM((ngrp, L, L), F32)] + [pltpu.VMEM((tb, RWKV_W), F32)] * 7
               + [pltpu.VMEM((nc, ngrp, L, L), BF16), pltpu.VMEM((nc, ngrp, L, L), F32)])
    if final:
        in_specs += [
            pl.BlockSpec((tb, RWKV_W), lambda b, j: (rowblk(b, j), 0)),
            pl.BlockSpec((RWKV_GATE_LORA, RWKV_W), lambda b, j: (0, 0)),
            vec, vec, vec,
        ]
        args += [y_fwd, g2, r_k, ln_w, ln_b]
        scratch.append(pltpu.VMEM((tb, RWKV_W), F32))
    kern = functools.partial(_rwkv_kernel, reverse=reverse, tb=tb, final=final)
    return pl.pallas_call(
        kern,
        grid=(batch, nt),
        in_specs=in_specs,
        out_specs=pl.BlockSpec((tb, RWKV_W), lambda b, j: (rowblk(b, j), 0)),
        out_shape=jax.ShapeDtypeStruct((n, RWKV_W), F32),
        scratch_shapes=scratch,
        compiler_params=pltpu.CompilerParams(
            dimension_semantics=("arbitrary", "arbitrary"),
            vmem_limit_bytes=VMEM_LIMIT),
        name="rwkv_bwd" if reverse else "rwkv_fwd",
    )(*args)


def _merge_kernel(x_ref, oa_ref, ob_ref, ga_ref, gb_ref, wa_ref, wb_ref, wo_ref, o_ref):
    ya = _dot(oa_ref[...].astype(BF16), wa_ref[...])
    yb = _dot(ob_ref[...].astype(BF16), wb_ref[...])
    merged = _sigmoid(ga_ref[...]) * ya + _sigmoid(gb_ref[...]) * yb
    o_ref[...] = x_ref[...] + _dot(merged.astype(BF16), wo_ref[...])


def _merge(x2, oa, ob, p, gla_proj, rwkv_proj, w_out, seq_len):
    n = x2.shape[0]
    tm = min(ROW_TILE, seq_len)
    return pl.pallas_call(
        _merge_kernel,
        grid=(n // tm,),
        in_specs=[
            pl.BlockSpec((tm, D_MODEL), lambda i: (i, 0)),
            pl.BlockSpec((tm, GLA_V_W), lambda i: (i, 0)),
            pl.BlockSpec((tm, RWKV_W), lambda i: (i, 0)),
            pl.BlockSpec((tm, D_MODEL), lambda i: (i, COL_GA // D_MODEL)),
            pl.BlockSpec((tm, D_MODEL), lambda i: (i, COL_GB // D_MODEL)),
            pl.BlockSpec((GLA_V_W, D_MODEL), lambda i: (0, 0)),
            pl.BlockSpec((RWKV_W, D_MODEL), lambda i: (0, 0)),
            pl.BlockSpec((D_MODEL, D_MODEL), lambda i: (0, 0)),
        ],
        out_specs=pl.BlockSpec((tm, D_MODEL), lambda i: (i, 0)),
        out_shape=jax.ShapeDtypeStruct((n, D_MODEL), F32),
        compiler_params=pltpu.CompilerParams(
            dimension_semantics=("arbitrary",),
            vmem_limit_bytes=VMEM_LIMIT),
        name="merge",
    )(x2, oa, ob, p, p, gla_proj, rwkv_proj, w_out)


def _ffn_kernel(xp_ref, x_ref, xn_ref, g2_ref, wu_ref, cw_ref, cb_ref, wd_ref, gf_ref, o_ref,
                h_ref, u_ref, acc_ref, *, tm, tiles_per_seq):
    i = pl.program_id(0)
    pos = lax.rem(i, tiles_per_seq)
    keep_prev = jnp.where(pos == 0, 0.0, 1.0)
    keep_next = jnp.where(pos == tiles_per_seq - 1, 0.0, 1.0)
    g2 = g2_ref[...]
    h_ref[0:HALO, :] = _rmsnorm(xp_ref[...] * keep_prev, g2).astype(BF16)
    h_ref[HALO:HALO + tm, :] = _rmsnorm(x_ref[...], g2).astype(BF16)
    h_ref[HALO + tm:HALO + tm + HALO, :] = _rmsnorm(xn_ref[...] * keep_next, g2).astype(BF16)

    up = lambda c: _dot(h_ref[...], wu_ref[c])
    u_next = up(0)
    for c in range(FF_NCHUNK):
        slot = c % 2
        u_ref[slot] = u_next
        if c + 1 < FF_NCHUNK:
            u_next = up(c + 1)
        cw = cw_ref[c]
        uc = (cw[0:1, :] * u_ref[slot, HALO - 1:HALO - 1 + tm, :]
              + cw[1:2, :] * u_ref[slot, HALO:HALO + tm, :]
              + cw[2:3, :] * u_ref[slot, HALO + 1:HALO + 1 + tm, :]
              + cb_ref[c])
        ug = uc[:, 0:FF_CHUNK]
        act = (ug * _sigmoid(ug)) * uc[:, FF_CHUNK:2 * FF_CHUNK]
        down = _dot(act.astype(BF16), wd_ref[c])
        if c == 0:
            acc_ref[...] = down
        else:
            acc_ref[...] += down
    o_ref[...] = _rmsnorm(x_ref[...] + acc_ref[...], gf_ref[...])


def _ffn(x1, g2, wu_p, cw_p, cb_p, wd_p, gf, seq_len):
    n = x1.shape[0]
    tm = min(FFN_ROW_TILE, seq_len)
    tiles_per_seq = seq_len // tm
    hb = tm // HALO
    nhb = n // HALO
    kern = functools.partial(_ffn_kernel, tm=tm, tiles_per_seq=tiles_per_seq)
    return pl.pallas_call(
        kern,
        grid=(n // tm,),
        in_specs=[
            pl.BlockSpec((HALO, D_MODEL), lambda i: (jnp.maximum(i * hb - 1, 0), 0)),
            pl.BlockSpec((tm, D_MODEL), lambda i: (i, 0)),
            pl.BlockSpec((HALO, D_MODEL), lambda i: (jnp.minimum((i + 1) * hb, nhb - 1), 0)),
            pl.BlockSpec((1, D_MODEL), lambda i: (0, 0)),
            pl.BlockSpec((FF_NCHUNK, D_MODEL, 2 * FF_CHUNK), lambda i: (0, 0, 0)),
            pl.BlockSpec((FF_NCHUNK, 3, 2 * FF_CHUNK), lambda i: (0, 0, 0)),
            pl.BlockSpec((FF_NCHUNK, 1, 2 * FF_CHUNK), lambda i: (0, 0, 0)),
            pl.BlockSpec((FF_NCHUNK, FF_CHUNK, D_MODEL), lambda i: (0, 0, 0)),
            pl.BlockSpec((1, D_MODEL), lambda i: (0, 0)),
        ],
        out_specs=pl.BlockSpec((tm, D_MODEL), lambda i: (i, 0)),
        out_shape=jax.ShapeDtypeStruct((n, D_MODEL), F32),
        scratch_shapes=[
            pltpu.VMEM((tm + 2 * HALO, D_MODEL), BF16),
            pltpu.VMEM((2, tm + 2 * HALO, 2 * FF_CHUNK), F32),
            pltpu.VMEM((tm, D_MODEL), F32),
        ],
        compiler_params=pltpu.CompilerParams(
            dimension_semantics=("arbitrary",),
            vmem_limit_bytes=VMEM_LIMIT),
        name="ffn",
    )(x1, x1, x1, g2, wu_p, cw_p, cb_p, wd_p, gf)


def _pack_w_in(w):
    gla_w = 2 * GLA_QK_W + 2 * GLA_V_W + 2 * GLA_GATE_RANK
    r0 = gla_w
    g0 = r0 + 3 * RWKV_W + RWKV_DECAY_LORA + RWKV_AAA_LORA + RWKV_GATE_LORA
    pad = jnp.zeros((w.shape[0], 256 - 2 * GLA_GATE_RANK), w.dtype)
    return jnp.concatenate([
        w[:, g0:g0 + 2 * D_MODEL],
        w[:, 0:2 * GLA_QK_W + 2 * GLA_V_W],
        w[:, r0:r0 + 3 * RWKV_W + 256],
        w[:, 2 * GLA_QK_W + 2 * GLA_V_W:gla_w],
        pad], axis=1)


def _pack_mu(mu):
    z = jnp.zeros((COL_R,), mu.dtype)
    z2 = jnp.zeros((P_COLS - COL_GS,), mu.dtype)
    return jnp.concatenate([z, mu, z2])[None, :]


def _pack_ff_cols(t):
    pad = [(0, 0)] * (t.ndim - 1) + [(0, FF_PAD - D_FF)]
    g = jnp.pad(t[..., :D_FF], pad)
    v = jnp.pad(t[..., D_FF:], pad)
    lead = t.shape[:-1]
    g = g.reshape(lead + (FF_NCHUNK, FF_CHUNK))
    v = v.reshape(lead + (FF_NCHUNK, FF_CHUNK))
    gv = jnp.concatenate([g, v], axis=-1)
    return jnp.moveaxis(gv, -2, 0)


def kernel(x, norm1_g, w_in, gla_wa2_f, gla_ba_f, gla_wa2_b, gla_ba_b, gla_norm_g, gla_proj, rwkv_mu_prev, rwkv_mu_next, rwkv_w0_f, rwkv_w2_f, rwkv_w0_b, rwkv_w2_b, rwkv_a0, rwkv_a2, rwkv_g2, rwkv_k_k, rwkv_k_a, rwkv_r_k, rwkv_ln_w, rwkv_ln_b, rwkv_proj, w_out, norm2_g, ffn_up, ffn_conv_w, ffn_conv_b, ffn_down, norm_f_g):
    batch, seq_len, d = x.shape
    assert w_in.shape[0] == 1 and d == D_MODEL and seq_len % CHUNK == 0
    x2 = x.reshape(batch * seq_len, d)
    row = lambda t: t.reshape(1, -1)
    for l in range(1):
        w_packed = _pack_w_in(w_in[l]).astype(BF16)
        wa2_f = jnp.pad(gla_wa2_f[l], ((0, 256 - GLA_GATE_RANK), (0, 0)))
        wa2_b = jnp.pad(gla_wa2_b[l], ((GLA_GATE_RANK, 256 - 2 * GLA_GATE_RANK), (0, 0)))
        zl = jnp.zeros((RWKV_AAA_LORA, RWKV_W), F32)
        w2_f = jnp.concatenate([rwkv_w2_f[l], zl], axis=0)
        w2_b = jnp.concatenate([rwkv_w2_b[l], zl], axis=0)
        a2p = jnp.concatenate([jnp.zeros((RWKV_DECAY_LORA, RWKV_W), F32), rwkv_a2[l]], axis=0).astype(BF16)

        p = _inproj(x2, row(norm1_g[l]), w_packed, _pack_mu(rwkv_mu_prev[l]),
                    _pack_mu(rwkv_mu_next[l]), seq_len)

        o_f = _gla(p, wa2_f, row(gla_ba_f[l]), batch, seq_len, reverse=False)
        oa = _gla(p, wa2_b, row(gla_ba_b[l]), batch, seq_len, reverse=True,
                  o_fwd=o_f, norm_g=row(gla_norm_g[l]))

        rw_common = (row(rwkv_a0[l]), a2p, row(rwkv_k_k[l]), row(rwkv_k_a[l]))
        y_f = _rwkv(p, row(rwkv_w0_f[l]), w2_f, *rw_common, batch, seq_len, reverse=False)
        ob = _rwkv(p, row(rwkv_w0_b[l]), w2_b, *rw_common, batch, seq_len, reverse=True,
                   y_fwd=y_f, g2=rwkv_g2[l].astype(BF16), r_k=row(rwkv_r_k[l]),
                   ln_w=row(rwkv_ln_w[l]), ln_b=row(rwkv_ln_b[l]))

        x1 = _merge(x2, oa, ob, p, gla_proj[l].astype(BF16), rwkv_proj[l].astype(BF16),
                    w_out[l].astype(BF16), seq_len)

        wu_p = _pack_ff_cols(ffn_up[l]).astype(BF16)
        cw_p = _pack_ff_cols(ffn_conv_w[l])
        cb_p = _pack_ff_cols(ffn_conv_b[l][None, :])
        wd_p = jnp.pad(ffn_down[l], ((0, FF_PAD - D_FF), (0, 0))).reshape(
            FF_NCHUNK, FF_CHUNK, D_MODEL).astype(BF16)
        x2 = _ffn(x1, row(norm2_g[l]), wu_p, cw_p, cb_p, wd_p, row(norm_f_g), seq_len)
    return x2.reshape(batch, seq_len, d)
```

```python
import functools

import jax
import jax.numpy as jnp
from jax import lax
from jax.experimental import pallas as pl
from jax.experimental.pallas import tpu as pltpu

F32 = jnp.float32
BF16 = jnp.bfloat16

D_MODEL = 1024
GLA_HEADS = 4
GLA_DK = 64
GLA_DV = 128
GLA_QK_W = GLA_HEADS * GLA_DK
GLA_V_W = GLA_HEADS * GLA_DV
GLA_GATE_RANK = 16
GLA_LOGIT_NORM = 16.0
CHUNK = 64
RWKV_HEAD = 64
RWKV_W = 512
RWKV_HEADS = RWKV_W // RWKV_HEAD
RWKV_DECAY_LORA = 64
RWKV_AAA_LORA = 64
RWKV_GATE_LORA = 128
RWKV_GN_EPS = RWKV_HEAD * 1e-5
D_FF = 2752
NORM_EPS = 1e-6
HEAD_NORM_EPS = 1e-5

COL_GA, COL_GB = 0, 1024
COL_QK, COL_V, COL_OG = 2048, 2560, 3072
COL_R, COL_RK, COL_RV = 3584, 4096, 4608
COL_RS, COL_GS = 5120, 5376
P_COLS = 5632
P_MAIN_COLS = COL_RS
HALO = 16

FF_CHUNK = 256
FF_PAD = 2816
FF_NCHUNK = FF_PAD // FF_CHUNK

VMEM_LIMIT = 56 * 1024 * 1024
ROW_TILE = 1024
FFN_ROW_TILE = 512
GLA_SEQ_TILE = 1024
RWKV_SEQ_TILE = 256
RWKV_GROUP = 4


def _sigmoid(x):
    return 1.0 / (1.0 + jnp.exp(-x))


def _softplus(z):
    return jnp.maximum(z, 0.0) + jnp.log(1.0 + jnp.exp(-jnp.abs(z)))


def _split3(x):
    hi = x.astype(BF16)
    r1 = x - hi.astype(F32)
    mid = r1.astype(BF16)
    lo = (r1 - mid.astype(F32)).astype(BF16)
    return hi, mid, lo


def _dot(a, b):
    return jnp.dot(a, b, preferred_element_type=F32)


def _dot_nt(a, b):
    return lax.dot_general(a, b, (((1,), (1,)), ((), ())), preferred_element_type=F32)


def _dot_tn(a, b):
    return lax.dot_general(a, b, (((0,), (0,)), ((), ())), preferred_element_type=F32)


def _dot_exact_lhs(sel, x):
    hi = x.astype(BF16)
    lo = (x - hi.astype(F32)).astype(BF16)
    s = sel.astype(BF16)
    return _dot(s, hi) + _dot(s, lo)


def _dot_exact_rhs(x, sel):
    hi = x.astype(BF16)
    lo = (x - hi.astype(F32)).astype(BF16)
    s = sel.astype(BF16)
    return _dot(hi, s) + _dot(lo, s)


def _dot_x3(a, b):
    ah = a.astype(BF16)
    al = (a - ah.astype(F32)).astype(BF16)
    bh = b.astype(BF16)
    bl = (b - bh.astype(F32)).astype(BF16)
    return _dot(ah, bh) + _dot(al, bh) + _dot(ah, bl)


def _rmsnorm(x, g):
    ms = jnp.mean(x * x, axis=-1, keepdims=True)
    return x * lax.rsqrt(ms + NORM_EPS) * g


def _iota2(shape, dim):
    return lax.broadcasted_iota(jnp.int32, shape, dim)


def _div(x, size):
    return lax.shift_right_logical(x, size.bit_length() - 1)


def _mod(x, size):
    return lax.bitwise_and(x, size - 1)


def _chunk_cumsum(x, reverse):
    rows = x.shape[0]
    blk = min(rows, 256)
    row = _iota2((blk, blk), 0)
    col = _iota2((blk, blk), 1)
    same = _div(row, CHUNK) == _div(col, CHUNK)
    tri = jnp.where(same & ((col >= row) if reverse else (col <= row)), 1.0, 0.0)
    parts = [_dot_exact_lhs(tri, x[i:i + blk]) for i in range(0, rows, blk)]
    return parts[0] if len(parts) == 1 else jnp.concatenate(parts, axis=0)


def _blockdiag(z, head_masks):
    return jnp.concatenate([jnp.where(m, z, 0.0) for m in head_masks], axis=0).astype(BF16)


def _inproj_kernel(xp_ref, x_ref, xn_ref, g_ref, w_ref, mup_ref, mun_ref, o_ref, os_ref,
                   h_ref, p_ref, *, tm, tiles_per_seq, first_shift_tile, n_main_tiles, n_sub):
    i = pl.program_id(0)
    j = pl.program_id(1)

    @pl.when(j == 0)
    def _():
        pos = lax.rem(i, tiles_per_seq)
        keep_prev = jnp.where(pos == 0, 0.0, 1.0)
        keep_next = jnp.where(pos == tiles_per_seq - 1, 0.0, 1.0)
        g = g_ref[...]
        h_ref[0:HALO, :] = _rmsnorm(xp_ref[...] * keep_prev, g).astype(BF16)
        h_ref[HALO:HALO + tm, :] = _rmsnorm(x_ref[...], g).astype(BF16)
        h_ref[HALO + tm:HALO + tm + HALO, :] = _rmsnorm(xn_ref[...] * keep_next, g).astype(BF16)

    @pl.when(j < first_shift_tile)
    def _():
        o_ref[...] = _dot(h_ref[HALO:HALO + tm, :], w_ref[...]).astype(o_ref.dtype)

    def shifted(dst_ref):
        sub = tm // n_sub
        mm = lambda s: _dot(h_ref[s * sub:(s + 1) * sub + 2 * HALO, :], w_ref[...])
        p_next = mm(0)
        for s in range(n_sub):
            slot = s % 2
            p_ref[slot] = p_next
            if s + 1 < n_sub:
                p_next = mm(s + 1)
            p = p_ref[slot, HALO:HALO + sub, :]
            pp = p_ref[slot, HALO - 1:HALO - 1 + sub, :]
            pn = p_ref[slot, HALO + 1:HALO + 1 + sub, :]
            dst_ref[s * sub:(s + 1) * sub, :] = (
                p + mup_ref[...] * (pp - p) + mun_ref[...] * (pn - p)).astype(dst_ref.dtype)

    @pl.when((j >= first_shift_tile) & (j < n_main_tiles))
    def _():
        shifted(o_ref)

    @pl.when(j == n_main_tiles)
    def _():
        shifted(os_ref)


def _inproj(x2, g, w_packed, mu_prev, mu_next, seq_len):
    n = x2.shape[0]
    tm = min(ROW_TILE, seq_len)
    tn = 512
    tiles_per_seq = seq_len // tm
    hb = tm // HALO
    nhb = n // HALO
    n_sub = max(1, tm // 256)
    sub = tm // n_sub
    kern = functools.partial(_inproj_kernel, tm=tm, tiles_per_seq=tiles_per_seq,
                             first_shift_tile=COL_R // tn, n_main_tiles=P_MAIN_COLS // tn,
                             n_sub=n_sub)
    last_main = P_MAIN_COLS // tn - 1
    return pl.pallas_call(
        kern,
        grid=(n // tm, P_COLS // tn),
        in_specs=[
            pl.BlockSpec((HALO, D_MODEL), lambda i, j: (jnp.maximum(i * hb - 1, 0), 0)),
            pl.BlockSpec((tm, D_MODEL), lambda i, j: (i, 0)),
            pl.BlockSpec((HALO, D_MODEL), lambda i, j: (jnp.minimum((i + 1) * hb, nhb - 1), 0)),
            pl.BlockSpec((1, D_MODEL), lambda i, j: (0, 0)),
            pl.BlockSpec((D_MODEL, tn), lambda i, j: (0, j)),
            pl.BlockSpec((1, tn), lambda i, j: (0, j)),
            pl.BlockSpec((1, tn), lambda i, j: (0, j)),
        ],
        out_specs=[pl.BlockSpec((tm, tn), lambda i, j: (i, jnp.minimum(j, last_main))),
                   pl.BlockSpec((tm, tn), lambda i, j: (i, 0))],
        out_shape=[jax.ShapeDtypeStruct((n, P_MAIN_COLS), BF16),
                   jax.ShapeDtypeStruct((n, P_COLS - P_MAIN_COLS), F32)],
        scratch_shapes=[
            pltpu.VMEM((tm + 2 * HALO, D_MODEL), BF16),
            pltpu.VMEM((2, sub + 2 * HALO, tn), F32),
        ],
        compiler_params=pltpu.CompilerParams(
            dimension_semantics=("arbitrary", "arbitrary"),
            vmem_limit_bytes=VMEM_LIMIT),
        name="inproj",
    )(x2, x2, x2, g, w_packed, mu_prev, mu_next)


def _gla_kernel(*refs, reverse, tb, final):
    if final:
        (qk_ref, v_ref, gs_ref, wa2_ref, ba_ref, of_ref, og_ref, ng_ref,
         o_ref, st_ref, b_ref, acc_ref) = refs
    else:
        qk_ref, v_ref, gs_ref, wa2_ref, ba_ref, o_ref, st_ref, b_ref = refs
    j = pl.program_id(1)
    nc = tb // CHUNK

    @pl.when(j == 0)
    def _():
        st_ref[...] = jnp.zeros_like(st_ref)

    logit = _dot_x3(gs_ref[...], wa2_ref[...]) + ba_ref[...]
    la = (jnp.minimum(logit, 0.0) - jnp.log(1.0 + jnp.exp(-jnp.abs(logit)))) * (1.0 / GLA_LOGIT_NORM)
    b_ref[...] = _chunk_cumsum(la, reverse)

    lane_k = _div(_iota2((CHUNK, GLA_QK_W), 1), GLA_DK)
    lane_v = _div(_iota2((CHUNK, GLA_V_W), 1), GLA_DV)
    k_masks = [lane_k == h for h in range(GLA_HEADS)]
    v_masks = [lane_v == h for h in range(GLA_HEADS)]
    t_idx = _iota2((CHUNK, GLA_QK_W), 0)
    s_idx = _mod(_iota2((CHUNK, GLA_QK_W), 1), CHUNK)
    causal = (s_idx >= t_idx) if reverse else (s_idx <= t_idx)
    st_mask = (_div(_iota2((GLA_V_W, GLA_QK_W), 0), GLA_DV)
               == _div(_iota2((GLA_V_W, GLA_QK_W), 1), GLA_DK))
    i_ref = (CHUNK - 1 - CHUNK // 2) if reverse else CHUNK // 2
    i_last = 0 if reverse else CHUNK - 1

    QI, KD, QE, VB, KI_BD, V_BD, DEC = ([] for _ in range(7))
    for c in range(nc):
        rows = slice(c * CHUNK, (c + 1) * CHUNK)
        b = b_ref[rows, :]
        b_mid = b[i_ref:i_ref + 1, :]
        b_last = b[i_last:i_last + 1, :]
        q = qk_ref[rows, 0:GLA_QK_W].astype(F32) * (GLA_DK ** -0.5)
        k = qk_ref[rows, GLA_QK_W:2 * GLA_QK_W].astype(F32)
        v = v_ref[rows, :].astype(F32)
        QI.append((q * jnp.exp(b - b_mid)).astype(BF16))
        KD.append((k * jnp.exp(b_last - b)).astype(BF16))
        QE.append((q * jnp.exp(b)).astype(BF16))
        VB.append(v.astype(BF16))
        KI_BD.append(_blockdiag(k * jnp.exp(b_mid - b), k_masks))
        V_BD.append(_blockdiag(v, v_masks))
        DEC.append(jnp.exp(b_last))
    A = [jnp.where(causal, _dot_nt(qi, kb), 0.0) for qi, kb in zip(QI, KI_BD)]
    O_IN = [_dot(a.astype(BF16), vb) for a, vb in zip(A, V_BD)]
    KV = [jnp.where(st_mask, _dot_tn(vb, kd), 0.0) for vb, kd in zip(VB, KD)]
    for ci in range(nc):
        c = (nc - 1 - ci) if reverse else ci
        rows = slice(c * CHUNK, (c + 1) * CHUNK)
        st = st_ref[...]
        o = O_IN[c] + _dot_nt(QE[c], st.astype(BF16))
        st_ref[...] = st * DEC[c] + KV[c]
        if final:
            acc_ref[rows, :] = of_ref[rows, :] + o
        else:
            o_ref[rows, :] = o

    if final:
        o = acc_ref[...]
        parts = []
        for h in range(GLA_HEADS):
            oh = o[:, h * GLA_DV:(h + 1) * GLA_DV]
            ms = jnp.mean(oh * oh, axis=-1, keepdims=True)
            parts.append(oh * lax.rsqrt(ms + HEAD_NORM_EPS))
        on = jnp.concatenate(parts, axis=1) * ng_ref[...]
        og = og_ref[...].astype(F32)
        o_ref[...] = on * (og * _sigmoid(og))


def _gla(p, ps, wa2p, ba, batch, seq_len, reverse, o_fwd=None, norm_g=None):
    final = o_fwd is not None
    tb = min(GLA_SEQ_TILE, seq_len)
    nt = seq_len // tb
    n = batch * seq_len

    def rowblk(b, j):
        return b * nt + ((nt - 1 - j) if reverse else j)

    in_specs = [
        pl.BlockSpec((tb, 512), lambda b, j: (rowblk(b, j), COL_QK // 512)),
        pl.BlockSpec((tb, 512), lambda b, j: (rowblk(b, j), COL_V // 512)),
        pl.BlockSpec((tb, 256), lambda b, j: (rowblk(b, j), (COL_GS - P_MAIN_COLS) // 256)),
        pl.BlockSpec((256, GLA_QK_W), lambda b, j: (0, 0)),
        pl.BlockSpec((1, GLA_QK_W), lambda b, j: (0, 0)),
    ]
    args = [p, p, ps, wa2p, ba]
    scratch = [pltpu.VMEM((GLA_V_W, GLA_QK_W), F32), pltpu.VMEM((tb, GLA_QK_W), F32)]
    if final:
        in_specs += [
            pl.BlockSpec((tb, GLA_V_W), lambda b, j: (rowblk(b, j), 0)),
            pl.BlockSpec((tb, 512), lambda b, j: (rowblk(b, j), COL_OG // 512)),
            pl.BlockSpec((1, GLA_V_W), lambda b, j: (0, 0)),
        ]
        args += [o_fwd, p, norm_g]
        scratch.append(pltpu.VMEM((tb, GLA_V_W), F32))
    kern = functools.partial(_gla_kernel, reverse=reverse, tb=tb, final=final)
    return pl.pallas_call(
        kern,
        grid=(batch, nt),
        in_specs=in_specs,
        out_specs=pl.BlockSpec((tb, GLA_V_W), lambda b, j: (rowblk(b, j), 0)),
        out_shape=jax.ShapeDtypeStruct((n, GLA_V_W), F32),
        scratch_shapes=scratch,
        compiler_params=pltpu.CompilerParams(
            dimension_semantics=("arbitrary", "arbitrary"),
            vmem_limit_bytes=VMEM_LIMIT),
        name="gla_bwd" if reverse else "gla_fwd",
    )(*args)


def _rwkv_kernel(*refs, reverse, tb, final):
    if final:
        (r_ref, k_ref, v_ref, rs_ref, w0_ref, w2_ref, a0_ref, a2_ref, kk_ref, ka_ref,
         yf_ref, g2_ref, rk_ref, lnw_ref, lnb_ref,
         o_ref, p_ref, cs_s, lw_s, kk_s, ba_s, k2_s, reff_s, oloc_s, gm_s, hm_s, y_s) = refs
    else:
        (r_ref, k_ref, v_ref, rs_ref, w0_ref, w2_ref, a0_ref, a2_ref, kk_ref, ka_ref,
         o_ref, p_ref, cs_s, lw_s, kk_s, ba_s, k2_s, reff_s, oloc_s, gm_s, hm_s) = refs
        y_s = o_ref
    j = pl.program_id(1)
    nc = tb // CHUNK
    G = RWKV_GROUP
    L = G * RWKV_HEAD
    ngrp = RWKV_W // L

    @pl.when(j == 0)
    def _():
        p_ref[...] = jnp.zeros_like(p_ref)

    r = r_ref[...].astype(F32)
    k = k_ref[...].astype(F32)
    lora_in = rs_ref[:, 0:RWKV_DECAY_LORA + RWKV_AAA_LORA]
    w_pre = w0_ref[...] + _dot_x3(jnp.tanh(lora_in), w2_ref[...])
    logw = -jnp.exp(-_softplus(-w_pre) - 0.5)
    a_lr = _sigmoid(a0_ref[...] + _dot(lora_in.astype(BF16), a2_ref[...]))
    hr = _div(_iota2((RWKV_W, RWKV_W), 0), RWKV_HEAD)
    hc = _div(_iota2((RWKV_W, RWKV_W), 1), RWKV_HEAD)
    head_ones = jnp.where(hr == hc, 1.0, 0.0).astype(BF16)
    head_sum = lambda t: _dot(t.astype(BF16), head_ones)
    kk = k * kk_ref[...]
    ssq = head_sum(kk * kk)
    kk = kk / jnp.maximum(jnp.sqrt(ssq), 1e-12)
    k2 = k * (1.0 + (a_lr - 1.0) * ka_ref[...])

    cs_s[...] = _chunk_cumsum(logw, reverse)
    lw_s[...] = logw
    kk_s[...] = kk
    ba_s[...] = kk * a_lr
    k2_s[...] = k2
    i_last = 0 if reverse else CHUNK - 1

    lane_h = _div(_iota2((CHUNK, L), 1), RWKV_HEAD)
    h_masks = [lane_h == h for h in range(G)]
    t_idx = _iota2((CHUNK, L), 0)
    s_idx = _mod(_iota2((CHUNK, L), 1), CHUNK)
    strict = (s_idx > t_idx) if reverse else (s_idx < t_idx)
    incl = (s_idx >= t_idx) if reverse else (s_idx <= t_idx)
    eye_p = jnp.where(s_idx == t_idx, 1.0, 0.0)
    bd_mask = _div(_iota2((L, L), 0), RWKV_HEAD) == _div(_iota2((L, L), 1), RWKV_HEAD)
    eye_l = jnp.where(_iota2((L, L), 0) == _iota2((L, L), 1), 1.0, 0.0)

    def blockdiag(z):
        return _blockdiag(z, h_masks)

    chains = [(c, gi) for c in range(nc) for gi in range(ngrp)]
    rows_of = lambda c: slice(c * CHUNK, (c + 1) * CHUNK)
    lanes_of = lambda gi: slice(gi * L, (gi + 1) * L)
    AT, BT, KT, RT, BD, KD, GL, V = ([] for _ in range(8))
    for c in range(nc):
        rows = rows_of(c)
        cs = cs_s[rows, :]
        cs_tot = cs[i_last:i_last + 1, :]
        g_inv = jnp.exp(-cs)
        g_rem = jnp.exp(cs_tot - cs)
        at_c = (-kk_s[rows, :]) * jnp.exp(cs - lw_s[rows, :])
        bt_c = ba_s[rows, :] * g_inv
        kt_c = k2_s[rows, :] * g_inv
        rt_c = r_ref[rows, :].astype(F32) * jnp.exp(cs)
        bd_c = ba_s[rows, :] * g_rem
        kd_c = k2_s[rows, :] * g_rem
        gl_c = jnp.exp(cs_tot)
        for gi in range(ngrp):
            lanes = lanes_of(gi)
            AT.append(at_c[:, lanes])
            BT.append(bt_c[:, lanes])
            KT.append(kt_c[:, lanes])
            RT.append(rt_c[:, lanes])
            BD.append(bd_c[:, lanes])
            KD.append(kd_c[:, lanes])
            GL.append(gl_c[:, lanes])
            V.append(v_ref[rows, lanes].astype(F32))
    bf = lambda t: t.astype(BF16)
    cat0 = lambda a, b: jnp.concatenate([a, b], axis=0)
    cat1 = lambda a, b: jnp.concatenate([a, b], axis=1)
    X = [bf(cat0(a, r_)) for a, r_ in zip(AT, RT)]
    G1 = [_dot_nt(x, blockdiag(b)) for x, b in zip(X, BT)]
    G2 = [_dot_nt(x, blockdiag(k_)) for x, k_ in zip(X, KT)]
    LAB = [jnp.where(strict, g[0:CHUNK], 0.0) for g in G1]
    MRB = [jnp.where(incl, g[CHUNK:2 * CHUNK], 0.0) for g in G1]
    LAK = [jnp.where(strict, g[0:CHUNK], 0.0) for g in G2]
    MRK = [jnp.where(incl, g[CHUNK:2 * CHUNK], 0.0) for g in G2]
    TM = [eye_p + l for l in LAB]
    XP = [_dot(bf(l), blockdiag(l)) for l in LAB]
    for _ in range(4):
        Y = [_dot(bf(cat0(t, x)), blockdiag(x)) for t, x in zip(TM, XP)]
        TM = [t + y[0:CHUNK] for t, y in zip(TM, Y)]
        XP = [y[CHUNK:2 * CHUNK] for y in Y]
    TM = [t + _dot(bf(t), blockdiag(x)) for t, x in zip(TM, XP)]
    VBD = [blockdiag(v) for v in V]
    LMV = [_dot(bf(cat0(l, m)), vb) for l, m, vb in zip(LAK, MRK, VBD)]
    LAKV = [t[0:CHUNK] for t in LMV]
    MV = [t[CHUNK:2 * CHUNK] for t in LMV]
    WU =[_dot(bf(t), cat1(blockdiag(a), blockdiag(lv))) for t, a, lv in zip(TM, AT, LAKV)]
    WA = [wu[:, 0:L] for wu in WU]
    UV = [wu[:, L:2 * L] for wu in WU]
    RO = [_dot(bf(m), cat1(blockdiag(wa), blockdiag(uv))) for m, wa, uv in zip(MRB, WA, UV)]
    GH = [_dot_tn(bf(bd), bf(cat1(wa, uv))) for bd, wa, uv in zip(BD, WA, UV)]
    KV = [_dot_tn(bf(kd), bf(v)) for kd, v in zip(KD, V)]
    for n_, (c, gi) in enumerate(chains):
        rows, lanes = rows_of(c), lanes_of(gi)
        reff_s[rows, lanes] = RT[n_] + RO[n_][:, 0:L]
        oloc_s[rows, lanes] = RO[n_][:, L:2 * L] + MV[n_]
        gm_s[c, gi] = bf(eye_l * GL[n_] + jnp.where(bd_mask, GH[n_][:, 0:L], 0.0))
        hm_s[c, gi] = jnp.where(bd_mask, GH[n_][:, L:2 * L] + KV[n_], 0.0)

    for ci in range(nc):
        c = (nc - 1 - ci) if reverse else ci
        rows = slice(c * CHUNK, (c + 1) * CHUNK)
        for gi in range(ngrp):
            lanes = slice(gi * L, (gi + 1) * L)
            pst = p_ref[gi].astype(BF16)
            lhs = jnp.concatenate([reff_s[rows, lanes].astype(BF16), gm_s[c, gi]], axis=0)
            yp = _dot(lhs, pst)
            y_s[rows, lanes] = yp[0:CHUNK] + oloc_s[rows, lanes]
            p_ref[gi] = yp[CHUNK:CHUNK + L] + hm_s[c, gi]

    if final:
        y = y_s[...] + yf_ref[...]
        inv_n = 1.0 / RWKV_HEAD
        mu = head_sum(y) * inv_n
        yc = y - mu
        var = head_sum(yc * yc) * inv_n
        yn = yc * lax.rsqrt(var + RWKV_GN_EPS) * lnw_ref[...] + lnb_ref[...]
        bonus = head_sum(r * k2 * rk_ref[...]) * v_ref[...].astype(F32)
        gl = rs_ref[:, RWKV_DECAY_LORA + RWKV_AAA_LORA:]
        g = _dot(_sigmoid(gl).astype(BF16), g2_ref[...])
        o_ref[...] = (yn + bonus) * g


def _rwkv(p, ps, w0, w2p, a0, a2p, k_k, k_a, batch, seq_len, reverse,
          y_fwd=None, g2=None, r_k=None, ln_w=None, ln_b=None):
    final = y_fwd is not None
    tb = min(RWKV_SEQ_TILE, seq_len)
    nt = seq_len // tb
    n = batch * seq_len
    L = RWKV_GROUP * RWKV_HEAD
    ngrp = RWKV_W // L
    lora = RWKV_DECAY_LORA + RWKV_AAA_LORA

    def rowblk(b, j):
        return b * nt + ((nt - 1 - j) if reverse else j)

    vec = pl.BlockSpec((1, RWKV_W), lambda b, j: (0, 0))
    in_specs = [
        pl.BlockSpec((tb, 512), lambda b, j: (rowblk(b, j), COL_R // 512)),
        pl.BlockSpec((tb, 512), lambda b, j: (rowblk(b, j), COL_RK // 512)),
        pl.BlockSpec((tb, 512), lambda b, j: (rowblk(b, j), COL_RV // 512)),
        pl.BlockSpec((tb, 256), lambda b, j: (rowblk(b, j), (COL_RS - P_MAIN_COLS) // 256)),
        vec,
        pl.BlockSpec((lora, RWKV_W), lambda b, j: (0, 0)),
        vec,
        pl.BlockSpec((lora, RWKV_W), lambda b, j: (0, 0)),
        vec, vec,
    ]
    args = [p, p, p, ps, w0, w2p, a0, a2p, k_k, k_a]
    nc = tb // CHUNK
    scratch = ([pltpu.VMEM((ngrp, L, L), F32)] + [pltpu.VMEM((tb, RWKV_W), F32)] * 7
               + [pltpu.VMEM((nc, ngrp, L, L), BF16), pltpu.VMEM((nc, ngrp, L, L), F32)])
    if final:
        in_specs += [
            pl.BlockSpec((tb, RWKV_W), lambda b, j: (rowblk(b, j), 0)),
            pl.BlockSpec((RWKV_GATE_LORA, RWKV_W), lambda b, j: (0, 0)),
            vec, vec, vec,
        ]
        args += [y_fwd, g2, r_k, ln_w, ln_b]
        scratch.append(pltpu.VMEM((tb, RWKV_W), F32))
    kern = functools.partial(_rwkv_kernel, reverse=reverse, tb=tb, final=final)
    return pl.pallas_call(
        kern,
        grid=(batch, nt),
        in_specs=in_specs,
        out_specs=pl.BlockSpec((tb, RWKV_W), lambda b, j: (rowblk(b, j), 0)),
        out_shape=jax.ShapeDtypeStruct((n, RWKV_W), F32),
        scratch_shapes=scratch,
        compiler_params=pltpu.CompilerParams(
            dimension_semantics=("arbitrary", "arbitrary"),
            vmem_limit_bytes=VMEM_LIMIT),
        name="rwkv_bwd" if reverse else "rwkv_fwd",
    )(*args)


def _merge_kernel(x_ref, oa_ref, ob_ref, ga_ref, gb_ref, wa_ref, wb_ref, wo_ref, o_ref):
    ya = _dot(oa_ref[...].astype(BF16), wa_ref[...])
    yb = _dot(ob_ref[...].astype(BF16), wb_ref[...])
    merged = (_sigmoid(ga_ref[...].astype(F32)) * ya
              + _sigmoid(gb_ref[...].astype(F32)) * yb)
    o_ref[...] = x_ref[...] + _dot(merged.astype(BF16), wo_ref[...])


def _merge(x2, oa, ob, p, gla_proj, rwkv_proj, w_out, seq_len):
    n = x2.shape[0]
    tm = min(ROW_TILE, seq_len)
    return pl.pallas_call(
        _merge_kernel,
        grid=(n // tm,),
        in_specs=[
            pl.BlockSpec((tm, D_MODEL), lambda i: (i, 0)),
            pl.BlockSpec((tm, GLA_V_W), lambda i: (i, 0)),
            pl.BlockSpec((tm, RWKV_W), lambda i: (i, 0)),
            pl.BlockSpec((tm, D_MODEL), lambda i: (i, COL_GA // D_MODEL)),
            pl.BlockSpec((tm, D_MODEL), lambda i: (i, COL_GB // D_MODEL)),
            pl.BlockSpec((GLA_V_W, D_MODEL), lambda i: (0, 0)),
            pl.BlockSpec((RWKV_W, D_MODEL), lambda i: (0, 0)),
            pl.BlockSpec((D_MODEL, D_MODEL), lambda i: (0, 0)),
        ],
        out_specs=pl.BlockSpec((tm, D_MODEL), lambda i: (i, 0)),
        out_shape=jax.ShapeDtypeStruct((n, D_MODEL), F32),
        compiler_params=pltpu.CompilerParams(
            dimension_semantics=("arbitrary",),
            vmem_limit_bytes=VMEM_LIMIT),
        name="merge",
    )(x2, oa, ob, p, p, gla_proj, rwkv_proj, w_out)


def _ffn_kernel(xp_ref, x_ref, xn_ref, g2_ref, wu_ref, cw_ref, cb_ref, wd_ref, gf_ref, o_ref,
                h_ref, u_ref, acc_ref, *, tm, tiles_per_seq):
    i = pl.program_id(0)
    pos = lax.rem(i, tiles_per_seq)
    keep_prev = jnp.where(pos == 0, 0.0, 1.0)
    keep_next = jnp.where(pos == tiles_per_seq - 1, 0.0, 1.0)
    g2 = g2_ref[...]
    h_ref[0:HALO, :] = _rmsnorm(xp_ref[...] * keep_prev, g2).astype(BF16)
    h_ref[HALO:HALO + tm, :] = _rmsnorm(x_ref[...], g2).astype(BF16)
    h_ref[HALO + tm:HALO + tm + HALO, :] = _rmsnorm(xn_ref[...] * keep_next, g2).astype(BF16)

    up = lambda c: _dot(h_ref[...], wu_ref[c])
    u_next = up(0)
    for c in range(FF_NCHUNK):
        slot = c % 2
        u_ref[slot] = u_next
        if c + 1 < FF_NCHUNK:
            u_next = up(c + 1)
        cw = cw_ref[c]
        uc = (cw[0:1, :] * u_ref[slot, HALO - 1:HALO - 1 + tm, :]
              + cw[1:2, :] * u_ref[slot, HALO:HALO + tm, :]
              + cw[2:3, :] * u_ref[slot, HALO + 1:HALO + 1 + tm, :]
              + cb_ref[c])
        ug = uc[:, 0:FF_CHUNK]
        act = (ug * _sigmoid(ug)) * uc[:, FF_CHUNK:2 * FF_CHUNK]
        down = _dot(act.astype(BF16), wd_ref[c])
        if c == 0:
            acc_ref[...] = down
        else:
            acc_ref[...] += down
    o_ref[...] = _rmsnorm(x_ref[...] + acc_ref[...], gf_ref[...])


def _ffn(x1, g2, wu_p, cw_p, cb_p, wd_p, gf, seq_len):
    n = x1.shape[0]
    tm = min(FFN_ROW_TILE, seq_len)
    tiles_per_seq = seq_len // tm
    hb = tm // HALO
    nhb = n // HALO
    kern = functools.partial(_ffn_kernel, tm=tm, tiles_per_seq=tiles_per_seq)
    return pl.pallas_call(
        kern,
        grid=(n // tm,),
        in_specs=[
            pl.BlockSpec((HALO, D_MODEL), lambda i: (jnp.maximum(i * hb - 1, 0), 0)),
            pl.BlockSpec((tm, D_MODEL), lambda i: (i, 0)),
            pl.BlockSpec((HALO, D_MODEL), lambda i: (jnp.minimum((i + 1) * hb, nhb - 1), 0)),
            pl.BlockSpec((1, D_MODEL), lambda i: (0, 0)),
            pl.BlockSpec((FF_NCHUNK, D_MODEL, 2 * FF_CHUNK), lambda i: (0, 0, 0)),
            pl.BlockSpec((FF_NCHUNK, 3, 2 * FF_CHUNK), lambda i: (0, 0, 0)),
            pl.BlockSpec((FF_NCHUNK, 1, 2 * FF_CHUNK), lambda i: (0, 0, 0)),
            pl.BlockSpec((FF_NCHUNK, FF_CHUNK, D_MODEL), lambda i: (0, 0, 0)),
            pl.BlockSpec((1, D_MODEL), lambda i: (0, 0)),
        ],
        out_specs=pl.BlockSpec((tm, D_MODEL), lambda i: (i, 0)),
        out_shape=jax.ShapeDtypeStruct((n, D_MODEL), F32),
        scratch_shapes=[
            pltpu.VMEM((tm + 2 * HALO, D_MODEL), BF16),
            pltpu.VMEM((2, tm + 2 * HALO, 2 * FF_CHUNK), F32),
            pltpu.VMEM((tm, D_MODEL), F32),
        ],
        compiler_params=pltpu.CompilerParams(
            dimension_semantics=("arbitrary",),
            vmem_limit_bytes=VMEM_LIMIT),
        name="ffn",
    )(x1, x1, x1, g2, wu_p, cw_p, cb_p, wd_p, gf)


def _pack_w_in(w):
    gla_w = 2 * GLA_QK_W + 2 * GLA_V_W + 2 * GLA_GATE_RANK
    r0 = gla_w
    g0 = r0 + 3 * RWKV_W + RWKV_DECAY_LORA + RWKV_AAA_LORA + RWKV_GATE_LORA
    pad = jnp.zeros((w.shape[0], 256 - 2 * GLA_GATE_RANK), w.dtype)
    return jnp.concatenate([
        w[:, g0:g0 + 2 * D_MODEL],
        w[:, 0:2 * GLA_QK_W + 2 * GLA_V_W],
        w[:, r0:r0 + 3 * RWKV_W + 256],
        w[:, 2 * GLA_QK_W + 2 * GLA_V_W:gla_w],
        pad], axis=1)


def _pack_mu(mu):
    z = jnp.zeros((COL_R,), mu.dtype)
    z2 = jnp.zeros((P_COLS - COL_GS,), mu.dtype)
    return jnp.concatenate([z, mu, z2])[None, :]


def _pack_ff_cols(t):
    pad = [(0, 0)] * (t.ndim - 1) + [(0, FF_PAD - D_FF)]
    g = jnp.pad(t[..., :D_FF], pad)
    v = jnp.pad(t[..., D_FF:], pad)
    lead = t.shape[:-1]
    g = g.reshape(lead + (FF_NCHUNK, FF_CHUNK))
    v = v.reshape(lead + (FF_NCHUNK, FF_CHUNK))
    gv = jnp.concatenate([g, v], axis=-1)
    return jnp.moveaxis(gv, -2, 0)


def kernel(x, norm1_g, w_in, gla_wa2_f, gla_ba_f, gla_wa2_b, gla_ba_b, gla_norm_g, gla_proj, rwkv_mu_prev, rwkv_mu_next, rwkv_w0_f, rwkv_w2_f, rwkv_w0_b, rwkv_w2_b, rwkv_a0, rwkv_a2, rwkv_g2, rwkv_k_k, rwkv_k_a, rwkv_r_k, rwkv_ln_w, rwkv_ln_b, rwkv_proj, w_out, norm2_g, ffn_up, ffn_conv_w, ffn_conv_b, ffn_down, norm_f_g):
    batch, seq_len, d = x.shape
    assert w_in.shape[0] == 1 and d == D_MODEL and seq_len % CHUNK == 0
    x2 = x.reshape(batch * seq_len, d)
    row = lambda t: t.reshape(1, -1)
    for l in range(1):
        w_packed = _pack_w_in(w_in[l]).astype(BF16)
        wa2_f = jnp.pad(gla_wa2_f[l], ((0, 256 - GLA_GATE_RANK), (0, 0)))
        wa2_b = jnp.pad(gla_wa2_b[l], ((GLA_GATE_RANK, 256 - 2 * GLA_GATE_RANK), (0, 0)))
        zl = jnp.zeros((RWKV_AAA_LORA, RWKV_W), F32)
        w2_f = jnp.concatenate([rwkv_w2_f[l], zl], axis=0)
        w2_b = jnp.concatenate([rwkv_w2_b[l], zl], axis=0)
        a2p = jnp.concatenate([jnp.zeros((RWKV_DECAY_LORA, RWKV_W), F32), rwkv_a2[l]], axis=0).astype(BF16)

        p, ps = _inproj(x2, row(norm1_g[l]), w_packed, _pack_mu(rwkv_mu_prev[l]),
                        _pack_mu(rwkv_mu_next[l]), seq_len)

        o_f = _gla(p, ps, wa2_f, row(gla_ba_f[l]), batch, seq_len, reverse=False)
        oa = _gla(p, ps, wa2_b, row(gla_ba_b[l]), batch, seq_len, reverse=True,
                  o_fwd=o_f, norm_g=row(gla_norm_g[l]))

        rw_common = (row(rwkv_a0[l]), a2p, row(rwkv_k_k[l]), row(rwkv_k_a[l]))
        y_f = _rwkv(p, ps, row(rwkv_w0_f[l]), w2_f, *rw_common, batch, seq_len, reverse=False)
        ob = _rwkv(p, ps, row(rwkv_w0_b[l]), w2_b, *rw_common, batch, seq_len, reverse=True,
                   y_fwd=y_f, g2=rwkv_g2[l].astype(BF16), r_k=row(rwkv_r_k[l]),
                   ln_w=row(rwkv_ln_w[l]), ln_b=row(rwkv_ln_b[l]))

        x1 = _merge(x2, oa, ob, p, gla_proj[l].astype(BF16), rwkv_proj[l].astype(BF16),
                    w_out[l].astype(BF16), seq_len)

        wu_p = _pack_ff_cols(ffn_up[l]).astype(BF16)
        cw_p = _pack_ff_cols(ffn_conv_w[l])
        cb_p = _pack_ff_cols(ffn_conv_b[l][None, :])
        wd_p = jnp.pad(ffn_down[l], ((0, FF_PAD - D_FF), (0, 0))).reshape(
            FF_NCHUNK, FF_CHUNK, D_MODEL).astype(BF16)
        x2 = _ffn(x1, row(norm2_g[l]), wu_p, cw_p, cb_p, wd_p, row(norm_f_g), seq_len)
    return x2.reshape(batch, seq_len, d)
```

```python
import functools

import jax
import jax.numpy as jnp
from jax import lax
from jax.experimental import pallas as pl
from jax.experimental.pallas import tpu as pltpu

F32 = jnp.float32
BF16 = jnp.bfloat16

D_MODEL = 1024
GLA_HEADS = 4
GLA_DK = 64
GLA_DV = 128
GLA_QK_W = GLA_HEADS * GLA_DK
GLA_V_W = GLA_HEADS * GLA_DV
GLA_GATE_RANK = 16
GLA_LOGIT_NORM = 16.0
CHUNK = 64
RWKV_HEAD = 64
RWKV_W = 512
RWKV_HEADS = RWKV_W // RWKV_HEAD
RWKV_DECAY_LORA = 64
RWKV_AAA_LORA = 64
RWKV_GATE_LORA = 128
RWKV_GN_EPS = RWKV_HEAD * 1e-5
D_FF = 2752
NORM_EPS = 1e-6
HEAD_NORM_EPS = 1e-5

COL_GA, COL_GB = 0, 1024
COL_QK, COL_V, COL_OG = 2048, 2560, 3072
COL_R, COL_RK, COL_RV = 3584, 4096, 4608
COL_RS, COL_GS = 5120, 5376
P_COLS = 5632
P_MAIN_COLS = COL_RS
HALO = 16

FF_CHUNK = 256
FF_PAD = 2816
FF_NCHUNK = FF_PAD // FF_CHUNK

VMEM_LIMIT = 56 * 1024 * 1024
ROW_TILE = 1024
INPROJ_ROW_TILE = 512
FFN_ROW_TILE = 512
GLA_SEQ_TILE = 1024
RWKV_SEQ_TILE = 256
RWKV_GROUP = 4


def _sigmoid(x):
    return 1.0 / (1.0 + jnp.exp(-x))


def _softplus(z):
    return jnp.maximum(z, 0.0) + jnp.log(1.0 + jnp.exp(-jnp.abs(z)))


def _split3(x):
    hi = x.astype(BF16)
    r1 = x - hi.astype(F32)
    mid = r1.astype(BF16)
    lo = (r1 - mid.astype(F32)).astype(BF16)
    return hi, mid, lo


def _dot(a, b):
    return jnp.dot(a, b, preferred_element_type=F32)


def _dot_nt(a, b):
    return lax.dot_general(a, b, (((1,), (1,)), ((), ())), preferred_element_type=F32)


def _dot_tn(a, b):
    return lax.dot_general(a, b, (((0,), (0,)), ((), ())), preferred_element_type=F32)


def _dot_exact_lhs(sel, x):
    hi = x.astype(BF16)
    lo = (x - hi.astype(F32)).astype(BF16)
    s = sel.astype(BF16)
    return _dot(s, hi) + _dot(s, lo)


def _dot_exact_rhs(x, sel):
    hi = x.astype(BF16)
    lo = (x - hi.astype(F32)).astype(BF16)
    s = sel.astype(BF16)
    return _dot(hi, s) + _dot(lo, s)


def _dot_x3(a, b):
    ah = a.astype(BF16)
    al = (a - ah.astype(F32)).astype(BF16)
    bh = b.astype(BF16)
    bl = (b - bh.astype(F32)).astype(BF16)
    return _dot(ah, bh) + _dot(al, bh) + _dot(ah, bl)


def _rmsnorm(x, g):
    ms = jnp.mean(x * x, axis=-1, keepdims=True)
    return x * lax.rsqrt(ms + NORM_EPS) * g


def _iota2(shape, dim):
    return lax.broadcasted_iota(jnp.int32, shape, dim)


def _div(x, size):
    return lax.shift_right_logical(x, size.bit_length() - 1)


def _mod(x, size):
    return lax.bitwise_and(x, size - 1)


def _chunk_cumsum(x, reverse):
    rows = x.shape[0]
    blk = min(rows, 256)
    row = _iota2((blk, blk), 0)
    col = _iota2((blk, blk), 1)
    same = _div(row, CHUNK) == _div(col, CHUNK)
    tri = jnp.where(same & ((col >= row) if reverse else (col <= row)), 1.0, 0.0)
    parts = [_dot_exact_lhs(tri, x[i:i + blk]) for i in range(0, rows, blk)]
    return parts[0] if len(parts) == 1 else jnp.concatenate(parts, axis=0)


def _blockdiag(z, head_masks):
    return jnp.concatenate([jnp.where(m, z, 0.0) for m in head_masks], axis=0).astype(BF16)


def _inproj_kernel(xp_ref, x_ref, xn_ref, g_ref, w_ref, mup_ref, mun_ref, o_ref, os_ref,
                   h_ref, p_ref, *, tm, tiles_per_seq, first_shift_tile, n_main_tiles, n_sub):
    i = pl.program_id(0)
    n_tiles, _, tn = w_ref.shape
    pos = lax.rem(i, tiles_per_seq)
    keep_prev = jnp.where(pos == 0, 0.0, 1.0)
    keep_next = jnp.where(pos == tiles_per_seq - 1, 0.0, 1.0)
    g = g_ref[...]
    h_ref[0:HALO, :] = _rmsnorm(xp_ref[...] * keep_prev, g).astype(BF16)
    h_ref[HALO:HALO + tm, :] = _rmsnorm(x_ref[...], g).astype(BF16)
    h_ref[HALO + tm:HALO + tm + HALO, :] = _rmsnorm(xn_ref[...] * keep_next, g).astype(BF16)

    for t in range(first_shift_tile):
        o_ref[:, t * tn:(t + 1) * tn] = _dot(h_ref[HALO:HALO + tm, :], w_ref[t]).astype(o_ref.dtype)

    sub = tm // n_sub
    jobs = [(t, s) for t in range(first_shift_tile, n_tiles) for s in range(n_sub)]
    mm = lambda t, s: _dot(h_ref[s * sub:(s + 1) * sub + 2 * HALO, :], w_ref[t])
    p_ref[0] = mm(*jobs[0])
    for n_, (t, s) in enumerate(jobs):
        slot = n_ % 2
        if n_ + 1 < len(jobs):
            p_ref[1 - slot] = mm(*jobs[n_ + 1])
        p = p_ref[slot, HALO:HALO + sub, :]
        pp = p_ref[slot, HALO - 1:HALO - 1 + sub, :]
        pn = p_ref[slot, HALO + 1:HALO + 1 + sub, :]
        val = p + mup_ref[t] * (pp - p) + mun_ref[t] * (pn - p)
        rows = slice(s * sub, (s + 1) * sub)
        if t < n_main_tiles:
            o_ref[rows, t * tn:(t + 1) * tn] = val.astype(o_ref.dtype)
        else:
            os_ref[rows, :] = val


def _inproj(x2, g, w_packed, mu_prev, mu_next, seq_len):
    n = x2.shape[0]
    tm = min(INPROJ_ROW_TILE, seq_len)
    tn = 512
    n_tiles = P_COLS // tn
    tiles_per_seq = seq_len // tm
    hb = tm // HALO
    nhb = n // HALO
    n_sub = max(1, tm // 256)
    sub = tm // n_sub
    kern = functools.partial(_inproj_kernel, tm=tm, tiles_per_seq=tiles_per_seq,
                             first_shift_tile=COL_R // tn, n_main_tiles=P_MAIN_COLS // tn,
                             n_sub=n_sub)
    w_t = jnp.moveaxis(w_packed.reshape(D_MODEL, n_tiles, tn), 1, 0)
    mup_t = jnp.moveaxis(mu_prev.reshape(1, n_tiles, tn), 1, 0)
    mun_t = jnp.moveaxis(mu_next.reshape(1, n_tiles, tn), 1, 0)
    return pl.pallas_call(
        kern,
        grid=(n // tm,),
        in_specs=[
            pl.BlockSpec((HALO, D_MODEL), lambda i: (jnp.maximum(i * hb - 1, 0), 0)),
            pl.BlockSpec((tm, D_MODEL), lambda i: (i, 0)),
            pl.BlockSpec((HALO, D_MODEL), lambda i: (jnp.minimum((i + 1) * hb, nhb - 1), 0)),
            pl.BlockSpec((1, D_MODEL), lambda i: (0, 0)),
            pl.BlockSpec((n_tiles, D_MODEL, tn), lambda i: (0, 0, 0)),
            pl.BlockSpec((n_tiles, 1, tn), lambda i: (0, 0, 0)),
            pl.BlockSpec((n_tiles, 1, tn), lambda i: (0, 0, 0)),
        ],
        out_specs=[pl.BlockSpec((tm, P_MAIN_COLS), lambda i: (i, 0)),
                   pl.BlockSpec((tm, P_COLS - P_MAIN_COLS), lambda i: (i, 0))],
        out_shape=[jax.ShapeDtypeStruct((n, P_MAIN_COLS), BF16),
                   jax.ShapeDtypeStruct((n, P_COLS - P_MAIN_COLS), F32)],
        scratch_shapes=[
            pltpu.VMEM((tm + 2 * HALO, D_MODEL), BF16),
            pltpu.VMEM((2, sub + 2 * HALO, tn), F32),
        ],
        compiler_params=pltpu.CompilerParams(
            dimension_semantics=("arbitrary",),
            vmem_limit_bytes=VMEM_LIMIT),
        name="inproj",
    )(x2, x2, x2, g, w_t, mup_t, mun_t)


def _gla_kernel(*refs, reverse, tb, final):
    if final:
        (qk_ref, v_ref, gs_ref, wa2_ref, ba_ref, of_ref, og_ref, ng_ref,
         o_ref, st_ref, b_ref, acc_ref) = refs
    else:
        qk_ref, v_ref, gs_ref, wa2_ref, ba_ref, o_ref, st_ref, b_ref = refs
    j = pl.program_id(1)
    nc = tb // CHUNK

    @pl.when(j == 0)
    def _():
        st_ref[...] = jnp.zeros_like(st_ref)

    logit = _dot_x3(gs_ref[...], wa2_ref[...]) + ba_ref[...]
    la = (jnp.minimum(logit, 0.0) - jnp.log(1.0 + jnp.exp(-jnp.abs(logit)))) * (1.0 / GLA_LOGIT_NORM)
    b_ref[...] = _chunk_cumsum(la, reverse)

    lane_k = _div(_iota2((CHUNK, GLA_QK_W), 1), GLA_DK)
    lane_v = _div(_iota2((CHUNK, GLA_V_W), 1), GLA_DV)
    k_masks = [lane_k == h for h in range(GLA_HEADS)]
    v_masks = [lane_v == h for h in range(GLA_HEADS)]
    t_idx = _iota2((CHUNK, GLA_QK_W), 0)
    s_idx = _mod(_iota2((CHUNK, GLA_QK_W), 1), CHUNK)
    causal = (s_idx >= t_idx) if reverse else (s_idx <= t_idx)
    st_mask = (_div(_iota2((GLA_V_W, GLA_QK_W), 0), GLA_DV)
               == _div(_iota2((GLA_V_W, GLA_QK_W), 1), GLA_DK))
    i_ref = (CHUNK - 1 - CHUNK // 2) if reverse else CHUNK // 2
    i_last = 0 if reverse else CHUNK - 1

    QI, KD, QE, VB, KI_BD, V_BD, DEC = ([] for _ in range(7))
    for c in range(nc):
        rows = slice(c * CHUNK, (c + 1) * CHUNK)
        b = b_ref[rows, :]
        b_mid = b[i_ref:i_ref + 1, :]
        b_last = b[i_last:i_last + 1, :]
        q = qk_ref[rows, 0:GLA_QK_W].astype(F32) * (GLA_DK ** -0.5)
        k = qk_ref[rows, GLA_QK_W:2 * GLA_QK_W].astype(F32)
        v = v_ref[rows, :].astype(F32)
        QI.append((q * jnp.exp(b - b_mid)).astype(BF16))
        KD.append((k * jnp.exp(b_last - b)).astype(BF16))
        QE.append((q * jnp.exp(b)).astype(BF16))
        VB.append(v.astype(BF16))
        KI_BD.append(_blockdiag(k * jnp.exp(b_mid - b), k_masks))
        V_BD.append(_blockdiag(v, v_masks))
        DEC.append(jnp.exp(b_last))
    A = [jnp.where(causal, _dot_nt(qi, kb), 0.0) for qi, kb in zip(QI, KI_BD)]
    O_IN = [_dot(a.astype(BF16), vb) for a, vb in zip(A, V_BD)]
    KV = [jnp.where(st_mask, _dot_tn(vb, kd), 0.0) for vb, kd in zip(VB, KD)]
    for ci in range(nc):
        c = (nc - 1 - ci) if reverse else ci
        rows = slice(c * CHUNK, (c + 1) * CHUNK)
        st = st_ref[...]
        o = O_IN[c] + _dot_nt(QE[c], st.astype(BF16))
        st_ref[...] = st * DEC[c] + KV[c]
        if final:
            acc_ref[rows, :] = of_ref[rows, :] + o
        else:
            o_ref[rows, :] = o

    if final:
        o = acc_ref[...]
        parts = []
        for h in range(GLA_HEADS):
            oh = o[:, h * GLA_DV:(h + 1) * GLA_DV]
            ms = jnp.mean(oh * oh, axis=-1, keepdims=True)
            parts.append(oh * lax.rsqrt(ms + HEAD_NORM_EPS))
        on = jnp.concatenate(parts, axis=1) * ng_ref[...]
        og = og_ref[...].astype(F32)
        o_ref[...] = on * (og * _sigmoid(og))


def _gla(p, ps, wa2p, ba, batch, seq_len, reverse, o_fwd=None, norm_g=None):
    final = o_fwd is not None
    tb = min(GLA_SEQ_TILE, seq_len)
    nt = seq_len // tb
    n = batch * seq_len

    def rowblk(b, j):
        return b * nt + ((nt - 1 - j) if reverse else j)

    in_specs = [
        pl.BlockSpec((tb, 512), lambda b, j: (rowblk(b, j), COL_QK // 512)),
        pl.BlockSpec((tb, 512), lambda b, j: (rowblk(b, j), COL_V // 512)),
        pl.BlockSpec((tb, 256), lambda b, j: (rowblk(b, j), (COL_GS - P_MAIN_COLS) // 256)),
        pl.BlockSpec((256, GLA_QK_W), lambda b, j: (0, 0)),
        pl.BlockSpec((1, GLA_QK_W), lambda b, j: (0, 0)),
    ]
    args = [p, p, ps, wa2p, ba]
    scratch = [pltpu.VMEM((GLA_V_W, GLA_QK_W), F32), pltpu.VMEM((tb, GLA_QK_W), F32)]
    if final:
        in_specs += [
            pl.BlockSpec((tb, GLA_V_W), lambda b, j: (rowblk(b, j), 0)),
            pl.BlockSpec((tb, 512), lambda b, j: (rowblk(b, j), COL_OG // 512)),
            pl.BlockSpec((1, GLA_V_W), lambda b, j: (0, 0)),
        ]
        args += [o_fwd, p, norm_g]
        scratch.append(pltpu.VMEM((tb, GLA_V_W), F32))
    kern = functools.partial(_gla_kernel, reverse=reverse, tb=tb, final=final)
    return pl.pallas_call(
        kern,
        grid=(batch, nt),
        in_specs=in_specs,
        out_specs=pl.BlockSpec((tb, GLA_V_W), lambda b, j: (rowblk(b, j), 0)),
        out_shape=jax.ShapeDtypeStruct((n, GLA_V_W), F32),
        scratch_shapes=scratch,
        compiler_params=pltpu.CompilerParams(
            dimension_semantics=("arbitrary", "arbitrary"),
            vmem_limit_bytes=VMEM_LIMIT),
        name="gla_bwd" if reverse else "gla_fwd",
    )(*args)


def _rwkv_kernel(*refs, reverse, tb, final):
    if final:
        (r_ref, k_ref, v_ref, rs_ref, w0_ref, w2_ref, a0_ref, a2_ref, kk_ref, ka_ref,
         yf_ref, g2_ref, rk_ref, lnw_ref, lnb_ref,
         o_ref, p_ref, cs_s, lw_s, kk_s, ba_s, k2_s, reff_s, oloc_s, gm_s, hm_s, y_s) = refs
    else:
        (r_ref, k_ref, v_ref, rs_ref, w0_ref, w2_ref, a0_ref, a2_ref, kk_ref, ka_ref,
         o_ref, p_ref, cs_s, lw_s, kk_s, ba_s, k2_s, reff_s, oloc_s, gm_s, hm_s) = refs
        y_s = o_ref
    j = pl.program_id(1)
    nc = tb // CHUNK
    G = RWKV_GROUP
    L = G * RWKV_HEAD
    ngrp = RWKV_W // L

    @pl.when(j == 0)
    def _():
        p_ref[...] = jnp.zeros_like(p_ref)

    r = r_ref[...].astype(F32)
    k = k_ref[...].astype(F32)
    lora_in = rs_ref[:, 0:RWKV_DECAY_LORA + RWKV_AAA_LORA]
    w_pre = w0_ref[...] + _dot_x3(jnp.tanh(lora_in), w2_ref[...])
    logw = -jnp.exp(-_softplus(-w_pre) - 0.5)
    a_lr = _sigmoid(a0_ref[...] + _dot(lora_in.astype(BF16), a2_ref[...]))
    hr = _div(_iota2((RWKV_W, RWKV_W), 0), RWKV_HEAD)
    hc = _div(_iota2((RWKV_W, RWKV_W), 1), RWKV_HEAD)
    head_ones = jnp.where(hr == hc, 1.0, 0.0).astype(BF16)
    head_sum = lambda t: _dot(t.astype(BF16), head_ones)
    kk = k * kk_ref[...]
    ssq = head_sum(kk * kk)
    kk = kk / jnp.maximum(jnp.sqrt(ssq), 1e-12)
    k2 = k * (1.0 + (a_lr - 1.0) * ka_ref[...])

    cs_s[...] = _chunk_cumsum(logw, reverse)
    lw_s[...] = logw
    kk_s[...] = kk
    ba_s[...] = kk * a_lr
    k2_s[...] = k2
    i_last = 0 if reverse else CHUNK - 1

    lane_h = _div(_iota2((CHUNK, L), 1), RWKV_HEAD)
    h_masks = [lane_h == h for h in range(G)]
    t_idx = _iota2((CHUNK, L), 0)
    s_idx = _mod(_iota2((CHUNK, L), 1), CHUNK)
    strict = (s_idx > t_idx) if reverse else (s_idx < t_idx)
    incl = (s_idx >= t_idx) if reverse else (s_idx <= t_idx)
    eye_p = jnp.where(s_idx == t_idx, 1.0, 0.0)
    bd_mask = _div(_iota2((L, L), 0), RWKV_HEAD) == _div(_iota2((L, L), 1), RWKV_HEAD)
    eye_l = jnp.where(_iota2((L, L), 0) == _iota2((L, L), 1), 1.0, 0.0)

    def blockdiag(z):
        return _blockdiag(z, h_masks)

    chains = [(c, gi) for c in range(nc) for gi in range(ngrp)]
    rows_of = lambda c: slice(c * CHUNK, (c + 1) * CHUNK)
    lanes_of = lambda gi: slice(gi * L, (gi + 1) * L)
    AT, BT, KT, RT, BD, KD, GL, V = ([] for _ in range(8))
    for c in range(nc):
        rows = rows_of(c)
        cs = cs_s[rows, :]
        cs_tot = cs[i_last:i_last + 1, :]
        g_inv = jnp.exp(-cs)
        g_rem = jnp.exp(cs_tot - cs)
        at_c = (-kk_s[rows, :]) * jnp.exp(cs - lw_s[rows, :])
        bt_c = ba_s[rows, :] * g_inv
        kt_c = k2_s[rows, :] * g_inv
        rt_c = r_ref[rows, :].astype(F32) * jnp.exp(cs)
        bd_c = ba_s[rows, :] * g_rem
        kd_c = k2_s[rows, :] * g_rem
        gl_c = jnp.exp(cs_tot)
        for gi in range(ngrp):
            lanes = lanes_of(gi)
            AT.append(at_c[:, lanes])
            BT.append(bt_c[:, lanes])
            KT.append(kt_c[:, lanes])
            RT.append(rt_c[:, lanes])
            BD.append(bd_c[:, lanes])
            KD.append(kd_c[:, lanes])
            GL.append(gl_c[:, lanes])
            V.append(v_ref[rows, lanes].astype(F32))
    bf = lambda t: t.astype(BF16)
    cat0 = lambda a, b: jnp.concatenate([a, b], axis=0)
    cat1 = lambda a, b: jnp.concatenate([a, b], axis=1)
    X = [bf(cat0(a, r_)) for a, r_ in zip(AT, RT)]
    G1 = [_dot_nt(x, blockdiag(b)) for x, b in zip(X, BT)]
    G2 = [_dot_nt(x, blockdiag(k_)) for x, k_ in zip(X, KT)]
    LAB = [jnp.where(strict, g[0:CHUNK], 0.0) for g in G1]
    MRB = [jnp.where(incl, g[CHUNK:2 * CHUNK], 0.0) for g in G1]
    LAK = [jnp.where(strict, g[0:CHUNK], 0.0) for g in G2]
    MRK = [jnp.where(incl, g[CHUNK:2 * CHUNK], 0.0) for g in G2]
    TM = [eye_p + l for l in LAB]
    XP = [_dot(bf(l), blockdiag(l)) for l in LAB]
    for _ in range(4):
        Y = [_dot(bf(cat0(t, x)), blockdiag(x)) for t, x in zip(TM, XP)]
        TM = [t + y[0:CHUNK] for t, y in zip(TM, Y)]
        XP = [y[CHUNK:2 * CHUNK] for y in Y]
    TM = [t + _dot(bf(t), blockdiag(x)) for t, x in zip(TM, XP)]
    VBD = [blockdiag(v) for v in V]
    LMV = [_dot(bf(cat0(l, m)), vb) for l, m, vb in zip(LAK, MRK, VBD)]
    LAKV = [t[0:CHUNK] for t in LMV]
    MV = [t[CHUNK:2 * CHUNK] for t in LMV]
    WU =[_dot(bf(t), cat1(blockdiag(a), blockdiag(lv))) for t, a, lv in zip(TM, AT, LAKV)]
    WA = [wu[:, 0:L] for wu in WU]
    UV = [wu[:, L:2 * L] for wu in WU]
    RO = [_dot(bf(m), cat1(blockdiag(wa), blockdiag(uv))) for m, wa, uv in zip(MRB, WA, UV)]
    GH = [_dot_tn(bf(bd), bf(cat1(wa, uv))) for bd, wa, uv in zip(BD, WA, UV)]
    KV = [_dot_tn(bf(kd), bf(v)) for kd, v in zip(KD, V)]
    for n_, (c, gi) in enumerate(chains):
        rows, lanes = rows_of(c), lanes_of(gi)
        reff_s[rows, lanes] = RT[n_] + RO[n_][:, 0:L]
        oloc_s[rows, lanes] = RO[n_][:, L:2 * L] + MV[n_]
        gm_s[c, gi] = bf(eye_l * GL[n_] + jnp.where(bd_mask, GH[n_][:, 0:L], 0.0))
        hm_s[c, gi] = jnp.where(bd_mask, GH[n_][:, L:2 * L] + KV[n_], 0.0)

    for ci in range(nc):
        c = (nc - 1 - ci) if reverse else ci
        rows = slice(c * CHUNK, (c + 1) * CHUNK)
        for gi in range(ngrp):
            lanes = slice(gi * L, (gi + 1) * L)
            pst = p_ref[gi].astype(BF16)
            lhs = jnp.concatenate([reff_s[rows, lanes].astype(BF16), gm_s[c, gi]], axis=0)
            yp = _dot(lhs, pst)
            y_s[rows, lanes] = yp[0:CHUNK] + oloc_s[rows, lanes]
            p_ref[gi] = yp[CHUNK:CHUNK + L] + hm_s[c, gi]

    if final:
        y = y_s[...] + yf_ref[...]
        inv_n = 1.0 / RWKV_HEAD
        mu = head_sum(y) * inv_n
        yc = y - mu
        var = head_sum(yc * yc) * inv_n
        yn = yc * lax.rsqrt(var + RWKV_GN_EPS) * lnw_ref[...] + lnb_ref[...]
        bonus = head_sum(r * k2 * rk_ref[...]) * v_ref[...].astype(F32)
        gl = rs_ref[:, RWKV_DECAY_LORA + RWKV_AAA_LORA:]
        g = _dot(_sigmoid(gl).astype(BF16), g2_ref[...])
        o_ref[...] = (yn + bonus) * g


def _rwkv(p, ps, w0, w2p, a0, a2p, k_k, k_a, batch, seq_len, reverse,
          y_fwd=None, g2=None, r_k=None, ln_w=None, ln_b=None):
    final = y_fwd is not None
    tb = min(RWKV_SEQ_TILE, seq_len)
    nt = seq_len // tb
    n = batch * seq_len
    L = RWKV_GROUP * RWKV_HEAD
    ngrp = RWKV_W // L
    lora = RWKV_DECAY_LORA + RWKV_AAA_LORA

    def rowblk(b, j):
        return b * nt + ((nt - 1 - j) if reverse else j)

    vec = pl.BlockSpec((1, RWKV_W), lambda b, j: (0, 0))
    in_specs = [
        pl.BlockSpec((tb, 512), lambda b, j: (rowblk(b, j), COL_R // 512)),
        pl.BlockSpec((tb, 512), lambda b, j: (rowblk(b, j), COL_RK // 512)),
        pl.BlockSpec((tb, 512), lambda b, j: (rowblk(b, j), COL_RV // 512)),
        pl.BlockSpec((tb, 256), lambda b, j: (rowblk(b, j), (COL_RS - P_MAIN_COLS) // 256)),
        vec,
        pl.BlockSpec((lora, RWKV_W), lambda b, j: (0, 0)),
        vec,
        pl.BlockSpec((lora, RWKV_W), lambda b, j: (0, 0)),
        vec, vec,
    ]
    args = [p, p, p, ps, w0, w2p, a0, a2p, k_k, k_a]
    nc = tb // CHUNK
    scratch = ([pltpu.VMEM((ngrp, L, L), F32)] + [pltpu.VMEM((tb, RWKV_W), F32)] * 7
               + [pltpu.VMEM((nc, ngrp, L, L), BF16), pltpu.VMEM((nc, ngrp, L, L), F32)])
    if final:
        in_specs += [
            pl.BlockSpec((tb, RWKV_W), lambda b, j: (rowblk(b, j), 0)),
            pl.BlockSpec((RWKV_GATE_LORA, RWKV_W), lambda b, j: (0, 0)),
            vec, vec, vec,
        ]
        args += [y_fwd, g2, r_k, ln_w, ln_b]
        scratch.append(pltpu.VMEM((tb, RWKV_W), F32))
    kern = functools.partial(_rwkv_kernel, reverse=reverse, tb=tb, final=final)
    return pl.pallas_call(
        kern,
        grid=(batch, nt),
        in_specs=in_specs,
        out_specs=pl.BlockSpec((tb, RWKV_W), lambda b, j: (rowblk(b, j), 0)),
        out_shape=jax.ShapeDtypeStruct((n, RWKV_W), F32),
        scratch_shapes=scratch,
        compiler_params=pltpu.CompilerParams(
            dimension_semantics=("arbitrary", "arbitrary"),
            vmem_limit_bytes=VMEM_LIMIT),
        name="rwkv_bwd" if reverse else "rwkv_fwd",
    )(*args)


def _merge_kernel(x_ref, oa_ref, ob_ref, ga_ref, gb_ref, wa_ref, wb_ref, wo_ref, o_ref):
    ya = _dot(oa_ref[...].astype(BF16), wa_ref[...])
    yb = _dot(ob_ref[...].astype(BF16), wb_ref[...])
    merged = (_sigmoid(ga_ref[...].astype(F32)) * ya
              + _sigmoid(gb_ref[...].astype(F32)) * yb)
    o_ref[...] = x_ref[...] + _dot(merged.astype(BF16), wo_ref[...])


def _merge(x2, oa, ob, p, gla_proj, rwkv_proj, w_out, seq_len):
    n = x2.shape[0]
    tm = min(ROW_TILE, seq_len)
    return pl.pallas_call(
        _merge_kernel,
        grid=(n // tm,),
        in_specs=[
            pl.BlockSpec((tm, D_MODEL), lambda i: (i, 0)),
            pl.BlockSpec((tm, GLA_V_W), lambda i: (i, 0)),
            pl.BlockSpec((tm, RWKV_W), lambda i: (i, 0)),
            pl.BlockSpec((tm, D_MODEL), lambda i: (i, COL_GA // D_MODEL)),
            pl.BlockSpec((tm, D_MODEL), lambda i: (i, COL_GB // D_MODEL)),
            pl.BlockSpec((GLA_V_W, D_MODEL), lambda i: (0, 0)),
            pl.BlockSpec((RWKV_W, D_MODEL), lambda i: (0, 0)),
            pl.BlockSpec((D_MODEL, D_MODEL), lambda i: (0, 0)),
        ],
        out_specs=pl.BlockSpec((tm, D_MODEL), lambda i: (i, 0)),
        out_shape=jax.ShapeDtypeStruct((n, D_MODEL), F32),
        compiler_params=pltpu.CompilerParams(
            dimension_semantics=("arbitrary",),
            vmem_limit_bytes=VMEM_LIMIT),
        name="merge",
    )(x2, oa, ob, p, p, gla_proj, rwkv_proj, w_out)


def _ffn_kernel(xp_ref, x_ref, xn_ref, g2_ref, wu_ref, cw_ref, cb_ref, wd_ref, gf_ref, o_ref,
                h_ref, u_ref, *, tm, tiles_per_seq):
    i = pl.program_id(0)
    pos = lax.rem(i, tiles_per_seq)
    keep_prev = jnp.where(pos == 0, 0.0, 1.0)
    keep_next = jnp.where(pos == tiles_per_seq - 1, 0.0, 1.0)
    g2 = g2_ref[...]
    h_ref[0:HALO, :] = _rmsnorm(xp_ref[...] * keep_prev, g2).astype(BF16)
    h_ref[HALO:HALO + tm, :] = _rmsnorm(x_ref[...], g2).astype(BF16)
    h_ref[HALO + tm:HALO + tm + HALO, :] = _rmsnorm(xn_ref[...] * keep_next, g2).astype(BF16)

    up = lambda c: _dot(h_ref[...], wu_ref[c])
    u_ref[0] = up(0)
    for c in range(FF_NCHUNK):
        slot = c % 2
        if c + 1 < FF_NCHUNK:
            u_ref[1 - slot] = up(c + 1)
        cw = cw_ref[c]
        uc = (cw[0:1, :] * u_ref[slot, HALO - 1:HALO - 1 + tm, :]
              + cw[1:2, :] * u_ref[slot, HALO:HALO + tm, :]
              + cw[2:3, :] * u_ref[slot, HALO + 1:HALO + 1 + tm, :]
              + cb_ref[c])
        ug = uc[:, 0:FF_CHUNK]
        act = (ug * _sigmoid(ug)) * uc[:, FF_CHUNK:2 * FF_CHUNK]
        down = _dot(act.astype(BF16), wd_ref[c])
        acc = down if c == 0 else acc + down
    o_ref[...] = _rmsnorm(x_ref[...] + acc, gf_ref[...])


def _ffn(x1, g2, wu_p, cw_p, cb_p, wd_p, gf, seq_len):
    n = x1.shape[0]
    tm = min(FFN_ROW_TILE, seq_len)
    tiles_per_seq = seq_len // tm
    hb = tm // HALO
    nhb = n // HALO
    kern = functools.partial(_ffn_kernel, tm=tm, tiles_per_seq=tiles_per_seq)
    return pl.pallas_call(
        kern,
        grid=(n // tm,),
        in_specs=[
            pl.BlockSpec((HALO, D_MODEL), lambda i: (jnp.maximum(i * hb - 1, 0), 0)),
            pl.BlockSpec((tm, D_MODEL), lambda i: (i, 0)),
            pl.BlockSpec((HALO, D_MODEL), lambda i: (jnp.minimum((i + 1) * hb, nhb - 1), 0)),
            pl.BlockSpec((1, D_MODEL), lambda i: (0, 0)),
            pl.BlockSpec((FF_NCHUNK, D_MODEL, 2 * FF_CHUNK), lambda i: (0, 0, 0)),
            pl.BlockSpec((FF_NCHUNK, 3, 2 * FF_CHUNK), lambda i: (0, 0, 0)),
            pl.BlockSpec((FF_NCHUNK, 1, 2 * FF_CHUNK), lambda i: (0, 0, 0)),
            pl.BlockSpec((FF_NCHUNK, FF_CHUNK, D_MODEL), lambda i: (0, 0, 0)),
            pl.BlockSpec((1, D_MODEL), lambda i: (0, 0)),
        ],
        out_specs=pl.BlockSpec((tm, D_MODEL), lambda i: (i, 0)),
        out_shape=jax.ShapeDtypeStruct((n, D_MODEL), F32),
        scratch_shapes=[
            pltpu.VMEM((tm + 2 * HALO, D_MODEL), BF16),
            pltpu.VMEM((2, tm + 2 * HALO, 2 * FF_CHUNK), F32),
        ],
        compiler_params=pltpu.CompilerParams(
            dimension_semantics=("arbitrary",),
            vmem_limit_bytes=VMEM_LIMIT),
        name="ffn",
    )(x1, x1, x1, g2, wu_p, cw_p, cb_p, wd_p, gf)


def _pack_w_in(w):
    gla_w = 2 * GLA_QK_W + 2 * GLA_V_W + 2 * GLA_GATE_RANK
    r0 = gla_w
    g0 = r0 + 3 * RWKV_W + RWKV_DECAY_LORA + RWKV_AAA_LORA + RWKV_GATE_LORA
    pad = jnp.zeros((w.shape[0], 256 - 2 * GLA_GATE_RANK), w.dtype)
    return jnp.concatenate([
        w[:, g0:g0 + 2 * D_MODEL],
        w[:, 0:2 * GLA_QK_W + 2 * GLA_V_W],
        w[:, r0:r0 + 3 * RWKV_W + 256],
        w[:, 2 * GLA_QK_W + 2 * GLA_V_W:gla_w],
        pad], axis=1)


def _pack_mu(mu):
    z = jnp.zeros((COL_R,), mu.dtype)
    z2 = jnp.zeros((P_COLS - COL_GS,), mu.dtype)
    return jnp.concatenate([z, mu, z2])[None, :]


def _pack_ff_cols(t):
    pad = [(0, 0)] * (t.ndim - 1) + [(0, FF_PAD - D_FF)]
    g = jnp.pad(t[..., :D_FF], pad)
    v = jnp.pad(t[..., D_FF:], pad)
    lead = t.shape[:-1]
    g = g.reshape(lead + (FF_NCHUNK, FF_CHUNK))
    v = v.reshape(lead + (FF_NCHUNK, FF_CHUNK))
    gv = jnp.concatenate([g, v], axis=-1)
    return jnp.moveaxis(gv, -2, 0)


def kernel(x, norm1_g, w_in, gla_wa2_f, gla_ba_f, gla_wa2_b, gla_ba_b, gla_norm_g, gla_proj, rwkv_mu_prev, rwkv_mu_next, rwkv_w0_f, rwkv_w2_f, rwkv_w0_b, rwkv_w2_b, rwkv_a0, rwkv_a2, rwkv_g2, rwkv_k_k, rwkv_k_a, rwkv_r_k, rwkv_ln_w, rwkv_ln_b, rwkv_proj, w_out, norm2_g, ffn_up, ffn_conv_w, ffn_conv_b, ffn_down, norm_f_g):
    batch, seq_len, d = x.shape
    assert w_in.shape[0] == 1 and d == D_MODEL and seq_len % CHUNK == 0
    x2 = x.reshape(batch * seq_len, d)
    row = lambda t: t.reshape(1, -1)
    for l in range(1):
        w_packed = _pack_w_in(w_in[l]).astype(BF16)
        wa2_f = jnp.pad(gla_wa2_f[l], ((0, 256 - GLA_GATE_RANK), (0, 0)))
        wa2_b = jnp.pad(gla_wa2_b[l], ((GLA_GATE_RANK, 256 - 2 * GLA_GATE_RANK), (0, 0)))
        zl = jnp.zeros((RWKV_AAA_LORA, RWKV_W), F32)
        w2_f = jnp.concatenate([rwkv_w2_f[l], zl], axis=0)
        w2_b = jnp.concatenate([rwkv_w2_b[l], zl], axis=0)
        a2p = jnp.concatenate([jnp.zeros((RWKV_DECAY_LORA, RWKV_W), F32), rwkv_a2[l]], axis=0).astype(BF16)

        p, ps = _inproj(x2, row(norm1_g[l]), w_packed, _pack_mu(rwkv_mu_prev[l]),
                        _pack_mu(rwkv_mu_next[l]), seq_len)

        o_f = _gla(p, ps, wa2_f, row(gla_ba_f[l]), batch, seq_len, reverse=False)
        oa = _gla(p, ps, wa2_b, row(gla_ba_b[l]), batch, seq_len, reverse=True,
                  o_fwd=o_f, norm_g=row(gla_norm_g[l]))

        rw_common = (row(rwkv_a0[l]), a2p, row(rwkv_k_k[l]), row(rwkv_k_a[l]))
        y_f = _rwkv(p, ps, row(rwkv_w0_f[l]), w2_f, *rw_common, batch, seq_len, reverse=False)
        ob = _rwkv(p, ps, row(rwkv_w0_b[l]), w2_b, *rw_common, batch, seq_len, reverse=True,
                   y_fwd=y_f, g2=rwkv_g2[l].astype(BF16), r_k=row(rwkv_r_k[l]),
                   ln_w=row(rwkv_ln_w[l]), ln_b=row(rwkv_ln_b[l]))

        x1 = _merge(x2, oa, ob, p, gla_proj[l].astype(BF16), rwkv_proj[l].astype(BF16),
                    w_out[l].astype(BF16), seq_len)

        wu_p = _pack_ff_cols(ffn_up[l]).astype(BF16)
        cw_p = _pack_ff_cols(ffn_conv_w[l])
        cb_p = _pack_ff_cols(ffn_conv_b[l][None, :])
        wd_p = jnp.pad(ffn_down[l], ((0, FF_PAD - D_FF), (0, 0))).reshape(
            FF_NCHUNK, FF_CHUNK, D_MODEL).astype(BF16)
        x2 = _ffn(x1, row(norm2_g[l]), wu_p, cw_p, cb_p, wd_p, row(norm_f_g), seq_len)
    return x2.reshape(batch, seq_len, d)
```

```python
import functools

import jax
import jax.numpy as jnp
from jax import lax
from jax.experimental import pallas as pl
from jax.experimental.pallas import tpu as pltpu

F32 = jnp.float32
BF16 = jnp.bfloat16

D_MODEL = 1024
GLA_HEADS = 4
GLA_DK = 64
GLA_DV = 128
GLA_QK_W = GLA_HEADS * GLA_DK
GLA_V_W = GLA_HEADS * GLA_DV
GLA_GATE_RANK = 16
GLA_LOGIT_NORM = 16.0
CHUNK = 64
RWKV_HEAD = 64
RWKV_W = 512
RWKV_HEADS = RWKV_W // RWKV_HEAD
RWKV_DECAY_LORA = 64
RWKV_AAA_LORA = 64
RWKV_GATE_LORA = 128
RWKV_GN_EPS = RWKV_HEAD * 1e-5
D_FF = 2752
NORM_EPS = 1e-6
HEAD_NORM_EPS = 1e-5

COL_GA, COL_GB = 0, 1024
COL_QK, COL_V, COL_OG = 2048, 2560, 3072
COL_R, COL_RK, COL_RV = 3584, 4096, 4608
COL_RS, COL_GS = 5120, 5376
P_COLS = 5632
P_MAIN_COLS = COL_RS
HALO = 16

FF_CHUNK = 256
FF_PAD = 2816
FF_NCHUNK = FF_PAD // FF_CHUNK

VMEM_LIMIT = 56 * 1024 * 1024
ROW_TILE = 1024
INPROJ_ROW_TILE = 512
FFN_ROW_TILE = 512
GLA_SEQ_TILE = 1024
RWKV_SEQ_TILE = 256
RWKV_GROUP = 4


def _sigmoid(x):
    return 1.0 / (1.0 + jnp.exp(-x))


def _softplus(z):
    return jnp.maximum(z, 0.0) + jnp.log(1.0 + jnp.exp(-jnp.abs(z)))


def _split3(x):
    hi = x.astype(BF16)
    r1 = x - hi.astype(F32)
    mid = r1.astype(BF16)
    lo = (r1 - mid.astype(F32)).astype(BF16)
    return hi, mid, lo


def _dot(a, b):
    return jnp.dot(a, b, preferred_element_type=F32)


def _dot_nt(a, b):
    return lax.dot_general(a, b, (((1,), (1,)), ((), ())), preferred_element_type=F32)


def _dot_tn(a, b):
    return lax.dot_general(a, b, (((0,), (0,)), ((), ())), preferred_element_type=F32)


def _dot_exact_lhs(sel, x):
    hi = x.astype(BF16)
    lo = (x - hi.astype(F32)).astype(BF16)
    s = sel.astype(BF16)
    return _dot(s, hi) + _dot(s, lo)


def _dot_exact_rhs(x, sel):
    hi = x.astype(BF16)
    lo = (x - hi.astype(F32)).astype(BF16)
    s = sel.astype(BF16)
    return _dot(hi, s) + _dot(lo, s)


def _dot_x3(a, b):
    ah = a.astype(BF16)
    al = (a - ah.astype(F32)).astype(BF16)
    bh = b.astype(BF16)
    bl = (b - bh.astype(F32)).astype(BF16)
    return _dot(ah, bh) + _dot(al, bh) + _dot(ah, bl)


def _rmsnorm(x, g):
    ms = jnp.mean(x * x, axis=-1, keepdims=True)
    return x * lax.rsqrt(ms + NORM_EPS) * g


def _iota2(shape, dim):
    return lax.broadcasted_iota(jnp.int32, shape, dim)


def _div(x, size):
    return lax.shift_right_logical(x, size.bit_length() - 1)


def _mod(x, size):
    return lax.bitwise_and(x, size - 1)


def _chunk_cumsum(x, reverse):
    rows = x.shape[0]
    blk = min(rows, 256)
    row = _iota2((blk, blk), 0)
    col = _iota2((blk, blk), 1)
    same = _div(row, CHUNK) == _div(col, CHUNK)
    tri = jnp.where(same & ((col >= row) if reverse else (col <= row)), 1.0, 0.0)
    parts = [_dot_exact_lhs(tri, x[i:i + blk]) for i in range(0, rows, blk)]
    return parts[0] if len(parts) == 1 else jnp.concatenate(parts, axis=0)


def _blockdiag(z, head_masks):
    return jnp.concatenate([jnp.where(m, z, 0.0) for m in head_masks], axis=0).astype(BF16)


def _inproj_kernel(xp_ref, x_ref, xn_ref, g_ref, w_ref, mup_ref, mun_ref, o_ref, os_ref,
                   h_ref, p_ref, *, tm, tiles_per_seq, first_shift_tile, n_main_tiles, n_sub):
    i = pl.program_id(0)
    n_tiles, _, tn = w_ref.shape
    pos = lax.rem(i, tiles_per_seq)
    keep_prev = jnp.where(pos == 0, 0.0, 1.0)
    keep_next = jnp.where(pos == tiles_per_seq - 1, 0.0, 1.0)
    g = g_ref[...]
    h_ref[0:HALO, :] = _rmsnorm(xp_ref[...] * keep_prev, g).astype(BF16)
    h_ref[HALO:HALO + tm, :] = _rmsnorm(x_ref[...], g).astype(BF16)
    h_ref[HALO + tm:HALO + tm + HALO, :] = _rmsnorm(xn_ref[...] * keep_next, g).astype(BF16)

    for t in range(first_shift_tile):
        o_ref[:, t * tn:(t + 1) * tn] = _dot(h_ref[HALO:HALO + tm, :], w_ref[t]).astype(o_ref.dtype)

    sub = tm // n_sub
    jobs = [(t, s) for t in range(first_shift_tile, n_tiles) for s in range(n_sub)]
    mm = lambda t, s: _dot(h_ref[s * sub:(s + 1) * sub + 2 * HALO, :], w_ref[t])
    p_ref[0] = mm(*jobs[0])
    for n_, (t, s) in enumerate(jobs):
        slot = n_ % 2
        if n_ + 1 < len(jobs):
            p_ref[1 - slot] = mm(*jobs[n_ + 1])
        p = p_ref[slot, HALO:HALO + sub, :]
        pp = p_ref[slot, HALO - 1:HALO - 1 + sub, :]
        pn = p_ref[slot, HALO + 1:HALO + 1 + sub, :]
        val = p + mup_ref[t] * (pp - p) + mun_ref[t] * (pn - p)
        rows = slice(s * sub, (s + 1) * sub)
        if t < n_main_tiles:
            o_ref[rows, t * tn:(t + 1) * tn] = val.astype(o_ref.dtype)
        else:
            os_ref[rows, :] = val


def _inproj(x2, g, w_packed, mu_prev, mu_next, seq_len):
    n = x2.shape[0]
    tm = min(INPROJ_ROW_TILE, seq_len)
    tn = 512
    n_tiles = P_COLS // tn
    tiles_per_seq = seq_len // tm
    hb = tm // HALO
    nhb = n // HALO
    n_sub = max(1, tm // 256)
    sub = tm // n_sub
    kern = functools.partial(_inproj_kernel, tm=tm, tiles_per_seq=tiles_per_seq,
                             first_shift_tile=COL_R // tn, n_main_tiles=P_MAIN_COLS // tn,
                             n_sub=n_sub)
    w_t = jnp.moveaxis(w_packed.reshape(D_MODEL, n_tiles, tn), 1, 0)
    mup_t = jnp.moveaxis(mu_prev.reshape(1, n_tiles, tn), 1, 0)
    mun_t = jnp.moveaxis(mu_next.reshape(1, n_tiles, tn), 1, 0)
    return pl.pallas_call(
        kern,
        grid=(n // tm,),
        in_specs=[
            pl.BlockSpec((HALO, D_MODEL), lambda i: (jnp.maximum(i * hb - 1, 0), 0)),
            pl.BlockSpec((tm, D_MODEL), lambda i: (i, 0)),
            pl.BlockSpec((HALO, D_MODEL), lambda i: (jnp.minimum((i + 1) * hb, nhb - 1), 0)),
            pl.BlockSpec((1, D_MODEL), lambda i: (0, 0)),
            pl.BlockSpec((n_tiles, D_MODEL, tn), lambda i: (0, 0, 0)),
            pl.BlockSpec((n_tiles, 1, tn), lambda i: (0, 0, 0)),
            pl.BlockSpec((n_tiles, 1, tn), lambda i: (0, 0, 0)),
        ],
        out_specs=[pl.BlockSpec((tm, P_MAIN_COLS), lambda i: (i, 0)),
                   pl.BlockSpec((tm, P_COLS - P_MAIN_COLS), lambda i: (i, 0))],
        out_shape=[jax.ShapeDtypeStruct((n, P_MAIN_COLS), BF16),
                   jax.ShapeDtypeStruct((n, P_COLS - P_MAIN_COLS), F32)],
        scratch_shapes=[
            pltpu.VMEM((tm + 2 * HALO, D_MODEL), BF16),
            pltpu.VMEM((2, sub + 2 * HALO, tn), F32),
        ],
        compiler_params=pltpu.CompilerParams(
            dimension_semantics=("arbitrary",),
            vmem_limit_bytes=VMEM_LIMIT),
        name="inproj",
    )(x2, x2, x2, g, w_t, mup_t, mun_t)


def _gla_kernel(*refs, reverse, tb, final):
    if final:
        (qk_ref, v_ref, gs_ref, wa2_ref, ba_ref, of_ref, og_ref, ng_ref,
         o_ref, st_ref, b_ref, acc_ref) = refs
    else:
        qk_ref, v_ref, gs_ref, wa2_ref, ba_ref, o_ref, st_ref, b_ref = refs
    j = pl.program_id(1)
    nc = tb // CHUNK

    @pl.when(j == 0)
    def _():
        st_ref[...] = jnp.zeros_like(st_ref)

    logit = _dot_x3(gs_ref[...], wa2_ref[...]) + ba_ref[...]
    la = (jnp.minimum(logit, 0.0) - jnp.log(1.0 + jnp.exp(-jnp.abs(logit)))) * (1.0 / GLA_LOGIT_NORM)
    b_ref[...] = _chunk_cumsum(la, reverse)

    lane_k = _div(_iota2((CHUNK, GLA_QK_W), 1), GLA_DK)
    lane_v = _div(_iota2((CHUNK, GLA_V_W), 1), GLA_DV)
    k_masks = [lane_k == h for h in range(GLA_HEADS)]
    v_masks = [lane_v == h for h in range(GLA_HEADS)]
    t_idx = _iota2((CHUNK, GLA_QK_W), 0)
    s_idx = _mod(_iota2((CHUNK, GLA_QK_W), 1), CHUNK)
    causal = (s_idx >= t_idx) if reverse else (s_idx <= t_idx)
    st_mask = (_div(_iota2((GLA_V_W, GLA_QK_W), 0), GLA_DV)
               == _div(_iota2((GLA_V_W, GLA_QK_W), 1), GLA_DK))
    i_ref = (CHUNK - 1 - CHUNK // 2) if reverse else CHUNK // 2
    i_last = 0 if reverse else CHUNK - 1

    QI, KD, QE, VB, KI_BD, V_BD, DEC = ([] for _ in range(7))
    for c in range(nc):
        rows = slice(c * CHUNK, (c + 1) * CHUNK)
        b = b_ref[rows, :]
        b_mid = b[i_ref:i_ref + 1, :]
        b_last = b[i_last:i_last + 1, :]
        q = qk_ref[rows, 0:GLA_QK_W].astype(F32) * (GLA_DK ** -0.5)
        k = qk_ref[rows, GLA_QK_W:2 * GLA_QK_W].astype(F32)
        v = v_ref[rows, :].astype(F32)
        QI.append((q * jnp.exp(b - b_mid)).astype(BF16))
        KD.append((k * jnp.exp(b_last - b)).astype(BF16))
        QE.append((q * jnp.exp(b)).astype(BF16))
        VB.append(v.astype(BF16))
        KI_BD.append(_blockdiag(k * jnp.exp(b_mid - b), k_masks))
        V_BD.append(_blockdiag(v, v_masks))
        DEC.append(jnp.exp(b_last))
    A = [jnp.where(causal, _dot_nt(qi, kb), 0.0) for qi, kb in zip(QI, KI_BD)]
    O_IN = [_dot(a.astype(BF16), vb) for a, vb in zip(A, V_BD)]
    KV = [jnp.where(st_mask, _dot_tn(vb, kd), 0.0) for vb, kd in zip(VB, KD)]
    for ci in range(nc):
        c = (nc - 1 - ci) if reverse else ci
        rows = slice(c * CHUNK, (c + 1) * CHUNK)
        st = st_ref[...]
        o = O_IN[c] + _dot_nt(QE[c], st.astype(BF16))
        st_ref[...] = st * DEC[c] + KV[c]
        if final:
            acc_ref[rows, :] = of_ref[rows, :] + o
        else:
            o_ref[rows, :] = o

    if final:
        o = acc_ref[...]
        parts = []
        for h in range(GLA_HEADS):
            oh = o[:, h * GLA_DV:(h + 1) * GLA_DV]
            ms = jnp.mean(oh * oh, axis=-1, keepdims=True)
            parts.append(oh * lax.rsqrt(ms + HEAD_NORM_EPS))
        on = jnp.concatenate(parts, axis=1) * ng_ref[...]
        og = og_ref[...].astype(F32)
        o_ref[...] = on * (og * _sigmoid(og))


def _gla(p, ps, wa2p, ba, batch, seq_len, reverse, o_fwd=None, norm_g=None):
    final = o_fwd is not None
    tb = min(GLA_SEQ_TILE, seq_len)
    nt = seq_len // tb
    n = batch * seq_len

    def rowblk(b, j):
        return b * nt + ((nt - 1 - j) if reverse else j)

    in_specs = [
        pl.BlockSpec((tb, 512), lambda b, j: (rowblk(b, j), COL_QK // 512)),
        pl.BlockSpec((tb, 512), lambda b, j: (rowblk(b, j), COL_V // 512)),
        pl.BlockSpec((tb, 256), lambda b, j: (rowblk(b, j), (COL_GS - P_MAIN_COLS) // 256)),
        pl.BlockSpec((256, GLA_QK_W), lambda b, j: (0, 0)),
        pl.BlockSpec((1, GLA_QK_W), lambda b, j: (0, 0)),
    ]
    args = [p, p, ps, wa2p, ba]
    scratch = [pltpu.VMEM((GLA_V_W, GLA_QK_W), F32), pltpu.VMEM((tb, GLA_QK_W), F32)]
    if final:
        in_specs += [
            pl.BlockSpec((tb, GLA_V_W), lambda b, j: (rowblk(b, j), 0)),
            pl.BlockSpec((tb, 512), lambda b, j: (rowblk(b, j), COL_OG // 512)),
            pl.BlockSpec((1, GLA_V_W), lambda b, j: (0, 0)),
        ]
        args += [o_fwd, p, norm_g]
        scratch.append(pltpu.VMEM((tb, GLA_V_W), F32))
    kern = functools.partial(_gla_kernel, reverse=reverse, tb=tb, final=final)
    return pl.pallas_call(
        kern,
        grid=(batch, nt),
        in_specs=in_specs,
        out_specs=pl.BlockSpec((tb, GLA_V_W), lambda b, j: (rowblk(b, j), 0)),
        out_shape=jax.ShapeDtypeStruct((n, GLA_V_W), F32),
        scratch_shapes=scratch,
        compiler_params=pltpu.CompilerParams(
            dimension_semantics=("arbitrary", "arbitrary"),
            vmem_limit_bytes=VMEM_LIMIT),
        name="gla_bwd" if reverse else "gla_fwd",
    )(*args)


def _rwkv_kernel(*refs, reverse, tb, final):
    if final:
        (r_ref, k_ref, v_ref, rs_ref, w0_ref, w2_ref, a0_ref, a2_ref, kk_ref, ka_ref,
         yf_ref, g2_ref, rk_ref, lnw_ref, lnb_ref,
         o_ref, p_ref, cs_s, lw_s, kk_s, ba_s, k2_s, reff_s, oloc_s, gm_s, hm_s, y_s) = refs
    else:
        (r_ref, k_ref, v_ref, rs_ref, w0_ref, w2_ref, a0_ref, a2_ref, kk_ref, ka_ref,
         o_ref, p_ref, cs_s, lw_s, kk_s, ba_s, k2_s, reff_s, oloc_s, gm_s, hm_s) = refs
        y_s = o_ref
    j = pl.program_id(1)
    nc = tb // CHUNK
    G = RWKV_GROUP
    L = G * RWKV_HEAD
    ngrp = RWKV_W // L

    @pl.when(j == 0)
    def _():
        p_ref[...] = jnp.zeros_like(p_ref)

    r = r_ref[...].astype(F32)
    k = k_ref[...].astype(F32)
    lora_in = rs_ref[:, 0:RWKV_DECAY_LORA + RWKV_AAA_LORA]
    w_pre = w0_ref[...] + _dot_x3(jnp.tanh(lora_in), w2_ref[...])
    logw = -jnp.exp(-_softplus(-w_pre) - 0.5)
    a_lr = _sigmoid(a0_ref[...] + _dot(lora_in.astype(BF16), a2_ref[...]))
    hr = _div(_iota2((RWKV_W, RWKV_W), 0), RWKV_HEAD)
    hc = _div(_iota2((RWKV_W, RWKV_W), 1), RWKV_HEAD)
    head_ones = jnp.where(hr == hc, 1.0, 0.0).astype(BF16)
    head_sum = lambda t: _dot(t.astype(BF16), head_ones)
    kk = k * kk_ref[...]
    ssq = head_sum(kk * kk)
    kk = kk / jnp.maximum(jnp.sqrt(ssq), 1e-12)
    k2 = k * (1.0 + (a_lr - 1.0) * ka_ref[...])
    if final:
        bonus = head_sum(r * k2 * rk_ref[...]) * v_ref[...].astype(F32)
        gl = rs_ref[:, RWKV_DECAY_LORA + RWKV_AAA_LORA:]
        g_gate = _dot(_sigmoid(gl).astype(BF16), g2_ref[...])

    cs_s[...] = _chunk_cumsum(logw, reverse)
    lw_s[...] = logw
    kk_s[...] = kk
    ba_s[...] = kk * a_lr
    k2_s[...] = k2
    i_last = 0 if reverse else CHUNK - 1

    lane_h = _div(_iota2((CHUNK, L), 1), RWKV_HEAD)
    h_masks = [lane_h == h for h in range(G)]
    t_idx = _iota2((CHUNK, L), 0)
    s_idx = _mod(_iota2((CHUNK, L), 1), CHUNK)
    strict = (s_idx > t_idx) if reverse else (s_idx < t_idx)
    incl = (s_idx >= t_idx) if reverse else (s_idx <= t_idx)
    eye_p = jnp.where(s_idx == t_idx, 1.0, 0.0)
    bd_mask = _div(_iota2((L, L), 0), RWKV_HEAD) == _div(_iota2((L, L), 1), RWKV_HEAD)
    eye_l = jnp.where(_iota2((L, L), 0) == _iota2((L, L), 1), 1.0, 0.0)

    def blockdiag(z):
        return _blockdiag(z, h_masks)

    chains = [(c, gi) for c in range(nc) for gi in range(ngrp)]
    rows_of = lambda c: slice(c * CHUNK, (c + 1) * CHUNK)
    lanes_of = lambda gi: slice(gi * L, (gi + 1) * L)
    AT, BT, KT, RT, BD, KD, GL, V = ([] for _ in range(8))
    for c in range(nc):
        rows = rows_of(c)
        cs = cs_s[rows, :]
        cs_tot = cs[i_last:i_last + 1, :]
        g_inv = jnp.exp(-cs)
        g_rem = jnp.exp(cs_tot - cs)
        at_c = (-kk_s[rows, :]) * jnp.exp(cs - lw_s[rows, :])
        bt_c = ba_s[rows, :] * g_inv
        kt_c = k2_s[rows, :] * g_inv
        rt_c = r_ref[rows, :].astype(F32) * jnp.exp(cs)
        bd_c = ba_s[rows, :] * g_rem
        kd_c = k2_s[rows, :] * g_rem
        gl_c = jnp.exp(cs_tot)
        for gi in range(ngrp):
            lanes = lanes_of(gi)
            AT.append(at_c[:, lanes])
            BT.append(bt_c[:, lanes])
            KT.append(kt_c[:, lanes])
            RT.append(rt_c[:, lanes])
            BD.append(bd_c[:, lanes])
            KD.append(kd_c[:, lanes])
            GL.append(gl_c[:, lanes])
            V.append(v_ref[rows, lanes].astype(F32))
    bf = lambda t: t.astype(BF16)
    cat0 = lambda a, b: jnp.concatenate([a, b], axis=0)
    cat1 = lambda a, b: jnp.concatenate([a, b], axis=1)
    X = [bf(cat0(a, r_)) for a, r_ in zip(AT, RT)]
    G1 = [_dot_nt(x, blockdiag(b)) for x, b in zip(X, BT)]
    G2 = [_dot_nt(x, blockdiag(k_)) for x, k_ in zip(X, KT)]
    LAB = [jnp.where(strict, g[0:CHUNK], 0.0) for g in G1]
    MRB = [jnp.where(incl, g[CHUNK:2 * CHUNK], 0.0) for g in G1]
    LAK = [jnp.where(strict, g[0:CHUNK], 0.0) for g in G2]
    MRK = [jnp.where(incl, g[CHUNK:2 * CHUNK], 0.0) for g in G2]
    TM = [eye_p + l for l in LAB]
    XP = [_dot(bf(l), blockdiag(l)) for l in LAB]
    for _ in range(4):
        Y = [_dot(bf(cat0(t, x)), blockdiag(x)) for t, x in zip(TM, XP)]
        TM = [t + y[0:CHUNK] for t, y in zip(TM, Y)]
        XP = [y[CHUNK:2 * CHUNK] for y in Y]
    TM = [t + _dot(bf(t), blockdiag(x)) for t, x in zip(TM, XP)]
    VBD = [blockdiag(v) for v in V]
    LMV = [_dot(bf(cat0(l, m)), vb) for l, m, vb in zip(LAK, MRK, VBD)]
    LAKV = [t[0:CHUNK] for t in LMV]
    MV = [t[CHUNK:2 * CHUNK] for t in LMV]
    WU =[_dot(bf(t), cat1(blockdiag(a), blockdiag(lv))) for t, a, lv in zip(TM, AT, LAKV)]
    WA = [wu[:, 0:L] for wu in WU]
    UV = [wu[:, L:2 * L] for wu in WU]
    RO = [_dot(bf(m), cat1(blockdiag(wa), blockdiag(uv))) for m, wa, uv in zip(MRB, WA, UV)]
    PW = 2 * RWKV_HEAD
    zpair = jnp.zeros((PW, PW), F32)

    def pair_tn(bd, wa, uv):
        blocks = []
        for q in range(L // PW):
            pl_ = slice(q * PW, (q + 1) * PW)
            blocks.append(_dot_tn(bf(bd[:, pl_]), bf(cat1(wa[:, pl_], uv[:, pl_]))))
        rows_g = [cat1(*[blocks[q][:, 0:PW] if q2 == q else zpair for q2 in range(L // PW)])
                  for q in range(L // PW)]
        rows_h = [cat1(*[blocks[q][:, PW:2 * PW] if q2 == q else zpair for q2 in range(L // PW)])
                  for q in range(L // PW)]
        return cat1(jnp.concatenate(rows_g, axis=0), jnp.concatenate(rows_h, axis=0))

    GH = [pair_tn(bd, wa, uv) for bd, wa, uv in zip(BD, WA, UV)]
    KV = [_dot_tn(bf(kd), bf(v)) for kd, v in zip(KD, V)]
    for n_, (c, gi) in enumerate(chains):
        rows, lanes = rows_of(c), lanes_of(gi)
        reff_s[rows, lanes] = RT[n_] + RO[n_][:, 0:L]
        oloc_s[rows, lanes] = RO[n_][:, L:2 * L] + MV[n_]
        gm_s[c, gi] = bf(eye_l * GL[n_] + jnp.where(bd_mask, GH[n_][:, 0:L], 0.0))
        hm_s[c, gi] = jnp.where(bd_mask, GH[n_][:, L:2 * L] + KV[n_], 0.0)

    for ci in range(nc):
        c = (nc - 1 - ci) if reverse else ci
        rows = slice(c * CHUNK, (c + 1) * CHUNK)
        for gi in range(ngrp):
            lanes = slice(gi * L, (gi + 1) * L)
            pst = p_ref[gi].astype(BF16)
            lhs = jnp.concatenate([reff_s[rows, lanes].astype(BF16), gm_s[c, gi]], axis=0)
            yp = _dot(lhs, pst)
            y_s[rows, lanes] = yp[0:CHUNK] + oloc_s[rows, lanes]
            p_ref[gi] = yp[CHUNK:CHUNK + L] + hm_s[c, gi]

    if final:
        y = y_s[...] + yf_ref[...]
        inv_n = 1.0 / RWKV_HEAD
        mu = head_sum(y) * inv_n
        yc = y - mu
        var = head_sum(yc * yc) * inv_n
        yn = yc * lax.rsqrt(var + RWKV_GN_EPS) * lnw_ref[...] + lnb_ref[...]
        o_ref[...] = (yn + bonus) * g_gate


def _rwkv(p, ps, w0, w2p, a0, a2p, k_k, k_a, batch, seq_len, reverse,
          y_fwd=None, g2=None, r_k=None, ln_w=None, ln_b=None):
    final = y_fwd is not None
    tb = min(RWKV_SEQ_TILE, seq_len)
    nt = seq_len // tb
    n = batch * seq_len
    L = RWKV_GROUP * RWKV_HEAD
    ngrp = RWKV_W // L
    lora = RWKV_DECAY_LORA + RWKV_AAA_LORA

    def rowblk(b, j):
        return b * nt + ((nt - 1 - j) if reverse else j)

    vec = pl.BlockSpec((1, RWKV_W), lambda b, j: (0, 0))
    in_specs = [
        pl.BlockSpec((tb, 512), lambda b, j: (rowblk(b, j), COL_R // 512)),
        pl.BlockSpec((tb, 512), lambda b, j: (rowblk(b, j), COL_RK // 512)),
        pl.BlockSpec((tb, 512), lambda b, j: (rowblk(b, j), COL_RV // 512)),
        pl.BlockSpec((tb, 256), lambda b, j: (rowblk(b, j), (COL_RS - P_MAIN_COLS) // 256)),
        vec,
        pl.BlockSpec((lora, RWKV_W), lambda b, j: (0, 0)),
        vec,
        pl.BlockSpec((lora, RWKV_W), lambda b, j: (0, 0)),
        vec, vec,
    ]
    args = [p, p, p, ps, w0, w2p, a0, a2p, k_k, k_a]
    nc = tb // CHUNK
    scratch = ([pltpu.VMEM((ngrp, L, L), F32)] + [pltpu.VMEM((tb, RWKV_W), F32)] * 7
               + [pltpu.VMEM((nc, ngrp, L, L), BF16), pltpu.VMEM((nc, ngrp, L, L), F32)])
    if final:
        in_specs += [
            pl.BlockSpec((tb, RWKV_W), lambda b, j: (rowblk(b, j), 0)),
            pl.BlockSpec((RWKV_GATE_LORA, RWKV_W), lambda b, j: (0, 0)),
            vec, vec, vec,
        ]
        args += [y_fwd, g2, r_k, ln_w, ln_b]
        scratch.append(pltpu.VMEM((tb, RWKV_W), F32))
    kern = functools.partial(_rwkv_kernel, reverse=reverse, tb=tb, final=final)
    return pl.pallas_call(
        kern,
        grid=(batch, nt),
        in_specs=in_specs,
        out_specs=pl.BlockSpec((tb, RWKV_W), lambda b, j: (rowblk(b, j), 0)),
        out_shape=jax.ShapeDtypeStruct((n, RWKV_W), F32),
        scratch_shapes=scratch,
        compiler_params=pltpu.CompilerParams(
            dimension_semantics=("arbitrary", "arbitrary"),
            vmem_limit_bytes=VMEM_LIMIT),
        name="rwkv_bwd" if reverse else "rwkv_fwd",
    )(*args)


def _merge_kernel(x_ref, oa_ref, ob_ref, ga_ref, gb_ref, wa_ref, wb_ref, wo_ref, o_ref):
    ya = _dot(oa_ref[...].astype(BF16), wa_ref[...])
    yb = _dot(ob_ref[...].astype(BF16), wb_ref[...])
    merged = (_sigmoid(ga_ref[...].astype(F32)) * ya
              + _sigmoid(gb_ref[...].astype(F32)) * yb)
    o_ref[...] = x_ref[...] + _dot(merged.astype(BF16), wo_ref[...])


def _merge(x2, oa, ob, p, gla_proj, rwkv_proj, w_out, seq_len):
    n = x2.shape[0]
    tm = min(ROW_TILE, seq_len)
    return pl.pallas_call(
        _merge_kernel,
        grid=(n // tm,),
        in_specs=[
            pl.BlockSpec((tm, D_MODEL), lambda i: (i, 0)),
            pl.BlockSpec((tm, GLA_V_W), lambda i: (i, 0)),
            pl.BlockSpec((tm, RWKV_W), lambda i: (i, 0)),
            pl.BlockSpec((tm, D_MODEL), lambda i: (i, COL_GA // D_MODEL)),
            pl.BlockSpec((tm, D_MODEL), lambda i: (i, COL_GB // D_MODEL)),
            pl.BlockSpec((GLA_V_W, D_MODEL), lambda i: (0, 0)),
            pl.BlockSpec((RWKV_W, D_MODEL), lambda i: (0, 0)),
            pl.BlockSpec((D_MODEL, D_MODEL), lambda i: (0, 0)),
        ],
        out_specs=pl.BlockSpec((tm, D_MODEL), lambda i: (i, 0)),
        out_shape=jax.ShapeDtypeStruct((n, D_MODEL), F32),
        compiler_params=pltpu.CompilerParams(
            dimension_semantics=("arbitrary",),
            vmem_limit_bytes=VMEM_LIMIT),
        name="merge",
    )(x2, oa, ob, p, p, gla_proj, rwkv_proj, w_out)


def _ffn_kernel(xp_ref, x_ref, xn_ref, g2_ref, wu_ref, cw_ref, cb_ref, wd_ref, gf_ref, o_ref,
                h_ref, u_ref, *, tm, tiles_per_seq):
    i = pl.program_id(0)
    pos = lax.rem(i, tiles_per_seq)
    keep_prev = jnp.where(pos == 0, 0.0, 1.0)
    keep_next = jnp.where(pos == tiles_per_seq - 1, 0.0, 1.0)
    g2 = g2_ref[...]
    h_ref[0:HALO, :] = _rmsnorm(xp_ref[...] * keep_prev, g2).astype(BF16)
    h_ref[HALO:HALO + tm, :] = _rmsnorm(x_ref[...], g2).astype(BF16)
    h_ref[HALO + tm:HALO + tm + HALO, :] = _rmsnorm(xn_ref[...] * keep_next, g2).astype(BF16)

    up = lambda c: _dot(h_ref[...], wu_ref[c])
    u_ref[0] = up(0)
    for c in range(FF_NCHUNK):
        slot = c % 2
        if c + 1 < FF_NCHUNK:
            u_ref[1 - slot] = up(c + 1)
        cw = cw_ref[c]
        uc = (cw[0:1, :] * u_ref[slot, HALO - 1:HALO - 1 + tm, :]
              + cw[1:2, :] * u_ref[slot, HALO:HALO + tm, :]
              + cw[2:3, :] * u_ref[slot, HALO + 1:HALO + 1 + tm, :]
              + cb_ref[c])
        ug = uc[:, 0:FF_CHUNK]
        act = (ug * _sigmoid(ug)) * uc[:, FF_CHUNK:2 * FF_CHUNK]
        down = _dot(act.astype(BF16), wd_ref[c])
        acc = down if c == 0 else acc + down
    o_ref[...] = _rmsnorm(x_ref[...] + acc, gf_ref[...])


def _ffn(x1, g2, wu_p, cw_p, cb_p, wd_p, gf, seq_len):
    n = x1.shape[0]
    tm = min(FFN_ROW_TILE, seq_len)
    tiles_per_seq = seq_len // tm
    hb = tm // HALO
    nhb = n // HALO
    kern = functools.partial(_ffn_kernel, tm=tm, tiles_per_seq=tiles_per_seq)
    return pl.pallas_call(
        kern,
        grid=(n // tm,),
        in_specs=[
            pl.BlockSpec((HALO, D_MODEL), lambda i: (jnp.maximum(i * hb - 1, 0), 0)),
            pl.BlockSpec((tm, D_MODEL), lambda i: (i, 0)),
            pl.BlockSpec((HALO, D_MODEL), lambda i: (jnp.minimum((i + 1) * hb, nhb - 1), 0)),
            pl.BlockSpec((1, D_MODEL), lambda i: (0, 0)),
            pl.BlockSpec((FF_NCHUNK, D_MODEL, 2 * FF_CHUNK), lambda i: (0, 0, 0)),
            pl.BlockSpec((FF_NCHUNK, 3, 2 * FF_CHUNK), lambda i: (0, 0, 0)),
            pl.BlockSpec((FF_NCHUNK, 1, 2 * FF_CHUNK), lambda i: (0, 0, 0)),
            pl.BlockSpec((FF_NCHUNK, FF_CHUNK, D_MODEL), lambda i: (0, 0, 0)),
            pl.BlockSpec((1, D_MODEL), lambda i: (0, 0)),
        ],
        out_specs=pl.BlockSpec((tm, D_MODEL), lambda i: (i, 0)),
        out_shape=jax.ShapeDtypeStruct((n, D_MODEL), F32),
        scratch_shapes=[
            pltpu.VMEM((tm + 2 * HALO, D_MODEL), BF16),
            pltpu.VMEM((2, tm + 2 * HALO, 2 * FF_CHUNK), F32),
        ],
        compiler_params=pltpu.CompilerParams(
            dimension_semantics=("arbitrary",),
            vmem_limit_bytes=VMEM_LIMIT),
        name="ffn",
    )(x1, x1, x1, g2, wu_p, cw_p, cb_p, wd_p, gf)


def _pack_w_in(w):
    gla_w = 2 * GLA_QK_W + 2 * GLA_V_W + 2 * GLA_GATE_RANK
    r0 = gla_w
    g0 = r0 + 3 * RWKV_W + RWKV_DECAY_LORA + RWKV_AAA_LORA + RWKV_GATE_LORA
    pad = jnp.zeros((w.shape[0], 256 - 2 * GLA_GATE_RANK), w.dtype)
    return jnp.concatenate([
        w[:, g0:g0 + 2 * D_MODEL],
        w[:, 0:2 * GLA_QK_W + 2 * GLA_V_W],
        w[:, r0:r0 + 3 * RWKV_W + 256],
        w[:, 2 * GLA_QK_W + 2 * GLA_V_W:gla_w],
        pad], axis=1)


def _pack_mu(mu):
    z = jnp.zeros((COL_R,), mu.dtype)
    z2 = jnp.zeros((P_COLS - COL_GS,), mu.dtype)
    return jnp.concatenate([z, mu, z2])[None, :]


def _pack_ff_cols(t):
    pad = [(0, 0)] * (t.ndim - 1) + [(0, FF_PAD - D_FF)]
    g = jnp.pad(t[..., :D_FF], pad)
    v = jnp.pad(t[..., D_FF:], pad)
    lead = t.shape[:-1]
    g = g.reshape(lead + (FF_NCHUNK, FF_CHUNK))
    v = v.reshape(lead + (FF_NCHUNK, FF_CHUNK))
    gv = jnp.concatenate([g, v], axis=-1)
    return jnp.moveaxis(gv, -2, 0)


def kernel(x, norm1_g, w_in, gla_wa2_f, gla_ba_f, gla_wa2_b, gla_ba_b, gla_norm_g, gla_proj, rwkv_mu_prev, rwkv_mu_next, rwkv_w0_f, rwkv_w2_f, rwkv_w0_b, rwkv_w2_b, rwkv_a0, rwkv_a2, rwkv_g2, rwkv_k_k, rwkv_k_a, rwkv_r_k, rwkv_ln_w, rwkv_ln_b, rwkv_proj, w_out, norm2_g, ffn_up, ffn_conv_w, ffn_conv_b, ffn_down, norm_f_g):
    batch, seq_len, d = x.shape
    assert w_in.shape[0] == 1 and d == D_MODEL and seq_len % CHUNK == 0
    x2 = x.reshape(batch * seq_len, d)
    row = lambda t: t.reshape(1, -1)
    for l in range(1):
        w_packed = _pack_w_in(w_in[l]).astype(BF16)
        wa2_f = jnp.pad(gla_wa2_f[l], ((0, 256 - GLA_GATE_RANK), (0, 0)))
        wa2_b = jnp.pad(gla_wa2_b[l], ((GLA_GATE_RANK, 256 - 2 * GLA_GATE_RANK), (0, 0)))
        zl = jnp.zeros((RWKV_AAA_LORA, RWKV_W), F32)
        w2_f = jnp.concatenate([rwkv_w2_f[l], zl], axis=0)
        w2_b = jnp.concatenate([rwkv_w2_b[l], zl], axis=0)
        a2p = jnp.concatenate([jnp.zeros((RWKV_DECAY_LORA, RWKV_W), F32), rwkv_a2[l]], axis=0).astype(BF16)

        p, ps = _inproj(x2, row(norm1_g[l]), w_packed, _pack_mu(rwkv_mu_prev[l]),
                        _pack_mu(rwkv_mu_next[l]), seq_len)

        o_f = _gla(p, ps, wa2_f, row(gla_ba_f[l]), batch, seq_len, reverse=False)
        oa = _gla(p, ps, wa2_b, row(gla_ba_b[l]), batch, seq_len, reverse=True,
                  o_fwd=o_f, norm_g=row(gla_norm_g[l]))

        rw_common = (row(rwkv_a0[l]), a2p, row(rwkv_k_k[l]), row(rwkv_k_a[l]))
        y_f = _rwkv(p, ps, row(rwkv_w0_f[l]), w2_f, *rw_common, batch, seq_len, reverse=False)
        ob = _rwkv(p, ps, row(rwkv_w0_b[l]), w2_b, *rw_common, batch, seq_len, reverse=True,
                   y_fwd=y_f, g2=rwkv_g2[l].astype(BF16), r_k=row(rwkv_r_k[l]),
                   ln_w=row(rwkv_ln_w[l]), ln_b=row(rwkv_ln_b[l]))

        x1 = _merge(x2, oa, ob, p, gla_proj[l].astype(BF16), rwkv_proj[l].astype(BF16),
                    w_out[l].astype(BF16), seq_len)

        wu_p = _pack_ff_cols(ffn_up[l]).astype(BF16)
        cw_p = _pack_ff_cols(ffn_conv_w[l])
        cb_p = _pack_ff_cols(ffn_conv_b[l][None, :])
        wd_p = jnp.pad(ffn_down[l], ((0, FF_PAD - D_FF), (0, 0))).reshape(
            FF_NCHUNK, FF_CHUNK, D_MODEL).astype(BF16)
        x2 = _ffn(x1, row(norm2_g[l]), wu_p, cw_p, cb_p, wd_p, row(norm_f_g), seq_len)
    return x2.reshape(batch, seq_len, d)
```

```python
import functools

import jax
import jax.numpy as jnp
from jax import lax
from jax.experimental import pallas as pl
from jax.experimental.pallas import tpu as pltpu

F32 = jnp.float32
BF16 = jnp.bfloat16

D_MODEL = 1024
GLA_HEADS = 4
GLA_DK = 64
GLA_DV = 128
GLA_QK_W = GLA_HEADS * GLA_DK
GLA_V_W = GLA_HEADS * GLA_DV
GLA_GATE_RANK = 16
GLA_LOGIT_NORM = 16.0
CHUNK = 64
RWKV_HEAD = 64
RWKV_W = 512
RWKV_HEADS = RWKV_W // RWKV_HEAD
RWKV_DECAY_LORA = 64
RWKV_AAA_LORA = 64
RWKV_GATE_LORA = 128
RWKV_GN_EPS = RWKV_HEAD * 1e-5
D_FF = 2752
NORM_EPS = 1e-6
HEAD_NORM_EPS = 1e-5

COL_GA, COL_GB = 0, 1024
COL_QK, COL_V, COL_OG = 2048, 2560, 3072
COL_R, COL_RK, COL_RV = 3584, 4096, 4608
COL_RS, COL_GS = 5120, 5376
P_COLS = 5632
P_MAIN_COLS = COL_RS
HALO = 16

FF_CHUNK = 256
FF_PAD = 2816
FF_NCHUNK = FF_PAD // FF_CHUNK

VMEM_LIMIT = 56 * 1024 * 1024
ROW_TILE = 1024
INPROJ_ROW_TILE = 512
FFN_ROW_TILE = 512
GLA_SEQ_TILE = 1024
RWKV_SEQ_TILE = 256
RWKV_GROUP = 4


def _sigmoid(x):
    return 1.0 / (1.0 + jnp.exp(-x))


def _softplus(z):
    return jnp.maximum(z, 0.0) + jnp.log(1.0 + jnp.exp(-jnp.abs(z)))


def _split3(x):
    hi = x.astype(BF16)
    r1 = x - hi.astype(F32)
    mid = r1.astype(BF16)
    lo = (r1 - mid.astype(F32)).astype(BF16)
    return hi, mid, lo


def _dot(a, b):
    return jnp.dot(a, b, preferred_element_type=F32)


def _dot_nt(a, b):
    return lax.dot_general(a, b, (((1,), (1,)), ((), ())), preferred_element_type=F32)


def _dot_tn(a, b):
    return lax.dot_general(a, b, (((0,), (0,)), ((), ())), preferred_element_type=F32)


def _dot_exact_lhs(sel, x):
    hi = x.astype(BF16)
    lo = (x - hi.astype(F32)).astype(BF16)
    s = sel.astype(BF16)
    return _dot(s, hi) + _dot(s, lo)


def _dot_exact_rhs(x, sel):
    hi = x.astype(BF16)
    lo = (x - hi.astype(F32)).astype(BF16)
    s = sel.astype(BF16)
    return _dot(hi, s) + _dot(lo, s)


def _dot_x3(a, b):
    ah = a.astype(BF16)
    al = (a - ah.astype(F32)).astype(BF16)
    bh = b.astype(BF16)
    bl = (b - bh.astype(F32)).astype(BF16)
    return _dot(ah, bh) + _dot(al, bh) + _dot(ah, bl)


def _rmsnorm(x, g):
    ms = jnp.mean(x * x, axis=-1, keepdims=True)
    return x * lax.rsqrt(ms + NORM_EPS) * g


def _iota2(shape, dim):
    return lax.broadcasted_iota(jnp.int32, shape, dim)


def _div(x, size):
    return lax.shift_right_logical(x, size.bit_length() - 1)


def _mod(x, size):
    return lax.bitwise_and(x, size - 1)


def _chunk_cumsum(x, reverse):
    rows = x.shape[0]
    blk = min(rows, 256)
    row = _iota2((blk, blk), 0)
    col = _iota2((blk, blk), 1)
    same = _div(row, CHUNK) == _div(col, CHUNK)
    tri = jnp.where(same & ((col >= row) if reverse else (col <= row)), 1.0, 0.0)
    parts = [_dot_exact_lhs(tri, x[i:i + blk]) for i in range(0, rows, blk)]
    return parts[0] if len(parts) == 1 else jnp.concatenate(parts, axis=0)


def _blockdiag(z, head_masks):
    return jnp.concatenate([jnp.where(m, z, 0.0) for m in head_masks], axis=0).astype(BF16)


def _inproj_kernel(xp_ref, x_ref, xn_ref, g_ref, w_ref, mup_ref, mun_ref, o_ref, os_ref,
                   h_ref, p_ref, *, tm, tiles_per_seq, first_shift_tile, n_main_tiles, n_sub):
    i = pl.program_id(0)
    n_tiles, _, tn = w_ref.shape
    pos = lax.rem(i, tiles_per_seq)
    keep_prev = jnp.where(pos == 0, 0.0, 1.0)
    keep_next = jnp.where(pos == tiles_per_seq - 1, 0.0, 1.0)
    g = g_ref[...]
    h_ref[0:HALO, :] = _rmsnorm(xp_ref[...] * keep_prev, g).astype(BF16)
    h_ref[HALO:HALO + tm, :] = _rmsnorm(x_ref[...], g).astype(BF16)
    h_ref[HALO + tm:HALO + tm + HALO, :] = _rmsnorm(xn_ref[...] * keep_next, g).astype(BF16)

    for t in range(first_shift_tile):
        o_ref[:, t * tn:(t + 1) * tn] = _dot(h_ref[HALO:HALO + tm, :], w_ref[t]).astype(o_ref.dtype)

    sub = tm // n_sub
    jobs = [(t, s) for t in range(first_shift_tile, n_tiles) for s in range(n_sub)]
    mm = lambda t, s: _dot(h_ref[s * sub:(s + 1) * sub + 2 * HALO, :], w_ref[t])
    p_ref[0] = mm(*jobs[0])
    for n_, (t, s) in enumerate(jobs):
        slot = n_ % 2
        if n_ + 1 < len(jobs):
            p_ref[1 - slot] = mm(*jobs[n_ + 1])
        p = p_ref[slot, HALO:HALO + sub, :]
        pp = p_ref[slot, HALO - 1:HALO - 1 + sub, :]
        pn = p_ref[slot, HALO + 1:HALO + 1 + sub, :]
        val = p + mup_ref[t] * (pp - p) + mun_ref[t] * (pn - p)
        rows = slice(s * sub, (s + 1) * sub)
        if t < n_main_tiles:
            o_ref[rows, t * tn:(t + 1) * tn] = val.astype(o_ref.dtype)
        else:
            os_ref[rows, :] = val


def _inproj(x2, g, w_packed, mu_prev, mu_next, seq_len):
    n = x2.shape[0]
    tm = min(INPROJ_ROW_TILE, seq_len)
    tn = 512
    n_tiles = P_COLS // tn
    tiles_per_seq = seq_len // tm
    hb = tm // HALO
    nhb = n // HALO
    n_sub = max(1, tm // 256)
    sub = tm // n_sub
    kern = functools.partial(_inproj_kernel, tm=tm, tiles_per_seq=tiles_per_seq,
                             first_shift_tile=COL_R // tn, n_main_tiles=P_MAIN_COLS // tn,
                             n_sub=n_sub)
    w_t = jnp.moveaxis(w_packed.reshape(D_MODEL, n_tiles, tn), 1, 0)
    mup_t = jnp.moveaxis(mu_prev.reshape(1, n_tiles, tn), 1, 0)
    mun_t = jnp.moveaxis(mu_next.reshape(1, n_tiles, tn), 1, 0)
    return pl.pallas_call(
        kern,
        grid=(n // tm,),
        in_specs=[
            pl.BlockSpec((HALO, D_MODEL), lambda i: (jnp.maximum(i * hb - 1, 0), 0)),
            pl.BlockSpec((tm, D_MODEL), lambda i: (i, 0)),
            pl.BlockSpec((HALO, D_MODEL), lambda i: (jnp.minimum((i + 1) * hb, nhb - 1), 0)),
            pl.BlockSpec((1, D_MODEL), lambda i: (0, 0)),
            pl.BlockSpec((n_tiles, D_MODEL, tn), lambda i: (0, 0, 0)),
            pl.BlockSpec((n_tiles, 1, tn), lambda i: (0, 0, 0)),
            pl.BlockSpec((n_tiles, 1, tn), lambda i: (0, 0, 0)),
        ],
        out_specs=[pl.BlockSpec((tm, P_MAIN_COLS), lambda i: (i, 0)),
                   pl.BlockSpec((tm, P_COLS - P_MAIN_COLS), lambda i: (i, 0))],
        out_shape=[jax.ShapeDtypeStruct((n, P_MAIN_COLS), BF16),
                   jax.ShapeDtypeStruct((n, P_COLS - P_MAIN_COLS), F32)],
        scratch_shapes=[
            pltpu.VMEM((tm + 2 * HALO, D_MODEL), BF16),
            pltpu.VMEM((2, sub + 2 * HALO, tn), F32),
        ],
        compiler_params=pltpu.CompilerParams(
            dimension_semantics=("arbitrary",),
            vmem_limit_bytes=VMEM_LIMIT),
        name="inproj",
    )(x2, x2, x2, g, w_t, mup_t, mun_t)


def _gla_kernel(*refs, reverse, tb, final):
    if final:
        (qk_ref, v_ref, gs_ref, wa2_ref, ba_ref, of_ref, og_ref, ng_ref,
         o_ref, st_ref, b_ref, acc_ref) = refs
    else:
        qk_ref, v_ref, gs_ref, wa2_ref, ba_ref, o_ref, st_ref, b_ref = refs
    j = pl.program_id(1)
    nc = tb // CHUNK

    @pl.when(j == 0)
    def _():
        st_ref[...] = jnp.zeros_like(st_ref)

    logit = _dot_x3(gs_ref[...], wa2_ref[...]) + ba_ref[...]
    la = (jnp.minimum(logit, 0.0) - jnp.log(1.0 + jnp.exp(-jnp.abs(logit)))) * (1.0 / GLA_LOGIT_NORM)
    b_ref[...] = _chunk_cumsum(la, reverse)

    lane_k = _div(_iota2((CHUNK, GLA_QK_W), 1), GLA_DK)
    lane_v = _div(_iota2((CHUNK, GLA_V_W), 1), GLA_DV)
    k_masks = [lane_k == h for h in range(GLA_HEADS)]
    v_masks = [lane_v == h for h in range(GLA_HEADS)]
    t_idx = _iota2((CHUNK, GLA_QK_W), 0)
    s_idx = _mod(_iota2((CHUNK, GLA_QK_W), 1), CHUNK)
    causal = (s_idx >= t_idx) if reverse else (s_idx <= t_idx)
    st_mask = (_div(_iota2((GLA_V_W, GLA_QK_W), 0), GLA_DV)
               == _div(_iota2((GLA_V_W, GLA_QK_W), 1), GLA_DK))
    i_ref = (CHUNK - 1 - CHUNK // 2) if reverse else CHUNK // 2
    i_last = 0 if reverse else CHUNK - 1

    QI, KD, QE, VB, KI_BD, V_BD, DEC = ([] for _ in range(7))
    for c in range(nc):
        rows = slice(c * CHUNK, (c + 1) * CHUNK)
        b = b_ref[rows, :]
        b_mid = b[i_ref:i_ref + 1, :]
        b_last = b[i_last:i_last + 1, :]
        q = qk_ref[rows, 0:GLA_QK_W].astype(F32) * (GLA_DK ** -0.5)
        k = qk_ref[rows, GLA_QK_W:2 * GLA_QK_W].astype(F32)
        v = v_ref[rows, :].astype(F32)
        QI.append((q * jnp.exp(b - b_mid)).astype(BF16))
        KD.append((k * jnp.exp(b_last - b)).astype(BF16))
        QE.append((q * jnp.exp(b)).astype(BF16))
        VB.append(v.astype(BF16))
        KI_BD.append(_blockdiag(k * jnp.exp(b_mid - b), k_masks))
        V_BD.append(_blockdiag(v, v_masks))
        DEC.append(jnp.exp(b_last))
    A = [jnp.where(causal, _dot_nt(qi, kb), 0.0) for qi, kb in zip(QI, KI_BD)]
    O_IN = [_dot(a.astype(BF16), vb) for a, vb in zip(A, V_BD)]
    KV = [jnp.where(st_mask, _dot_tn(vb, kd), 0.0) for vb, kd in zip(VB, KD)]
    for ci in range(nc):
        c = (nc - 1 - ci) if reverse else ci
        rows = slice(c * CHUNK, (c + 1) * CHUNK)
        st = st_ref[...]
        o = O_IN[c] + _dot_nt(QE[c], st.astype(BF16))
        st_ref[...] = st * DEC[c] + KV[c]
        if final:
            acc_ref[rows, :] = of_ref[rows, :] + o
        else:
            o_ref[rows, :] = o

    if final:
        o = acc_ref[...]
        parts = []
        for h in range(GLA_HEADS):
            oh = o[:, h * GLA_DV:(h + 1) * GLA_DV]
            ms = jnp.mean(oh * oh, axis=-1, keepdims=True)
            parts.append(oh * lax.rsqrt(ms + HEAD_NORM_EPS))
        on = jnp.concatenate(parts, axis=1) * ng_ref[...]
        og = og_ref[...].astype(F32)
        o_ref[...] = on * (og * _sigmoid(og))


def _gla(p, ps, wa2p, ba, batch, seq_len, reverse, o_fwd=None, norm_g=None):
    final = o_fwd is not None
    tb = min(GLA_SEQ_TILE, seq_len)
    nt = seq_len // tb
    n = batch * seq_len

    def rowblk(b, j):
        return b * nt + ((nt - 1 - j) if reverse else j)

    in_specs = [
        pl.BlockSpec((tb, 512), lambda b, j: (rowblk(b, j), COL_QK // 512)),
        pl.BlockSpec((tb, 512), lambda b, j: (rowblk(b, j), COL_V // 512)),
        pl.BlockSpec((tb, 256), lambda b, j: (rowblk(b, j), (COL_GS - P_MAIN_COLS) // 256)),
        pl.BlockSpec((256, GLA_QK_W), lambda b, j: (0, 0)),
        pl.BlockSpec((1, GLA_QK_W), lambda b, j: (0, 0)),
    ]
    args = [p, p, ps, wa2p, ba]
    scratch = [pltpu.VMEM((GLA_V_W, GLA_QK_W), F32), pltpu.VMEM((tb, GLA_QK_W), F32)]
    if final:
        in_specs += [
            pl.BlockSpec((tb, GLA_V_W), lambda b, j: (rowblk(b, j), 0)),
            pl.BlockSpec((tb, 512), lambda b, j: (rowblk(b, j), COL_OG // 512)),
            pl.BlockSpec((1, GLA_V_W), lambda b, j: (0, 0)),
        ]
        args += [o_fwd, p, norm_g]
        scratch.append(pltpu.VMEM((tb, GLA_V_W), F32))
    kern = functools.partial(_gla_kernel, reverse=reverse, tb=tb, final=final)
    return pl.pallas_call(
        kern,
        grid=(batch, nt),
        in_specs=in_specs,
        out_specs=pl.BlockSpec((tb, GLA_V_W), lambda b, j: (rowblk(b, j), 0)),
        out_shape=jax.ShapeDtypeStruct((n, GLA_V_W), F32),
        scratch_shapes=scratch,
        compiler_params=pltpu.CompilerParams(
            dimension_semantics=("arbitrary", "arbitrary"),
            vmem_limit_bytes=VMEM_LIMIT),
        name="gla_bwd" if reverse else "gla_fwd",
    )(*args)


def _rwkv_kernel(*refs, reverse, tb, final):
    if final:
        (r_ref, k_ref, v_ref, rs_ref, w0_ref, w2_ref, a0_ref, a2_ref, kk_ref, ka_ref,
         yf_ref, g2_ref, rk_ref, lnw_ref, lnb_ref,
         o_ref, p_ref, cs_s, lw_s, kk_s, ba_s, k2_s, reff_s, oloc_s, gm_s, hm_s, y_s) = refs
    else:
        (r_ref, k_ref, v_ref, rs_ref, w0_ref, w2_ref, a0_ref, a2_ref, kk_ref, ka_ref,
         o_ref, p_ref, cs_s, lw_s, kk_s, ba_s, k2_s, reff_s, oloc_s, gm_s, hm_s) = refs
        y_s = o_ref
    j = pl.program_id(1)
    nc = tb // CHUNK
    G = RWKV_GROUP
    L = G * RWKV_HEAD
    ngrp = RWKV_W // L

    @pl.when(j == 0)
    def _():
        p_ref[...] = jnp.zeros_like(p_ref)

    r = r_ref[...].astype(F32)
    k = k_ref[...].astype(F32)
    lora_in = rs_ref[:, 0:RWKV_DECAY_LORA + RWKV_AAA_LORA]
    w_pre = w0_ref[...] + _dot_x3(jnp.tanh(lora_in), w2_ref[...])
    logw = -jnp.exp(-_softplus(-w_pre) - 0.5)
    a_lr = _sigmoid(a0_ref[...] + _dot(lora_in.astype(BF16), a2_ref[...]))
    hr = _div(_iota2((RWKV_W, RWKV_W), 0), RWKV_HEAD)
    hc = _div(_iota2((RWKV_W, RWKV_W), 1), RWKV_HEAD)
    head_ones = jnp.where(hr == hc, 1.0, 0.0).astype(BF16)
    head_sum = lambda t: _dot(t.astype(BF16), head_ones)
    kk = k * kk_ref[...]
    ssq = head_sum(kk * kk)
    kk = kk / jnp.maximum(jnp.sqrt(ssq), 1e-12)
    k2 = k * (1.0 + (a_lr - 1.0) * ka_ref[...])
    if final:
        bonus = head_sum(r * k2 * rk_ref[...]) * v_ref[...].astype(F32)
        gl = rs_ref[:, RWKV_DECAY_LORA + RWKV_AAA_LORA:]
        g_gate = _dot(_sigmoid(gl).astype(BF16), g2_ref[...])

    cs_s[...] = _chunk_cumsum(logw, reverse)
    lw_s[...] = logw
    kk_s[...] = kk
    ba_s[...] = kk * a_lr
    k2_s[...] = k2
    i_last = 0 if reverse else CHUNK - 1

    lane_h = _div(_iota2((CHUNK, L), 1), RWKV_HEAD)
    h_masks = [lane_h == h for h in range(G)]
    t_idx = _iota2((CHUNK, L), 0)
    s_idx = _mod(_iota2((CHUNK, L), 1), CHUNK)
    strict = (s_idx > t_idx) if reverse else (s_idx < t_idx)
    incl = (s_idx >= t_idx) if reverse else (s_idx <= t_idx)
    eye_p = jnp.where(s_idx == t_idx, 1.0, 0.0)
    bd_mask = _div(_iota2((L, L), 0), RWKV_HEAD) == _div(_iota2((L, L), 1), RWKV_HEAD)
    eye_l = jnp.where(_iota2((L, L), 0) == _iota2((L, L), 1), 1.0, 0.0)

    def blockdiag(z):
        return _blockdiag(z, h_masks)

    chains = [(c, gi) for c in range(nc) for gi in range(ngrp)]
    rows_of = lambda c: slice(c * CHUNK, (c + 1) * CHUNK)
    lanes_of = lambda gi: slice(gi * L, (gi + 1) * L)
    AT, BT, KT, RT, BD, KD, GL, V = ([] for _ in range(8))
    for c in range(nc):
        rows = rows_of(c)
        cs = cs_s[rows, :]
        cs_tot = cs[i_last:i_last + 1, :]
        g_inv = jnp.exp(-cs)
        g_rem = jnp.exp(cs_tot - cs)
        at_c = (-kk_s[rows, :]) * jnp.exp(cs - lw_s[rows, :])
        bt_c = ba_s[rows, :] * g_inv
        kt_c = k2_s[rows, :] * g_inv
        rt_c = r_ref[rows, :].astype(F32) * jnp.exp(cs)
        bd_c = ba_s[rows, :] * g_rem
        kd_c = k2_s[rows, :] * g_rem
        gl_c = jnp.exp(cs_tot)
        for gi in range(ngrp):
            lanes = lanes_of(gi)
            AT.append(at_c[:, lanes])
            BT.append(bt_c[:, lanes])
            KT.append(kt_c[:, lanes])
            RT.append(rt_c[:, lanes])
            BD.append(bd_c[:, lanes])
            KD.append(kd_c[:, lanes])
            GL.append(gl_c[:, lanes])
            V.append(v_ref[rows, lanes].astype(F32))
    bf = lambda t: t.astype(BF16)
    cat0 = lambda a, b: jnp.concatenate([a, b], axis=0)
    cat1 = lambda a, b: jnp.concatenate([a, b], axis=1)
    X = [bf(cat0(a, r_)) for a, r_ in zip(AT, RT)]
    G1 = [_dot_nt(x, blockdiag(b)) for x, b in zip(X, BT)]
    G2 = [_dot_nt(x, blockdiag(k_)) for x, k_ in zip(X, KT)]
    LAB = [jnp.where(strict, g[0:CHUNK], 0.0) for g in G1]
    MRB = [jnp.where(incl, g[CHUNK:2 * CHUNK], 0.0) for g in G1]
    LAK = [jnp.where(strict, g[0:CHUNK], 0.0) for g in G2]
    MRK = [jnp.where(incl, g[CHUNK:2 * CHUNK], 0.0) for g in G2]
    TM = [eye_p + l for l in LAB]
    XP = [_dot(bf(l), blockdiag(l)) for l in LAB]
    for _ in range(4):
        Y = [_dot(bf(cat0(t, x)), blockdiag(x)) for t, x in zip(TM, XP)]
        TM = [t + y[0:CHUNK] for t, y in zip(TM, Y)]
        XP = [y[CHUNK:2 * CHUNK] for y in Y]
    TM = [t + _dot(bf(t), blockdiag(x)) for t, x in zip(TM, XP)]
    VBD = [blockdiag(v) for v in V]
    LMV = [_dot(bf(cat0(l, m)), vb) for l, m, vb in zip(LAK, MRK, VBD)]
    LAKV = [t[0:CHUNK] for t in LMV]
    MV = [t[CHUNK:2 * CHUNK] for t in LMV]
    WU =[_dot(bf(t), cat1(blockdiag(a), blockdiag(lv))) for t, a, lv in zip(TM, AT, LAKV)]
    WA = [wu[:, 0:L] for wu in WU]
    UV = [wu[:, L:2 * L] for wu in WU]
    RO = [_dot(bf(m), cat1(blockdiag(wa), blockdiag(uv))) for m, wa, uv in zip(MRB, WA, UV)]
    GH = [_dot_tn(bf(bd), bf(cat1(wa, uv))) for bd, wa, uv in zip(BD, WA, UV)]
    KV = [_dot_tn(bf(kd), bf(v)) for kd, v in zip(KD, V)]
    for n_, (c, gi) in enumerate(chains):
        rows, lanes = rows_of(c), lanes_of(gi)
        reff_s[rows, lanes] = RT[n_] + RO[n_][:, 0:L]
        oloc_s[rows, lanes] = RO[n_][:, L:2 * L] + MV[n_]
        gm_s[c, gi] = bf(eye_l * GL[n_] + jnp.where(bd_mask, GH[n_][:, 0:L], 0.0))
        hm_s[c, gi] = jnp.where(bd_mask, GH[n_][:, L:2 * L] + KV[n_], 0.0)

    for ci in range(nc):
        c = (nc - 1 - ci) if reverse else ci
        rows = slice(c * CHUNK, (c + 1) * CHUNK)
        for gi in range(ngrp):
            lanes = slice(gi * L, (gi + 1) * L)
            pst = p_ref[gi].astype(BF16)
            lhs = jnp.concatenate([reff_s[rows, lanes].astype(BF16), gm_s[c, gi]], axis=0)
            yp = _dot(lhs, pst)
            y_s[rows, lanes] = yp[0:CHUNK] + oloc_s[rows, lanes]
            p_ref[gi] = yp[CHUNK:CHUNK + L] + hm_s[c, gi]

    if final:
        y = y_s[...] + yf_ref[...]
        inv_n = 1.0 / RWKV_HEAD
        mu = head_sum(y) * inv_n
        yc = y - mu
        var = head_sum(yc * yc) * inv_n
        yn = yc * lax.rsqrt(var + RWKV_GN_EPS) * lnw_ref[...] + lnb_ref[...]
        o_ref[...] = (yn + bonus) * g_gate


def _rwkv(p, ps, w0, w2p, a0, a2p, k_k, k_a, batch, seq_len, reverse,
          y_fwd=None, g2=None, r_k=None, ln_w=None, ln_b=None):
    final = y_fwd is not None
    tb = min(RWKV_SEQ_TILE, seq_len)
    nt = seq_len // tb
    n = batch * seq_len
    L = RWKV_GROUP * RWKV_HEAD
    ngrp = RWKV_W // L
    lora = RWKV_DECAY_LORA + RWKV_AAA_LORA

    def rowblk(b, j):
        return b * nt + ((nt - 1 - j) if reverse else j)

    vec = pl.BlockSpec((1, RWKV_W), lambda b, j: (0, 0))
    in_specs = [
        pl.BlockSpec((tb, 512), lambda b, j: (rowblk(b, j), COL_R // 512)),
        pl.BlockSpec((tb, 512), lambda b, j: (rowblk(b, j), COL_RK // 512)),
        pl.BlockSpec((tb, 512), lambda b, j: (rowblk(b, j), COL_RV // 512)),
        pl.BlockSpec((tb, 256), lambda b, j: (rowblk(b, j), (COL_RS - P_MAIN_COLS) // 256)),
        vec,
        pl.BlockSpec((lora, RWKV_W), lambda b, j: (0, 0)),
        vec,
        pl.BlockSpec((lora, RWKV_W), lambda b, j: (0, 0)),
        vec, vec,
    ]
    args = [p, p, p, ps, w0, w2p, a0, a2p, k_k, k_a]
    nc = tb // CHUNK
    scratch = ([pltpu.VMEM((ngrp, L, L), F32)] + [pltpu.VMEM((tb, RWKV_W), F32)] * 7
               + [pltpu.VMEM((nc, ngrp, L, L), BF16), pltpu.VMEM((nc, ngrp, L, L), F32)])
    if final:
        in_specs += [
            pl.BlockSpec((tb, RWKV_W), lambda b, j: (rowblk(b, j), 0)),
            pl.BlockSpec((RWKV_GATE_LORA, RWKV_W), lambda b, j: (0, 0)),
            vec, vec, vec,
        ]
        args += [y_fwd, g2, r_k, ln_w, ln_b]
        scratch.append(pltpu.VMEM((tb, RWKV_W), F32))
    kern = functools.partial(_rwkv_kernel, reverse=reverse, tb=tb, final=final)
    return pl.pallas_call(
        kern,
        grid=(batch, nt),
        in_specs=in_specs,
        out_specs=pl.BlockSpec((tb, RWKV_W), lambda b, j: (rowblk(b, j), 0)),
        out_shape=jax.ShapeDtypeStruct((n, RWKV_W), F32),
        scratch_shapes=scratch,
        compiler_params=pltpu.CompilerParams(
            dimension_semantics=("arbitrary", "arbitrary"),
            vmem_limit_bytes=VMEM_LIMIT),
        name="rwkv_bwd" if reverse else "rwkv_fwd",
    )(*args)


def _merge_kernel(x_ref, oa_ref, ob_ref, ga_ref, gb_ref, wa_ref, wb_ref, wo_ref, o_ref):
    ya = _dot(oa_ref[...].astype(BF16), wa_ref[...])
    yb = _dot(ob_ref[...].astype(BF16), wb_ref[...])
    merged = (_sigmoid(ga_ref[...].astype(F32)) * ya
              + _sigmoid(gb_ref[...].astype(F32)) * yb)
    o_ref[...] = x_ref[...] + _dot(merged.astype(BF16), wo_ref[...])


def _merge(x2, oa, ob, p, gla_proj, rwkv_proj, w_out, seq_len):
    n = x2.shape[0]
    tm = min(ROW_TILE, seq_len)
    return pl.pallas_call(
        _merge_kernel,
        grid=(n // tm,),
        in_specs=[
            pl.BlockSpec((tm, D_MODEL), lambda i: (i, 0)),
            pl.BlockSpec((tm, GLA_V_W), lambda i: (i, 0)),
            pl.BlockSpec((tm, RWKV_W), lambda i: (i, 0)),
            pl.BlockSpec((tm, D_MODEL), lambda i: (i, COL_GA // D_MODEL)),
            pl.BlockSpec((tm, D_MODEL), lambda i: (i, COL_GB // D_MODEL)),
            pl.BlockSpec((GLA_V_W, D_MODEL), lambda i: (0, 0)),
            pl.BlockSpec((RWKV_W, D_MODEL), lambda i: (0, 0)),
            pl.BlockSpec((D_MODEL, D_MODEL), lambda i: (0, 0)),
        ],
        out_specs=pl.BlockSpec((tm, D_MODEL), lambda i: (i, 0)),
        out_shape=jax.ShapeDtypeStruct((n, D_MODEL), F32),
        compiler_params=pltpu.CompilerParams(
            dimension_semantics=("arbitrary",),
            vmem_limit_bytes=VMEM_LIMIT),
        name="merge",
    )(x2, oa, ob, p, p, gla_proj, rwkv_proj, w_out)


def _ffn_kernel(xp_ref, x_ref, xn_ref, g2_ref, wu_ref, cw_ref, cb_ref, wd_ref, gf_ref, o_ref,
                h_ref, u_ref, *, tm, tiles_per_seq):
    i = pl.program_id(0)
    pos = lax.rem(i, tiles_per_seq)
    keep_prev = jnp.where(pos == 0, 0.0, 1.0)
    keep_next = jnp.where(pos == tiles_per_seq - 1, 0.0, 1.0)
    g2 = g2_ref[...]
    h_ref[0:HALO, :] = _rmsnorm(xp_ref[...] * keep_prev, g2).astype(BF16)
    h_ref[HALO:HALO + tm, :] = _rmsnorm(x_ref[...], g2).astype(BF16)
    h_ref[HALO + tm:HALO + tm + HALO, :] = _rmsnorm(xn_ref[...] * keep_next, g2).astype(BF16)

    up = lambda c: _dot(h_ref[...], wu_ref[c])
    u_ref[0] = up(0)
    for c in range(FF_NCHUNK):
        slot = c % 2
        if c + 1 < FF_NCHUNK:
            u_ref[1 - slot] = up(c + 1)
        cw = cw_ref[c]
        uc = (cw[0:1, :] * u_ref[slot, HALO - 1:HALO - 1 + tm, :]
              + cw[1:2, :] * u_ref[slot, HALO:HALO + tm, :]
              + cw[2:3, :] * u_ref[slot, HALO + 1:HALO + 1 + tm, :]
              + cb_ref[c])
        ug = uc[:, 0:FF_CHUNK]
        act = (ug * _sigmoid(ug)) * uc[:, FF_CHUNK:2 * FF_CHUNK]
        down = _dot(act.astype(BF16), wd_ref[c])
        acc = down if c == 0 else acc + down
    o_ref[...] = _rmsnorm(x_ref[...] + acc, gf_ref[...])


def _ffn(x1, g2, wu_p, cw_p, cb_p, wd_p, gf, seq_len):
    n = x1.shape[0]
    tm = min(FFN_ROW_TILE, seq_len)
    tiles_per_seq = seq_len // tm
    hb = tm // HALO
    nhb = n // HALO
    kern = functools.partial(_ffn_kernel, tm=tm, tiles_per_seq=tiles_per_seq)
    return pl.pallas_call(
        kern,
        grid=(n // tm,),
        in_specs=[
            pl.BlockSpec((HALO, D_MODEL), lambda i: (jnp.maximum(i * hb - 1, 0), 0)),
            pl.BlockSpec((tm, D_MODEL), lambda i: (i, 0)),
            pl.BlockSpec((HALO, D_MODEL), lambda i: (jnp.minimum((i + 1) * hb, nhb - 1), 0)),
            pl.BlockSpec((1, D_MODEL), lambda i: (0, 0)),
            pl.BlockSpec((FF_NCHUNK, D_MODEL, 2 * FF_CHUNK), lambda i: (0, 0, 0)),
            pl.BlockSpec((FF_NCHUNK, 3, 2 * FF_CHUNK), lambda i: (0, 0, 0)),
            pl.BlockSpec((FF_NCHUNK, 1, 2 * FF_CHUNK), lambda i: (0, 0, 0)),
            pl.BlockSpec((FF_NCHUNK, FF_CHUNK, D_MODEL), lambda i: (0, 0, 0)),
            pl.BlockSpec((1, D_MODEL), lambda i: (0, 0)),
        ],
        out_specs=pl.BlockSpec((tm, D_MODEL), lambda i: (i, 0)),
        out_shape=jax.ShapeDtypeStruct((n, D_MODEL), F32),
        scratch_shapes=[
            pltpu.VMEM((tm + 2 * HALO, D_MODEL), BF16),
            pltpu.VMEM((2, tm + 2 * HALO, 2 * FF_CHUNK), F32),
        ],
        compiler_params=pltpu.CompilerParams(
            dimension_semantics=("arbitrary",),
            vmem_limit_bytes=VMEM_LIMIT),
        name="ffn",
    )(x1, x1, x1, g2, wu_p, cw_p, cb_p, wd_p, gf)


def _pack_w_in(w):
    gla_w = 2 * GLA_QK_W + 2 * GLA_V_W + 2 * GLA_GATE_RANK
    r0 = gla_w
    g0 = r0 + 3 * RWKV_W + RWKV_DECAY_LORA + RWKV_AAA_LORA + RWKV_GATE_LORA
    pad = jnp.zeros((w.shape[0], 256 - 2 * GLA_GATE_RANK), w.dtype)
    return jnp.concatenate([
        w[:, g0:g0 + 2 * D_MODEL],
        w[:, 0:2 * GLA_QK_W + 2 * GLA_V_W],
        w[:, r0:r0 + 3 * RWKV_W + 256],
        w[:, 2 * GLA_QK_W + 2 * GLA_V_W:gla_w],
        pad], axis=1)


def _pack_mu(mu):
    z = jnp.zeros((COL_R,), mu.dtype)
    z2 = jnp.zeros((P_COLS - COL_GS,), mu.dtype)
    return jnp.concatenate([z, mu, z2])[None, :]


def _pack_ff_cols(t):
    pad = [(0, 0)] * (t.ndim - 1) + [(0, FF_PAD - D_FF)]
    g = jnp.pad(t[..., :D_FF], pad)
    v = jnp.pad(t[..., D_FF:], pad)
    lead = t.shape[:-1]
    g = g.reshape(lead + (FF_NCHUNK, FF_CHUNK))
    v = v.reshape(lead + (FF_NCHUNK, FF_CHUNK))
    gv = jnp.concatenate([g, v], axis=-1)
    return jnp.moveaxis(gv, -2, 0)


def kernel(x, norm1_g, w_in, gla_wa2_f, gla_ba_f, gla_wa2_b, gla_ba_b, gla_norm_g, gla_proj, rwkv_mu_prev, rwkv_mu_next, rwkv_w0_f, rwkv_w2_f, rwkv_w0_b, rwkv_w2_b, rwkv_a0, rwkv_a2, rwkv_g2, rwkv_k_k, rwkv_k_a, rwkv_r_k, rwkv_ln_w, rwkv_ln_b, rwkv_proj, w_out, norm2_g, ffn_up, ffn_conv_w, ffn_conv_b, ffn_down, norm_f_g):
    batch, seq_len, d = x.shape
    assert w_in.shape[0] == 1 and d == D_MODEL and seq_len % CHUNK == 0
    x2 = x.reshape(batch * seq_len, d)
    row = lambda t: t.reshape(1, -1)
    for l in range(1):
        w_packed = _pack_w_in(w_in[l]).astype(BF16)
        wa2_f = jnp.pad(gla_wa2_f[l], ((0, 256 - GLA_GATE_RANK), (0, 0)))
        wa2_b = jnp.pad(gla_wa2_b[l], ((GLA_GATE_RANK, 256 - 2 * GLA_GATE_RANK), (0, 0)))
        zl = jnp.zeros((RWKV_AAA_LORA, RWKV_W), F32)
        w2_f = jnp.concatenate([rwkv_w2_f[l], zl], axis=0)
        w2_b = jnp.concatenate([rwkv_w2_b[l], zl], axis=0)
        a2p = jnp.concatenate([jnp.zeros((RWKV_DECAY_LORA, RWKV_W), F32), rwkv_a2[l]], axis=0).astype(BF16)

        p, ps = _inproj(x2, row(norm1_g[l]), w_packed, _pack_mu(rwkv_mu_prev[l]),
                        _pack_mu(rwkv_mu_next[l]), seq_len)

        o_f = _gla(p, ps, wa2_f, row(gla_ba_f[l]), batch, seq_len, reverse=False)
        oa = _gla(p, ps, wa2_b, row(gla_ba_b[l]), batch, seq_len, reverse=True,
                  o_fwd=o_f, norm_g=row(gla_norm_g[l]))

        rw_common = (row(rwkv_a0[l]), a2p, row(rwkv_k_k[l]), row(rwkv_k_a[l]))
        y_f = _rwkv(p, ps, row(rwkv_w0_f[l]), w2_f, *rw_common, batch, seq_len, reverse=False)
        ob = _rwkv(p, ps, row(rwkv_w0_b[l]), w2_b, *rw_common, batch, seq_len, reverse=True,
                   y_fwd=y_f, g2=rwkv_g2[l].astype(BF16), r_k=row(rwkv_r_k[l]),
                   ln_w=row(rwkv_ln_w[l]), ln_b=row(rwkv_ln_b[l]))

        x1 = _merge(x2, oa, ob, p, gla_proj[l].astype(BF16), rwkv_proj[l].astype(BF16),
                    w_out[l].astype(BF16), seq_len)

        wu_p = _pack_ff_cols(ffn_up[l]).astype(BF16)
        cw_p = _pack_ff_cols(ffn_conv_w[l])
        cb_p = _pack_ff_cols(ffn_conv_b[l][None, :])
        wd_p = jnp.pad(ffn_down[l], ((0, FF_PAD - D_FF), (0, 0))).reshape(
            FF_NCHUNK, FF_CHUNK, D_MODEL).astype(BF16)
        x2 = _ffn(x1, row(norm2_g[l]), wu_p, cw_p, cb_p, wd_p, row(norm_f_g), seq_len)
    return x2.reshape(batch, seq_len, d)
```

```python
import functools

import jax
import jax.numpy as jnp
from jax import lax
from jax.experimental import pallas as pl
from jax.experimental.pallas import tpu as pltpu

F32 = jnp.float32
BF16 = jnp.bfloat16

D_MODEL = 1024
GLA_HEADS = 4
GLA_DK = 64
GLA_DV = 128
GLA_QK_W = GLA_HEADS * GLA_DK
GLA_V_W = GLA_HEADS * GLA_DV
GLA_GATE_RANK = 16
GLA_LOGIT_NORM = 16.0
CHUNK = 64
RWKV_HEAD = 64
RWKV_W = 512
RWKV_HEADS = RWKV_W // RWKV_HEAD
RWKV_DECAY_LORA = 64
RWKV_AAA_LORA = 64
RWKV_GATE_LORA = 128
RWKV_GN_EPS = RWKV_HEAD * 1e-5
D_FF = 2752
NORM_EPS = 1e-6
HEAD_NORM_EPS = 1e-5

COL_GA, COL_GB = 0, 1024
COL_QK, COL_V, COL_OG = 2048, 2560, 3072
COL_R, COL_RK, COL_RV = 3584, 4096, 4608
COL_RS, COL_GS = 5120, 5376
P_COLS = 5632
P_MAIN_COLS = COL_RS
HALO = 16

FF_CHUNK = 256
FF_PAD = 2816
FF_NCHUNK = FF_PAD // FF_CHUNK

VMEM_LIMIT = 56 * 1024 * 1024
ROW_TILE = 1024
INPROJ_ROW_TILE = 512
FFN_ROW_TILE = 512
GLA_SEQ_TILE = 1024
RWKV_SEQ_TILE = 256
RWKV_GROUP = 4


def _sigmoid(x):
    return 1.0 / (1.0 + jnp.exp(-x))


def _softplus(z):
    return jnp.maximum(z, 0.0) + jnp.log(1.0 + jnp.exp(-jnp.abs(z)))


def _split3(x):
    hi = x.astype(BF16)
    r1 = x - hi.astype(F32)
    mid = r1.astype(BF16)
    lo = (r1 - mid.astype(F32)).astype(BF16)
    return hi, mid, lo


def _dot(a, b):
    return jnp.dot(a, b, preferred_element_type=F32)


def _dot_nt(a, b):
    return lax.dot_general(a, b, (((1,), (1,)), ((), ())), preferred_element_type=F32)


def _dot_tn(a, b):
    return lax.dot_general(a, b, (((0,), (0,)), ((), ())), preferred_element_type=F32)


def _dot_exact_lhs(sel, x):
    hi = x.astype(BF16)
    lo = (x - hi.astype(F32)).astype(BF16)
    s = sel.astype(BF16)
    return _dot(s, hi) + _dot(s, lo)


def _dot_exact_rhs(x, sel):
    hi = x.astype(BF16)
    lo = (x - hi.astype(F32)).astype(BF16)
    s = sel.astype(BF16)
    return _dot(hi, s) + _dot(lo, s)


def _dot_x3(a, b):
    ah = a.astype(BF16)
    al = (a - ah.astype(F32)).astype(BF16)
    bh = b.astype(BF16)
    bl = (b - bh.astype(F32)).astype(BF16)
    return _dot(ah, bh) + _dot(al, bh) + _dot(ah, bl)


def _rmsnorm(x, g):
    ms = jnp.mean(x * x, axis=-1, keepdims=True)
    return x * lax.rsqrt(ms + NORM_EPS) * g


def _iota2(shape, dim):
    return lax.broadcasted_iota(jnp.int32, shape, dim)


def _div(x, size):
    return lax.shift_right_logical(x, size.bit_length() - 1)


def _mod(x, size):
    return lax.bitwise_and(x, size - 1)


def _chunk_cumsum(x, reverse):
    rows = x.shape[0]
    blk = min(rows, 256)
    row = _iota2((blk, blk), 0)
    col = _iota2((blk, blk), 1)
    same = _div(row, CHUNK) == _div(col, CHUNK)
    tri = jnp.where(same & ((col >= row) if reverse else (col <= row)), 1.0, 0.0)
    parts = [_dot_exact_lhs(tri, x[i:i + blk]) for i in range(0, rows, blk)]
    return parts[0] if len(parts) == 1 else jnp.concatenate(parts, axis=0)


def _blockdiag(z, head_masks):
    return jnp.concatenate([jnp.where(m, z, 0.0) for m in head_masks], axis=0).astype(BF16)


def _inproj_kernel(xp_ref, x_ref, xn_ref, g_ref, w_ref, mup_ref, mun_ref, o_ref, os_ref,
                   h_ref, p_ref, *, tm, tiles_per_seq, first_shift_tile, n_main_tiles, n_sub):
    i = pl.program_id(0)
    n_tiles, _, tn = w_ref.shape
    pos = lax.rem(i, tiles_per_seq)
    keep_prev = jnp.where(pos == 0, 0.0, 1.0)
    keep_next = jnp.where(pos == tiles_per_seq - 1, 0.0, 1.0)
    g = g_ref[...]
    h_ref[0:HALO, :] = _rmsnorm(xp_ref[...] * keep_prev, g).astype(BF16)
    h_ref[HALO:HALO + tm, :] = _rmsnorm(x_ref[...], g).astype(BF16)
    h_ref[HALO + tm:HALO + tm + HALO, :] = _rmsnorm(xn_ref[...] * keep_next, g).astype(BF16)

    for t in range(first_shift_tile):
        o_ref[:, t * tn:(t + 1) * tn] = _dot(h_ref[HALO:HALO + tm, :], w_ref[t]).astype(o_ref.dtype)

    sub = tm // n_sub
    jobs = [(t, s) for t in range(first_shift_tile, n_tiles) for s in range(n_sub)]
    mm = lambda t, s: _dot(h_ref[s * sub:(s + 1) * sub + 2 * HALO, :], w_ref[t])
    p_ref[0] = mm(*jobs[0])
    for n_, (t, s) in enumerate(jobs):
        slot = n_ % 2
        if n_ + 1 < len(jobs):
            p_ref[1 - slot] = mm(*jobs[n_ + 1])
        p = p_ref[slot, HALO:HALO + sub, :]
        pp = p_ref[slot, HALO - 1:HALO - 1 + sub, :]
        pn = p_ref[slot, HALO + 1:HALO + 1 + sub, :]
        val = p + mup_ref[t] * (pp - p) + mun_ref[t] * (pn - p)
        rows = slice(s * sub, (s + 1) * sub)
        if t < n_main_tiles:
            o_ref[rows, t * tn:(t + 1) * tn] = val.astype(o_ref.dtype)
        else:
            os_ref[rows, :] = val


def _inproj(x2, g, w_packed, mu_prev, mu_next, seq_len):
    n = x2.shape[0]
    tm = min(INPROJ_ROW_TILE, seq_len)
    tn = 512
    n_tiles = P_COLS // tn
    tiles_per_seq = seq_len // tm
    hb = tm // HALO
    nhb = n // HALO
    n_sub = max(1, tm // 256)
    sub = tm // n_sub
    kern = functools.partial(_inproj_kernel, tm=tm, tiles_per_seq=tiles_per_seq,
                             first_shift_tile=COL_R // tn, n_main_tiles=P_MAIN_COLS // tn,
                             n_sub=n_sub)
    w_t = jnp.moveaxis(w_packed.reshape(D_MODEL, n_tiles, tn), 1, 0)
    mup_t = jnp.moveaxis(mu_prev.reshape(1, n_tiles, tn), 1, 0)
    mun_t = jnp.moveaxis(mu_next.reshape(1, n_tiles, tn), 1, 0)
    return pl.pallas_call(
        kern,
        grid=(n // tm,),
        in_specs=[
            pl.BlockSpec((HALO, D_MODEL), lambda i: (jnp.maximum(i * hb - 1, 0), 0)),
            pl.BlockSpec((tm, D_MODEL), lambda i: (i, 0)),
            pl.BlockSpec((HALO, D_MODEL), lambda i: (jnp.minimum((i + 1) * hb, nhb - 1), 0)),
            pl.BlockSpec((1, D_MODEL), lambda i: (0, 0)),
            pl.BlockSpec((n_tiles, D_MODEL, tn), lambda i: (0, 0, 0)),
            pl.BlockSpec((n_tiles, 1, tn), lambda i: (0, 0, 0)),
            pl.BlockSpec((n_tiles, 1, tn), lambda i: (0, 0, 0)),
        ],
        out_specs=[pl.BlockSpec((tm, P_MAIN_COLS), lambda i: (i, 0)),
                   pl.BlockSpec((tm, P_COLS - P_MAIN_COLS), lambda i: (i, 0))],
        out_shape=[jax.ShapeDtypeStruct((n, P_MAIN_COLS), BF16),
                   jax.ShapeDtypeStruct((n, P_COLS - P_MAIN_COLS), F32)],
        scratch_shapes=[
            pltpu.VMEM((tm + 2 * HALO, D_MODEL), BF16),
            pltpu.VMEM((2, sub + 2 * HALO, tn), F32),
        ],
        compiler_params=pltpu.CompilerParams(
            dimension_semantics=("arbitrary",),
            vmem_limit_bytes=VMEM_LIMIT),
        name="inproj",
    )(x2, x2, x2, g, w_t, mup_t, mun_t)


def _gla_kernel(*refs, reverse, tb, final):
    if final:
        (qk_ref, v_ref, gs_ref, wa2_ref, ba_ref, of_ref, og_ref, ng_ref,
         o_ref, st_ref, b_ref, acc_ref) = refs
    else:
        qk_ref, v_ref, gs_ref, wa2_ref, ba_ref, o_ref, st_ref, b_ref = refs
    j = pl.program_id(1)
    nc = tb // CHUNK

    @pl.when(j == 0)
    def _():
        st_ref[...] = jnp.zeros_like(st_ref)

    logit = _dot_x3(gs_ref[...], wa2_ref[...]) + ba_ref[...]
    la = (jnp.minimum(logit, 0.0) - jnp.log(1.0 + jnp.exp(-jnp.abs(logit)))) * (1.0 / GLA_LOGIT_NORM)
    b_ref[...] = _chunk_cumsum(la, reverse)

    lane_k = _div(_iota2((CHUNK, GLA_QK_W), 1), GLA_DK)
    lane_v = _div(_iota2((CHUNK, GLA_V_W), 1), GLA_DV)
    k_masks = [lane_k == h for h in range(GLA_HEADS)]
    v_masks = [lane_v == h for h in range(GLA_HEADS)]
    t_idx = _iota2((CHUNK, GLA_QK_W), 0)
    s_idx = _mod(_iota2((CHUNK, GLA_QK_W), 1), CHUNK)
    causal = (s_idx >= t_idx) if reverse else (s_idx <= t_idx)
    st_mask = (_div(_iota2((GLA_V_W, GLA_QK_W), 0), GLA_DV)
               == _div(_iota2((GLA_V_W, GLA_QK_W), 1), GLA_DK))
    i_ref = (CHUNK - 1 - CHUNK // 2) if reverse else CHUNK // 2
    i_last = 0 if reverse else CHUNK - 1

    QI, KD, QE, VB, KI_BD, V_BD, DEC = ([] for _ in range(7))
    for c in range(nc):
        rows = slice(c * CHUNK, (c + 1) * CHUNK)
        b = b_ref[rows, :]
        b_mid = b[i_ref:i_ref + 1, :]
        b_last = b[i_last:i_last + 1, :]
        q = qk_ref[rows, 0:GLA_QK_W].astype(F32) * (GLA_DK ** -0.5)
        k = qk_ref[rows, GLA_QK_W:2 * GLA_QK_W].astype(F32)
        v = v_ref[rows, :].astype(F32)
        QI.append((q * jnp.exp(b - b_mid)).astype(BF16))
        KD.append((k * jnp.exp(b_last - b)).astype(BF16))
        QE.append((q * jnp.exp(b)).astype(BF16))
        VB.append(v.astype(BF16))
        KI_BD.append(_blockdiag(k * jnp.exp(b_mid - b), k_masks))
        V_BD.append(_blockdiag(v, v_masks))
        DEC.append(jnp.exp(b_last))
    A = [jnp.where(causal, _dot_nt(qi, kb), 0.0) for qi, kb in zip(QI, KI_BD)]
    O_IN = [_dot(a.astype(BF16), vb) for a, vb in zip(A, V_BD)]
    KV = [jnp.where(st_mask, _dot_tn(vb, kd), 0.0) for vb, kd in zip(VB, KD)]
    for ci in range(nc):
        c = (nc - 1 - ci) if reverse else ci
        rows = slice(c * CHUNK, (c + 1) * CHUNK)
        st = st_ref[...]
        o = O_IN[c] + _dot_nt(QE[c], st.astype(BF16))
        st_ref[...] = st * DEC[c] + KV[c]
        if final:
            acc_ref[rows, :] = of_ref[rows, :] + o
        else:
            o_ref[rows, :] = o

    if final:
        o = acc_ref[...]
        parts = []
        for h in range(GLA_HEADS):
            oh = o[:, h * GLA_DV:(h + 1) * GLA_DV]
            ms = jnp.mean(oh * oh, axis=-1, keepdims=True)
            parts.append(oh * lax.rsqrt(ms + HEAD_NORM_EPS))
        on = jnp.concatenate(parts, axis=1) * ng_ref[...]
        og = og_ref[...].astype(F32)
        o_ref[...] = on * (og * _sigmoid(og))


def _gla(p, ps, wa2p, ba, batch, seq_len, reverse, o_fwd=None, norm_g=None):
    final = o_fwd is not None
    tb = min(GLA_SEQ_TILE, seq_len)
    nt = seq_len // tb
    n = batch * seq_len

    def rowblk(b, j):
        return b * nt + ((nt - 1 - j) if reverse else j)

    in_specs = [
        pl.BlockSpec((tb, 512), lambda b, j: (rowblk(b, j), COL_QK // 512)),
        pl.BlockSpec((tb, 512), lambda b, j: (rowblk(b, j), COL_V // 512)),
        pl.BlockSpec((tb, 256), lambda b, j: (rowblk(b, j), (COL_GS - P_MAIN_COLS) // 256)),
        pl.BlockSpec((256, GLA_QK_W), lambda b, j: (0, 0)),
        pl.BlockSpec((1, GLA_QK_W), lambda b, j: (0, 0)),
    ]
    args = [p, p, ps, wa2p, ba]
    scratch = [pltpu.VMEM((GLA_V_W, GLA_QK_W), F32), pltpu.VMEM((tb, GLA_QK_W), F32)]
    if final:
        in_specs += [
            pl.BlockSpec((tb, GLA_V_W), lambda b, j: (rowblk(b, j), 0)),
            pl.BlockSpec((tb, 512), lambda b, j: (rowblk(b, j), COL_OG // 512)),
            pl.BlockSpec((1, GLA_V_W), lambda b, j: (0, 0)),
        ]
        args += [o_fwd, p, norm_g]
        scratch.append(pltpu.VMEM((tb, GLA_V_W), F32))
    kern = functools.partial(_gla_kernel, reverse=reverse, tb=tb, final=final)
    return pl.pallas_call(
        kern,
        grid=(batch, nt),
        in_specs=in_specs,
        out_specs=pl.BlockSpec((tb, GLA_V_W), lambda b, j: (rowblk(b, j), 0)),
        out_shape=jax.ShapeDtypeStruct((n, GLA_V_W), F32),
        scratch_shapes=scratch,
        compiler_params=pltpu.CompilerParams(
            dimension_semantics=("arbitrary", "arbitrary"),
            vmem_limit_bytes=VMEM_LIMIT),
        name="gla_bwd" if reverse else "gla_fwd",
    )(*args)


def _rwkv_kernel(*refs, reverse, tb, final):
    if final:
        (r_ref, k_ref, v_ref, rs_ref, w0_ref, w2_ref, a0_ref, a2_ref, kk_ref, ka_ref,
         yf_ref, g2_ref, rk_ref, lnw_ref, lnb_ref,
         o_ref, p_ref, cs_s, lw_s, kk_s, ba_s, k2_s, reff_s, oloc_s, gm_s, hm_s, y_s) = refs
    else:
        (r_ref, k_ref, v_ref, rs_ref, w0_ref, w2_ref, a0_ref, a2_ref, kk_ref, ka_ref,
         o_ref, p_ref, cs_s, lw_s, kk_s, ba_s, k2_s, reff_s, oloc_s, gm_s, hm_s) = refs
        y_s = o_ref
    j = pl.program_id(1)
    nc = tb // CHUNK
    G = RWKV_GROUP
    L = G * RWKV_HEAD
    ngrp = RWKV_W // L

    @pl.when(j == 0)
    def _():
        p_ref[...] = jnp.zeros_like(p_ref)

    r = r_ref[...].astype(F32)
    k = k_ref[...].astype(F32)
    lora_in = rs_ref[:, 0:RWKV_DECAY_LORA + RWKV_AAA_LORA]
    w_pre = w0_ref[...] + _dot_x3(jnp.tanh(lora_in), w2_ref[...])
    logw = -jnp.exp(-_softplus(-w_pre) - 0.5)
    a_lr = _sigmoid(a0_ref[...] + _dot(lora_in.astype(BF16), a2_ref[...]))
    hr = _div(_iota2((RWKV_W, RWKV_W), 0), RWKV_HEAD)
    hc = _div(_iota2((RWKV_W, RWKV_W), 1), RWKV_HEAD)
    head_ones = jnp.where(hr == hc, 1.0, 0.0).astype(BF16)
    head_sum = lambda t: _dot(t.astype(BF16), head_ones)
    kk = k * kk_ref[...]
    ssq = head_sum(kk * kk)
    kk = kk / jnp.maximum(jnp.sqrt(ssq), 1e-12)
    k2 = k * (1.0 + (a_lr - 1.0) * ka_ref[...])
    if final:
        bonus = head_sum(r * k2 * rk_ref[...]) * v_ref[...].astype(F32)
        gl = rs_ref[:, RWKV_DECAY_LORA + RWKV_AAA_LORA:]
        g_gate = _dot(_sigmoid(gl).astype(BF16), g2_ref[...])

    cs_s[...] = _chunk_cumsum(logw, reverse)
    lw_s[...] = logw
    kk_s[...] = kk
    ba_s[...] = kk * a_lr
    k2_s[...] = k2
    i_last = 0 if reverse else CHUNK - 1

    lane_h = _div(_iota2((CHUNK, L), 1), RWKV_HEAD)
    h_masks = [lane_h == h for h in range(G)]
    t_idx = _iota2((CHUNK, L), 0)
    s_idx = _mod(_iota2((CHUNK, L), 1), CHUNK)
    strict = (s_idx > t_idx) if reverse else (s_idx < t_idx)
    incl = (s_idx >= t_idx) if reverse else (s_idx <= t_idx)
    eye_p = jnp.where(s_idx == t_idx, 1.0, 0.0)
    bd_mask = _div(_iota2((L, L), 0), RWKV_HEAD) == _div(_iota2((L, L), 1), RWKV_HEAD)
    eye_l = jnp.where(_iota2((L, L), 0) == _iota2((L, L), 1), 1.0, 0.0)

    def blockdiag(z):
        return _blockdiag(z, h_masks)

    chains = [(c, gi) for c in range(nc) for gi in range(ngrp)]
    rows_of = lambda c: slice(c * CHUNK, (c + 1) * CHUNK)
    lanes_of = lambda gi: slice(gi * L, (gi + 1) * L)
    AT, BT, KT, RT, BD, KD, GL, V = ([] for _ in range(8))
    for c in range(nc):
        rows = rows_of(c)
        cs = cs_s[rows, :]
        cs_tot = cs[i_last:i_last + 1, :]
        g_inv = jnp.exp(-cs)
        g_rem = jnp.exp(cs_tot - cs)
        at_c = (-kk_s[rows, :]) * jnp.exp(cs - lw_s[rows, :])
        bt_c = ba_s[rows, :] * g_inv
        kt_c = k2_s[rows, :] * g_inv
        rt_c = r_ref[rows, :].astype(F32) * jnp.exp(cs)
        bd_c = ba_s[rows, :] * g_rem
        kd_c = k2_s[rows, :] * g_rem
        gl_c = jnp.exp(cs_tot)
        for gi in range(ngrp):
            lanes = lanes_of(gi)
            AT.append(at_c[:, lanes])
            BT.append(bt_c[:, lanes])
            KT.append(kt_c[:, lanes])
            RT.append(rt_c[:, lanes])
            BD.append(bd_c[:, lanes])
            KD.append(kd_c[:, lanes])
            GL.append(gl_c[:, lanes])
            V.append(v_ref[rows, lanes].astype(F32))
    bf = lambda t: t.astype(BF16)
    cat0 = lambda a, b: jnp.concatenate([a, b], axis=0)
    cat1 = lambda a, b: jnp.concatenate([a, b], axis=1)
    X = [bf(cat0(a, r_)) for a, r_ in zip(AT, RT)]
    G1 = [_dot_nt(x, blockdiag(b)) for x, b in zip(X, BT)]
    G2 = [_dot_nt(x, blockdiag(k_)) for x, k_ in zip(X, KT)]
    LAB = [jnp.where(strict, g[0:CHUNK], 0.0) for g in G1]
    MRB = [jnp.where(incl, g[CHUNK:2 * CHUNK], 0.0) for g in G1]
    LAK = [jnp.where(strict, g[0:CHUNK], 0.0) for g in G2]
    MRK = [jnp.where(incl, g[CHUNK:2 * CHUNK], 0.0) for g in G2]
    TM = [eye_p + l for l in LAB]
    XP = [_dot(bf(l), blockdiag(l)) for l in LAB]
    for _ in range(4):
        Y = [_dot(bf(cat0(t, x)), blockdiag(x)) for t, x in zip(TM, XP)]
        TM = [t + y[0:CHUNK] for t, y in zip(TM, Y)]
        XP = [y[CHUNK:2 * CHUNK] for y in Y]
    TM = [t + _dot(bf(t), blockdiag(x)) for t, x in zip(TM, XP)]
    VBD = [blockdiag(v) for v in V]
    LMV = [_dot(bf(cat0(l, m)), vb) for l, m, vb in zip(LAK, MRK, VBD)]
    LAKV = [t[0:CHUNK] for t in LMV]
    MV = [t[CHUNK:2 * CHUNK] for t in LMV]
    WU =[_dot(bf(t), cat1(blockdiag(a), blockdiag(lv))) for t, a, lv in zip(TM, AT, LAKV)]
    WA = [wu[:, 0:L] for wu in WU]
    UV = [wu[:, L:2 * L] for wu in WU]
    RO = [_dot(bf(m), cat1(blockdiag(wa), blockdiag(uv))) for m, wa, uv in zip(MRB, WA, UV)]
    PW = 2 * RWKV_HEAD
    zpair = jnp.zeros((PW, PW), F32)

    def pair_tn(bd, wa, uv):
        blocks = []
        for q in range(L // PW):
            pl_ = slice(q * PW, (q + 1) * PW)
            blocks.append(_dot_tn(bf(bd[:, pl_]), bf(cat1(wa[:, pl_], uv[:, pl_]))))
        rows_g = [cat1(*[blocks[q][:, 0:PW] if q2 == q else zpair for q2 in range(L // PW)])
                  for q in range(L // PW)]
        rows_h = [cat1(*[blocks[q][:, PW:2 * PW] if q2 == q else zpair for q2 in range(L // PW)])
                  for q in range(L // PW)]
        return cat1(jnp.concatenate(rows_g, axis=0), jnp.concatenate(rows_h, axis=0))

    if final:
        GH = [pair_tn(bd, wa, uv) for bd, wa, uv in zip(BD, WA, UV)]
    else:
        GH = [_dot_tn(bf(bd), bf(cat1(wa, uv))) for bd, wa, uv in zip(BD, WA, UV)]
    KV = [_dot_tn(bf(kd), bf(v)) for kd, v in zip(KD, V)]
    for n_, (c, gi) in enumerate(chains):
        rows, lanes = rows_of(c), lanes_of(gi)
        reff_s[rows, lanes] = RT[n_] + RO[n_][:, 0:L]
        oloc_s[rows, lanes] = RO[n_][:, L:2 * L] + MV[n_]
        gm_s[c, gi] = bf(eye_l * GL[n_] + jnp.where(bd_mask, GH[n_][:, 0:L], 0.0))
        hm_s[c, gi] = jnp.where(bd_mask, GH[n_][:, L:2 * L] + KV[n_], 0.0)

    for ci in range(nc):
        c = (nc - 1 - ci) if reverse else ci
        rows = slice(c * CHUNK, (c + 1) * CHUNK)
        for gi in range(ngrp):
            lanes = slice(gi * L, (gi + 1) * L)
            pst = p_ref[gi].astype(BF16)
            lhs = jnp.concatenate([reff_s[rows, lanes].astype(BF16), gm_s[c, gi]], axis=0)
            yp = _dot(lhs, pst)
            y_s[rows, lanes] = yp[0:CHUNK] + oloc_s[rows, lanes]
            p_ref[gi] = yp[CHUNK:CHUNK + L] + hm_s[c, gi]

    if final:
        y = y_s[...] + yf_ref[...]
        inv_n = 1.0 / RWKV_HEAD
        mu = head_sum(y) * inv_n
        yc = y - mu
        var = head_sum(yc * yc) * inv_n
        yn = yc * lax.rsqrt(var + RWKV_GN_EPS) * lnw_ref[...] + lnb_ref[...]
        o_ref[...] = (yn + bonus) * g_gate


def _rwkv(p, ps, w0, w2p, a0, a2p, k_k, k_a, batch, seq_len, reverse,
          y_fwd=None, g2=None, r_k=None, ln_w=None, ln_b=None):
    final = y_fwd is not None
    tb = min(RWKV_SEQ_TILE, seq_len)
    nt = seq_len // tb
    n = batch * seq_len
    L = RWKV_GROUP * RWKV_HEAD
    ngrp = RWKV_W // L
    lora = RWKV_DECAY_LORA + RWKV_AAA_LORA

    def rowblk(b, j):
        return b * nt + ((nt - 1 - j) if reverse else j)

    vec = pl.BlockSpec((1, RWKV_W), lambda b, j: (0, 0))
    in_specs = [
        pl.BlockSpec((tb, 512), lambda b, j: (rowblk(b, j), COL_R // 512)),
        pl.BlockSpec((tb, 512), lambda b, j: (rowblk(b, j), COL_RK // 512)),
        pl.BlockSpec((tb, 512), lambda b, j: (rowblk(b, j), COL_RV // 512)),
        pl.BlockSpec((tb, 256), lambda b, j: (rowblk(b, j), (COL_RS - P_MAIN_COLS) // 256)),
        vec,
        pl.BlockSpec((lora, RWKV_W), lambda b, j: (0, 0)),
        vec,
        pl.BlockSpec((lora, RWKV_W), lambda b, j: (0, 0)),
        vec, vec,
    ]
    args = [p, p, p, ps, w0, w2p, a0, a2p, k_k, k_a]
    nc = tb // CHUNK
    scratch = ([pltpu.VMEM((ngrp, L, L), F32)] + [pltpu.VMEM((tb, RWKV_W), F32)] * 7
               + [pltpu.VMEM((nc, ngrp, L, L), BF16), pltpu.VMEM((nc, ngrp, L, L), F32)])
    if final:
        in_specs += [
            pl.BlockSpec((tb, RWKV_W), lambda b, j: (rowblk(b, j), 0)),
            pl.BlockSpec((RWKV_GATE_LORA, RWKV_W), lambda b, j: (0, 0)),
            vec, vec, vec,
        ]
        args += [y_fwd, g2, r_k, ln_w, ln_b]
        scratch.append(pltpu.VMEM((tb, RWKV_W), F32))
    kern = functools.partial(_rwkv_kernel, reverse=reverse, tb=tb, final=final)
    return pl.pallas_call(
        kern,
        grid=(batch, nt),
        in_specs=in_specs,
        out_specs=pl.BlockSpec((tb, RWKV_W), lambda b, j: (rowblk(b, j), 0)),
        out_shape=jax.ShapeDtypeStruct((n, RWKV_W), F32),
        scratch_shapes=scratch,
        compiler_params=pltpu.CompilerParams(
            dimension_semantics=("arbitrary", "arbitrary"),
            vmem_limit_bytes=VMEM_LIMIT),
        name="rwkv_bwd" if reverse else "rwkv_fwd",
    )(*args)


def _merge_kernel(x_ref, oa_ref, ob_ref, ga_ref, gb_ref, wa_ref, wb_ref, wo_ref, o_ref):
    ya = _dot(oa_ref[...].astype(BF16), wa_ref[...])
    yb = _dot(ob_ref[...].astype(BF16), wb_ref[...])
    merged = (_sigmoid(ga_ref[...].astype(F32)) * ya
              + _sigmoid(gb_ref[...].astype(F32)) * yb)
    o_ref[...] = x_ref[...] + _dot(merged.astype(BF16), wo_ref[...])


def _merge(x2, oa, ob, p, gla_proj, rwkv_proj, w_out, seq_len):
    n = x2.shape[0]
    tm = min(ROW_TILE, seq_len)
    return pl.pallas_call(
        _merge_kernel,
        grid=(n // tm,),
        in_specs=[
            pl.BlockSpec((tm, D_MODEL), lambda i: (i, 0)),
            pl.BlockSpec((tm, GLA_V_W), lambda i: (i, 0)),
            pl.BlockSpec((tm, RWKV_W), lambda i: (i, 0)),
            pl.BlockSpec((tm, D_MODEL), lambda i: (i, COL_GA // D_MODEL)),
            pl.BlockSpec((tm, D_MODEL), lambda i: (i, COL_GB // D_MODEL)),
            pl.BlockSpec((GLA_V_W, D_MODEL), lambda i: (0, 0)),
            pl.BlockSpec((RWKV_W, D_MODEL), lambda i: (0, 0)),
            pl.BlockSpec((D_MODEL, D_MODEL), lambda i: (0, 0)),
        ],
        out_specs=pl.BlockSpec((tm, D_MODEL), lambda i: (i, 0)),
        out_shape=jax.ShapeDtypeStruct((n, D_MODEL), F32),
        compiler_params=pltpu.CompilerParams(
            dimension_semantics=("arbitrary",),
            vmem_limit_bytes=VMEM_LIMIT),
        name="merge",
    )(x2, oa, ob, p, p, gla_proj, rwkv_proj, w_out)


def _ffn_kernel(xp_ref, x_ref, xn_ref, g2_ref, wu_ref, cw_ref, cb_ref, wd_ref, gf_ref, o_ref,
                h_ref, u_ref, *, tm, tiles_per_seq):
    i = pl.program_id(0)
    pos = lax.rem(i, tiles_per_seq)
    keep_prev = jnp.where(pos == 0, 0.0, 1.0)
    keep_next = jnp.where(pos == tiles_per_seq - 1, 0.0, 1.0)
    g2 = g2_ref[...]
    h_ref[0:HALO, :] = _rmsnorm(xp_ref[...] * keep_prev, g2).astype(BF16)
    h_ref[HALO:HALO + tm, :] = _rmsnorm(x_ref[...], g2).astype(BF16)
    h_ref[HALO + tm:HALO + tm + HALO, :] = _rmsnorm(xn_ref[...] * keep_next, g2).astype(BF16)

    up = lambda c: _dot(h_ref[...], wu_ref[c])
    u_ref[0] = up(0)
    for c in range(FF_NCHUNK):
        slot = c % 2
        if c + 1 < FF_NCHUNK:
            u_ref[1 - slot] = up(c + 1)
        cw = cw_ref[c]
        uc = (cw[0:1, :] * u_ref[slot, HALO - 1:HALO - 1 + tm, :]
              + cw[1:2, :] * u_ref[slot, HALO:HALO + tm, :]
              + cw[2:3, :] * u_ref[slot, HALO + 1:HALO + 1 + tm, :]
              + cb_ref[c])
        ug = uc[:, 0:FF_CHUNK]
        act = (ug * _sigmoid(ug)) * uc[:, FF_CHUNK:2 * FF_CHUNK]
        down = _dot(act.astype(BF16), wd_ref[c])
        acc = down if c == 0 else acc + down
    o_ref[...] = _rmsnorm(x_ref[...] + acc, gf_ref[...])


def _ffn(x1, g2, wu_p, cw_p, cb_p, wd_p, gf, seq_len):
    n = x1.shape[0]
    tm = min(FFN_ROW_TILE, seq_len)
    tiles_per_seq = seq_len // tm
    hb = tm // HALO
    nhb = n // HALO
    kern = functools.partial(_ffn_kernel, tm=tm, tiles_per_seq=tiles_per_seq)
    return pl.pallas_call(
        kern,
        grid=(n // tm,),
        in_specs=[
            pl.BlockSpec((HALO, D_MODEL), lambda i: (jnp.maximum(i * hb - 1, 0), 0)),
            pl.BlockSpec((tm, D_MODEL), lambda i: (i, 0)),
            pl.BlockSpec((HALO, D_MODEL), lambda i: (jnp.minimum((i + 1) * hb, nhb - 1), 0)),
            pl.BlockSpec((1, D_MODEL), lambda i: (0, 0)),
            pl.BlockSpec((FF_NCHUNK, D_MODEL, 2 * FF_CHUNK), lambda i: (0, 0, 0)),
            pl.BlockSpec((FF_NCHUNK, 3, 2 * FF_CHUNK), lambda i: (0, 0, 0)),
            pl.BlockSpec((FF_NCHUNK, 1, 2 * FF_CHUNK), lambda i: (0, 0, 0)),
            pl.BlockSpec((FF_NCHUNK, FF_CHUNK, D_MODEL), lambda i: (0, 0, 0)),
            pl.BlockSpec((1, D_MODEL), lambda i: (0, 0)),
        ],
        out_specs=pl.BlockSpec((tm, D_MODEL), lambda i: (i, 0)),
        out_shape=jax.ShapeDtypeStruct((n, D_MODEL), F32),
        scratch_shapes=[
            pltpu.VMEM((tm + 2 * HALO, D_MODEL), BF16),
            pltpu.VMEM((2, tm + 2 * HALO, 2 * FF_CHUNK), F32),
        ],
        compiler_params=pltpu.CompilerParams(
            dimension_semantics=("arbitrary",),
            vmem_limit_bytes=VMEM_LIMIT),
        name="ffn",
    )(x1, x1, x1, g2, wu_p, cw_p, cb_p, wd_p, gf)


def _pack_w_in(w):
    gla_w = 2 * GLA_QK_W + 2 * GLA_V_W + 2 * GLA_GATE_RANK
    r0 = gla_w
    g0 = r0 + 3 * RWKV_W + RWKV_DECAY_LORA + RWKV_AAA_LORA + RWKV_GATE_LORA
    pad = jnp.zeros((w.shape[0], 256 - 2 * GLA_GATE_RANK), w.dtype)
    return jnp.concatenate([
        w[:, g0:g0 + 2 * D_MODEL],
        w[:, 0:2 * GLA_QK_W + 2 * GLA_V_W],
        w[:, r0:r0 + 3 * RWKV_W + 256],
        w[:, 2 * GLA_QK_W + 2 * GLA_V_W:gla_w],
        pad], axis=1)


def _pack_mu(mu):
    z = jnp.zeros((COL_R,), mu.dtype)
    z2 = jnp.zeros((P_COLS - COL_GS,), mu.dtype)
    return jnp.concatenate([z, mu, z2])[None, :]


def _pack_ff_cols(t):
    pad = [(0, 0)] * (t.ndim - 1) + [(0, FF_PAD - D_FF)]
    g = jnp.pad(t[..., :D_FF], pad)
    v = jnp.pad(t[..., D_FF:], pad)
    lead = t.shape[:-1]
    g = g.reshape(lead + (FF_NCHUNK, FF_CHUNK))
    v = v.reshape(lead + (FF_NCHUNK, FF_CHUNK))
    gv = jnp.concatenate([g, v], axis=-1)
    return jnp.moveaxis(gv, -2, 0)


def kernel(x, norm1_g, w_in, gla_wa2_f, gla_ba_f, gla_wa2_b, gla_ba_b, gla_norm_g, gla_proj, rwkv_mu_prev, rwkv_mu_next, rwkv_w0_f, rwkv_w2_f, rwkv_w0_b, rwkv_w2_b, rwkv_a0, rwkv_a2, rwkv_g2, rwkv_k_k, rwkv_k_a, rwkv_r_k, rwkv_ln_w, rwkv_ln_b, rwkv_proj, w_out, norm2_g, ffn_up, ffn_conv_w, ffn_conv_b, ffn_down, norm_f_g):
    batch, seq_len, d = x.shape
    assert w_in.shape[0] == 1 and d == D_MODEL and seq_len % CHUNK == 0
    x2 = x.reshape(batch * seq_len, d)
    row = lambda t: t.reshape(1, -1)
    for l in range(1):
        w_packed = _pack_w_in(w_in[l]).astype(BF16)
        wa2_f = jnp.pad(gla_wa2_f[l], ((0, 256 - GLA_GATE_RANK), (0, 0)))
        wa2_b = jnp.pad(gla_wa2_b[l], ((GLA_GATE_RANK, 256 - 2 * GLA_GATE_RANK), (0, 0)))
        zl = jnp.zeros((RWKV_AAA_LORA, RWKV_W), F32)
        w2_f = jnp.concatenate([rwkv_w2_f[l], zl], axis=0)
        w2_b = jnp.concatenate([rwkv_w2_b[l], zl], axis=0)
        a2p = jnp.concatenate([jnp.zeros((RWKV_DECAY_LORA, RWKV_W), F32), rwkv_a2[l]], axis=0).astype(BF16)

        p, ps = _inproj(x2, row(norm1_g[l]), w_packed, _pack_mu(rwkv_mu_prev[l]),
                        _pack_mu(rwkv_mu_next[l]), seq_len)

        o_f = _gla(p, ps, wa2_f, row(gla_ba_f[l]), batch, seq_len, reverse=False)
        oa = _gla(p, ps, wa2_b, row(gla_ba_b[l]), batch, seq_len, reverse=True,
                  o_fwd=o_f, norm_g=row(gla_norm_g[l]))

        rw_common = (row(rwkv_a0[l]), a2p, row(rwkv_k_k[l]), row(rwkv_k_a[l]))
        y_f = _rwkv(p, ps, row(rwkv_w0_f[l]), w2_f, *rw_common, batch, seq_len, reverse=False)
        ob = _rwkv(p, ps, row(rwkv_w0_b[l]), w2_b, *rw_common, batch, seq_len, reverse=True,
                   y_fwd=y_f, g2=rwkv_g2[l].astype(BF16), r_k=row(rwkv_r_k[l]),
                   ln_w=row(rwkv_ln_w[l]), ln_b=row(rwkv_ln_b[l]))

        x1 = _merge(x2, oa, ob, p, gla_proj[l].astype(BF16), rwkv_proj[l].astype(BF16),
                    w_out[l].astype(BF16), seq_len)

        wu_p = _pack_ff_cols(ffn_up[l]).astype(BF16)
        cw_p = _pack_ff_cols(ffn_conv_w[l])
        cb_p = _pack_ff_cols(ffn_conv_b[l][None, :])
        wd_p = jnp.pad(ffn_down[l], ((0, FF_PAD - D_FF), (0, 0))).reshape(
            FF_NCHUNK, FF_CHUNK, D_MODEL).astype(BF16)
        x2 = _ffn(x1, row(norm2_g[l]), wu_p, cw_p, cb_p, wd_p, row(norm_f_g), seq_len)
    return x2.reshape(batch, seq_len, d)
```

```python
import functools

import jax
import jax.numpy as jnp
from jax import lax
from jax.experimental import pallas as pl
from jax.experimental.pallas import tpu as pltpu

F32 = jnp.float32
BF16 = jnp.bfloat16

D_MODEL = 1024
GLA_HEADS = 4
GLA_DK = 64
GLA_DV = 128
GLA_QK_W = GLA_HEADS * GLA_DK
GLA_V_W = GLA_HEADS * GLA_DV
GLA_GATE_RANK = 16
GLA_LOGIT_NORM = 16.0
CHUNK = 64
RWKV_HEAD = 64
RWKV_W = 512
RWKV_HEADS = RWKV_W // RWKV_HEAD
RWKV_DECAY_LORA = 64
RWKV_AAA_LORA = 64
RWKV_GATE_LORA = 128
RWKV_GN_EPS = RWKV_HEAD * 1e-5
D_FF = 2752
NORM_EPS = 1e-6
HEAD_NORM_EPS = 1e-5

COL_GA, COL_GB = 0, 1024
COL_QK, COL_V, COL_OG = 2048, 2560, 3072
COL_R, COL_RK, COL_RV = 3584, 4096, 4608
COL_RS, COL_GS = 5120, 5376
P_COLS = 5632
P_MAIN_COLS = COL_RS
HALO = 16

FF_CHUNK = 256
FF_PAD = 2816
FF_NCHUNK = FF_PAD // FF_CHUNK

VMEM_LIMIT = 56 * 1024 * 1024
ROW_TILE = 1024
INPROJ_ROW_TILE = 512
FFN_ROW_TILE = 512
GLA_SEQ_TILE = 2048
RWKV_SEQ_TILE = 256
RWKV_GROUP = 4


def _sigmoid(x):
    return 1.0 / (1.0 + jnp.exp(-x))


def _softplus(z):
    return jnp.maximum(z, 0.0) + jnp.log(1.0 + jnp.exp(-jnp.abs(z)))


def _split3(x):
    hi = x.astype(BF16)
    r1 = x - hi.astype(F32)
    mid = r1.astype(BF16)
    lo = (r1 - mid.astype(F32)).astype(BF16)
    return hi, mid, lo


def _dot(a, b):
    return jnp.dot(a, b, preferred_element_type=F32)


def _dot_nt(a, b):
    return lax.dot_general(a, b, (((1,), (1,)), ((), ())), preferred_element_type=F32)


def _dot_tn(a, b):
    return lax.dot_general(a, b, (((0,), (0,)), ((), ())), preferred_element_type=F32)


def _dot_exact_lhs(sel, x):
    hi = x.astype(BF16)
    lo = (x - hi.astype(F32)).astype(BF16)
    s = sel.astype(BF16)
    return _dot(s, hi) + _dot(s, lo)


def _dot_exact_rhs(x, sel):
    hi = x.astype(BF16)
    lo = (x - hi.astype(F32)).astype(BF16)
    s = sel.astype(BF16)
    return _dot(hi, s) + _dot(lo, s)


def _dot_x3(a, b):
    ah = a.astype(BF16)
    al = (a - ah.astype(F32)).astype(BF16)
    bh = b.astype(BF16)
    bl = (b - bh.astype(F32)).astype(BF16)
    return _dot(ah, bh) + _dot(al, bh) + _dot(ah, bl)


def _rmsnorm(x, g):
    ms = jnp.mean(x * x, axis=-1, keepdims=True)
    return x * lax.rsqrt(ms + NORM_EPS) * g


def _iota2(shape, dim):
    return lax.broadcasted_iota(jnp.int32, shape, dim)


def _div(x, size):
    return lax.shift_right_logical(x, size.bit_length() - 1)


def _mod(x, size):
    return lax.bitwise_and(x, size - 1)


def _chunk_cumsum(x, reverse):
    rows = x.shape[0]
    blk = min(rows, 256)
    row = _iota2((blk, blk), 0)
    col = _iota2((blk, blk), 1)
    same = _div(row, CHUNK) == _div(col, CHUNK)
    tri = jnp.where(same & ((col >= row) if reverse else (col <= row)), 1.0, 0.0)
    parts = [_dot_exact_lhs(tri, x[i:i + blk]) for i in range(0, rows, blk)]
    return parts[0] if len(parts) == 1 else jnp.concatenate(parts, axis=0)


def _blockdiag(z, head_masks):
    return jnp.concatenate([jnp.where(m, z, 0.0) for m in head_masks], axis=0).astype(BF16)


def _inproj_kernel(xp_ref, x_ref, xn_ref, g_ref, w_ref, mup_ref, mun_ref, o_ref, os_ref,
                   h_ref, p_ref, *, tm, tiles_per_seq, first_shift_tile, n_main_tiles, n_sub):
    i = pl.program_id(0)
    n_tiles, _, tn = w_ref.shape
    pos = lax.rem(i, tiles_per_seq)
    keep_prev = jnp.where(pos == 0, 0.0, 1.0)
    keep_next = jnp.where(pos == tiles_per_seq - 1, 0.0, 1.0)
    g = g_ref[...]
    h_ref[0:HALO, :] = _rmsnorm(xp_ref[...] * keep_prev, g).astype(BF16)
    h_ref[HALO:HALO + tm, :] = _rmsnorm(x_ref[...], g).astype(BF16)
    h_ref[HALO + tm:HALO + tm + HALO, :] = _rmsnorm(xn_ref[...] * keep_next, g).astype(BF16)

    for t in range(first_shift_tile):
        o_ref[:, t * tn:(t + 1) * tn] = _dot(h_ref[HALO:HALO + tm, :], w_ref[t]).astype(o_ref.dtype)

    sub = tm // n_sub
    jobs = [(t, s) for t in range(first_shift_tile, n_tiles) for s in range(n_sub)]
    mm = lambda t, s: _dot(h_ref[s * sub:(s + 1) * sub + 2 * HALO, :], w_ref[t])
    p_ref[0] = mm(*jobs[0])
    for n_, (t, s) in enumerate(jobs):
        slot = n_ % 2
        if n_ + 1 < len(jobs):
            p_ref[1 - slot] = mm(*jobs[n_ + 1])
        p = p_ref[slot, HALO:HALO + sub, :]
        pp = p_ref[slot, HALO - 1:HALO - 1 + sub, :]
        pn = p_ref[slot, HALO + 1:HALO + 1 + sub, :]
        val = p + mup_ref[t] * (pp - p) + mun_ref[t] * (pn - p)
        rows = slice(s * sub, (s + 1) * sub)
        if t < n_main_tiles:
            o_ref[rows, t * tn:(t + 1) * tn] = val.astype(o_ref.dtype)
        else:
            os_ref[rows, :] = val


def _inproj(x2, g, w_packed, mu_prev, mu_next, seq_len):
    n = x2.shape[0]
    tm = min(INPROJ_ROW_TILE, seq_len)
    tn = 512
    n_tiles = P_COLS // tn
    tiles_per_seq = seq_len // tm
    hb = tm // HALO
    nhb = n // HALO
    n_sub = max(1, tm // 256)
    sub = tm // n_sub
    kern = functools.partial(_inproj_kernel, tm=tm, tiles_per_seq=tiles_per_seq,
                             first_shift_tile=COL_R // tn, n_main_tiles=P_MAIN_COLS // tn,
                             n_sub=n_sub)
    w_t = jnp.moveaxis(w_packed.reshape(D_MODEL, n_tiles, tn), 1, 0)
    mup_t = jnp.moveaxis(mu_prev.reshape(1, n_tiles, tn), 1, 0)
    mun_t = jnp.moveaxis(mu_next.reshape(1, n_tiles, tn), 1, 0)
    return pl.pallas_call(
        kern,
        grid=(n // tm,),
        in_specs=[
            pl.BlockSpec((HALO, D_MODEL), lambda i: (jnp.maximum(i * hb - 1, 0), 0)),
            pl.BlockSpec((tm, D_MODEL), lambda i: (i, 0)),
            pl.BlockSpec((HALO, D_MODEL), lambda i: (jnp.minimum((i + 1) * hb, nhb - 1), 0)),
            pl.BlockSpec((1, D_MODEL), lambda i: (0, 0)),
            pl.BlockSpec((n_tiles, D_MODEL, tn), lambda i: (0, 0, 0)),
            pl.BlockSpec((n_tiles, 1, tn), lambda i: (0, 0, 0)),
            pl.BlockSpec((n_tiles, 1, tn), lambda i: (0, 0, 0)),
        ],
        out_specs=[pl.BlockSpec((tm, P_MAIN_COLS), lambda i: (i, 0)),
                   pl.BlockSpec((tm, P_COLS - P_MAIN_COLS), lambda i: (i, 0))],
        out_shape=[jax.ShapeDtypeStruct((n, P_MAIN_COLS), BF16),
                   jax.ShapeDtypeStruct((n, P_COLS - P_MAIN_COLS), F32)],
        scratch_shapes=[
            pltpu.VMEM((tm + 2 * HALO, D_MODEL), BF16),
            pltpu.VMEM((2, sub + 2 * HALO, tn), F32),
        ],
        compiler_params=pltpu.CompilerParams(
            dimension_semantics=("arbitrary",),
            vmem_limit_bytes=VMEM_LIMIT),
        name="inproj",
    )(x2, x2, x2, g, w_t, mup_t, mun_t)


def _gla_kernel(*refs, reverse, tb, final):
    if final:
        (qk_ref, v_ref, gs_ref, wa2_ref, ba_ref, of_ref, og_ref, ng_ref,
         o_ref, st_ref, b_ref, acc_ref) = refs
    else:
        qk_ref, v_ref, gs_ref, wa2_ref, ba_ref, o_ref, st_ref, b_ref = refs
    j = pl.program_id(1)
    nc = tb // CHUNK

    @pl.when(j == 0)
    def _():
        st_ref[...] = jnp.zeros_like(st_ref)

    logit = _dot_x3(gs_ref[...], wa2_ref[...]) + ba_ref[...]
    la = (jnp.minimum(logit, 0.0) - jnp.log(1.0 + jnp.exp(-jnp.abs(logit)))) * (1.0 / GLA_LOGIT_NORM)
    b_ref[...] = _chunk_cumsum(la, reverse)

    lane_k = _div(_iota2((CHUNK, GLA_QK_W), 1), GLA_DK)
    lane_v = _div(_iota2((CHUNK, GLA_V_W), 1), GLA_DV)
    k_masks = [lane_k == h for h in range(GLA_HEADS)]
    v_masks = [lane_v == h for h in range(GLA_HEADS)]
    t_idx = _iota2((CHUNK, GLA_QK_W), 0)
    s_idx = _mod(_iota2((CHUNK, GLA_QK_W), 1), CHUNK)
    causal = (s_idx >= t_idx) if reverse else (s_idx <= t_idx)
    st_mask = (_div(_iota2((GLA_V_W, GLA_QK_W), 0), GLA_DV)
               == _div(_iota2((GLA_V_W, GLA_QK_W), 1), GLA_DK))
    i_ref = (CHUNK - 1 - CHUNK // 2) if reverse else CHUNK // 2
    i_last = 0 if reverse else CHUNK - 1

    QI, KD, QE, VB, KI_BD, V_BD, DEC = ([] for _ in range(7))
    for c in range(nc):
        rows = slice(c * CHUNK, (c + 1) * CHUNK)
        b = b_ref[rows, :]
        b_mid = b[i_ref:i_ref + 1, :]
        b_last = b[i_last:i_last + 1, :]
        q = qk_ref[rows, 0:GLA_QK_W].astype(F32) * (GLA_DK ** -0.5)
        k = qk_ref[rows, GLA_QK_W:2 * GLA_QK_W].astype(F32)
        v = v_ref[rows, :].astype(F32)
        QI.append((q * jnp.exp(b - b_mid)).astype(BF16))
        KD.append((k * jnp.exp(b_last - b)).astype(BF16))
        QE.append((q * jnp.exp(b)).astype(BF16))
        VB.append(v.astype(BF16))
        KI_BD.append(_blockdiag(k * jnp.exp(b_mid - b), k_masks))
        V_BD.append(_blockdiag(v, v_masks))
        DEC.append(jnp.exp(b_last))
    A = [jnp.where(causal, _dot_nt(qi, kb), 0.0) for qi, kb in zip(QI, KI_BD)]
    O_IN = [_dot(a.astype(BF16), vb) for a, vb in zip(A, V_BD)]
    KV = [jnp.where(st_mask, _dot_tn(vb, kd), 0.0) for vb, kd in zip(VB, KD)]
    for ci in range(nc):
        c = (nc - 1 - ci) if reverse else ci
        rows = slice(c * CHUNK, (c + 1) * CHUNK)
        st = st_ref[...]
        o = O_IN[c] + _dot_nt(QE[c], st.astype(BF16))
        st_ref[...] = st * DEC[c] + KV[c]
        if final:
            acc_ref[rows, :] = of_ref[rows, :] + o
        else:
            o_ref[rows, :] = o

    if final:
        o = acc_ref[...]
        parts = []
        for h in range(GLA_HEADS):
            oh = o[:, h * GLA_DV:(h + 1) * GLA_DV]
            ms = jnp.mean(oh * oh, axis=-1, keepdims=True)
            parts.append(oh * lax.rsqrt(ms + HEAD_NORM_EPS))
        on = jnp.concatenate(parts, axis=1) * ng_ref[...]
        og = og_ref[...].astype(F32)
        o_ref[...] = on * (og * _sigmoid(og))


def _gla(p, ps, wa2p, ba, batch, seq_len, reverse, o_fwd=None, norm_g=None):
    final = o_fwd is not None
    tb = min(GLA_SEQ_TILE, seq_len)
    nt = seq_len // tb
    n = batch * seq_len

    def rowblk(b, j):
        return b * nt + ((nt - 1 - j) if reverse else j)

    in_specs = [
        pl.BlockSpec((tb, 512), lambda b, j: (rowblk(b, j), COL_QK // 512)),
        pl.BlockSpec((tb, 512), lambda b, j: (rowblk(b, j), COL_V // 512)),
        pl.BlockSpec((tb, 256), lambda b, j: (rowblk(b, j), (COL_GS - P_MAIN_COLS) // 256)),
        pl.BlockSpec((256, GLA_QK_W), lambda b, j: (0, 0)),
        pl.BlockSpec((1, GLA_QK_W), lambda b, j: (0, 0)),
    ]
    args = [p, p, ps, wa2p, ba]
    scratch = [pltpu.VMEM((GLA_V_W, GLA_QK_W), F32), pltpu.VMEM((tb, GLA_QK_W), F32)]
    if final:
        in_specs += [
            pl.BlockSpec((tb, GLA_V_W), lambda b, j: (rowblk(b, j), 0)),
            pl.BlockSpec((tb, 512), lambda b, j: (rowblk(b, j), COL_OG // 512)),
            pl.BlockSpec((1, GLA_V_W), lambda b, j: (0, 0)),
        ]
        args += [o_fwd, p, norm_g]
        scratch.append(pltpu.VMEM((tb, GLA_V_W), F32))
    kern = functools.partial(_gla_kernel, reverse=reverse, tb=tb, final=final)
    return pl.pallas_call(
        kern,
        grid=(batch, nt),
        in_specs=in_specs,
        out_specs=pl.BlockSpec((tb, GLA_V_W), lambda b, j: (rowblk(b, j), 0)),
        out_shape=jax.ShapeDtypeStruct((n, GLA_V_W), F32),
        scratch_shapes=scratch,
        compiler_params=pltpu.CompilerParams(
            dimension_semantics=("arbitrary", "arbitrary"),
            vmem_limit_bytes=VMEM_LIMIT),
        name="gla_bwd" if reverse else "gla_fwd",
    )(*args)


def _rwkv_kernel(*refs, reverse, tb, final):
    if final:
        (r_ref, k_ref, v_ref, rs_ref, w0_ref, w2_ref, a0_ref, a2_ref, kk_ref, ka_ref,
         yf_ref, g2_ref, rk_ref, lnw_ref, lnb_ref,
         o_ref, p_ref, cs_s, lw_s, kk_s, ba_s, k2_s, reff_s, oloc_s, gm_s, hm_s, y_s) = refs
    else:
        (r_ref, k_ref, v_ref, rs_ref, w0_ref, w2_ref, a0_ref, a2_ref, kk_ref, ka_ref,
         o_ref, p_ref, cs_s, lw_s, kk_s, ba_s, k2_s, reff_s, oloc_s, gm_s, hm_s) = refs
        y_s = o_ref
    j = pl.program_id(1)
    nc = tb // CHUNK
    G = RWKV_GROUP
    L = G * RWKV_HEAD
    ngrp = RWKV_W // L

    @pl.when(j == 0)
    def _():
        p_ref[...] = jnp.zeros_like(p_ref)

    r = r_ref[...].astype(F32)
    k = k_ref[...].astype(F32)
    lora_in = rs_ref[:, 0:RWKV_DECAY_LORA + RWKV_AAA_LORA]
    w_pre = w0_ref[...] + _dot_x3(jnp.tanh(lora_in), w2_ref[...])
    logw = -jnp.exp(-_softplus(-w_pre) - 0.5)
    a_lr = _sigmoid(a0_ref[...] + _dot(lora_in.astype(BF16), a2_ref[...]))
    hr = _div(_iota2((RWKV_W, RWKV_W), 0), RWKV_HEAD)
    hc = _div(_iota2((RWKV_W, RWKV_W), 1), RWKV_HEAD)
    head_ones = jnp.where(hr == hc, 1.0, 0.0).astype(BF16)
    head_sum = lambda t: _dot(t.astype(BF16), head_ones)
    kk = k * kk_ref[...]
    ssq = head_sum(kk * kk)
    kk = kk / jnp.maximum(jnp.sqrt(ssq), 1e-12)
    k2 = k * (1.0 + (a_lr - 1.0) * ka_ref[...])
    if final:
        bonus = head_sum(r * k2 * rk_ref[...]) * v_ref[...].astype(F32)
        gl = rs_ref[:, RWKV_DECAY_LORA + RWKV_AAA_LORA:]
        g_gate = _dot(_sigmoid(gl).astype(BF16), g2_ref[...])

    cs_s[...] = _chunk_cumsum(logw, reverse)
    lw_s[...] = logw
    kk_s[...] = kk
    ba_s[...] = kk * a_lr
    k2_s[...] = k2
    i_last = 0 if reverse else CHUNK - 1

    lane_h = _div(_iota2((CHUNK, L), 1), RWKV_HEAD)
    h_masks = [lane_h == h for h in range(G)]
    t_idx = _iota2((CHUNK, L), 0)
    s_idx = _mod(_iota2((CHUNK, L), 1), CHUNK)
    strict = (s_idx > t_idx) if reverse else (s_idx < t_idx)
    incl = (s_idx >= t_idx) if reverse else (s_idx <= t_idx)
    eye_p = jnp.where(s_idx == t_idx, 1.0, 0.0)
    bd_mask = _div(_iota2((L, L), 0), RWKV_HEAD) == _div(_iota2((L, L), 1), RWKV_HEAD)
    eye_l = jnp.where(_iota2((L, L), 0) == _iota2((L, L), 1), 1.0, 0.0)

    def blockdiag(z):
        return _blockdiag(z, h_masks)

    chains = [(c, gi) for c in range(nc) for gi in range(ngrp)]
    rows_of = lambda c: slice(c * CHUNK, (c + 1) * CHUNK)
    lanes_of = lambda gi: slice(gi * L, (gi + 1) * L)
    AT, BT, KT, RT, BD, KD, GL, V = ([] for _ in range(8))
    for c in range(nc):
        rows = rows_of(c)
        cs = cs_s[rows, :]
        cs_tot = cs[i_last:i_last + 1, :]
        g_inv = jnp.exp(-cs)
        g_rem = jnp.exp(cs_tot - cs)
        at_c = (-kk_s[rows, :]) * jnp.exp(cs - lw_s[rows, :])
        bt_c = ba_s[rows, :] * g_inv
        kt_c = k2_s[rows, :] * g_inv
        rt_c = r_ref[rows, :].astype(F32) * jnp.exp(cs)
        bd_c = ba_s[rows, :] * g_rem
        kd_c = k2_s[rows, :] * g_rem
        gl_c = jnp.exp(cs_tot)
        for gi in range(ngrp):
            lanes = lanes_of(gi)
            AT.append(at_c[:, lanes])
            BT.append(bt_c[:, lanes])
            KT.append(kt_c[:, lanes])
            RT.append(rt_c[:, lanes])
            BD.append(bd_c[:, lanes])
            KD.append(kd_c[:, lanes])
            GL.append(gl_c[:, lanes])
            V.append(v_ref[rows, lanes].astype(F32))
    bf = lambda t: t.astype(BF16)
    cat0 = lambda a, b: jnp.concatenate([a, b], axis=0)
    cat1 = lambda a, b: jnp.concatenate([a, b], axis=1)
    X = [bf(cat0(a, r_)) for a, r_ in zip(AT, RT)]
    G1 = [_dot_nt(x, blockdiag(b)) for x, b in zip(X, BT)]
    G2 = [_dot_nt(x, blockdiag(k_)) for x, k_ in zip(X, KT)]
    LAB = [jnp.where(strict, g[0:CHUNK], 0.0) for g in G1]
    MRB = [jnp.where(incl, g[CHUNK:2 * CHUNK], 0.0) for g in G1]
    LAK = [jnp.where(strict, g[0:CHUNK], 0.0) for g in G2]
    MRK = [jnp.where(incl, g[CHUNK:2 * CHUNK], 0.0) for g in G2]
    TM = [eye_p + l for l in LAB]
    XP = [_dot(bf(l), blockdiag(l)) for l in LAB]
    for _ in range(4):
        Y = [_dot(bf(cat0(t, x)), blockdiag(x)) for t, x in zip(TM, XP)]
        TM = [t + y[0:CHUNK] for t, y in zip(TM, Y)]
        XP = [y[CHUNK:2 * CHUNK] for y in Y]
    TM = [t + _dot(bf(t), blockdiag(x)) for t, x in zip(TM, XP)]
    VBD = [blockdiag(v) for v in V]
    LMV = [_dot(bf(cat0(l, m)), vb) for l, m, vb in zip(LAK, MRK, VBD)]
    LAKV = [t[0:CHUNK] for t in LMV]
    MV = [t[CHUNK:2 * CHUNK] for t in LMV]
    WU =[_dot(bf(t), cat1(blockdiag(a), blockdiag(lv))) for t, a, lv in zip(TM, AT, LAKV)]
    WA = [wu[:, 0:L] for wu in WU]
    UV = [wu[:, L:2 * L] for wu in WU]
    RO = [_dot(bf(m), cat1(blockdiag(wa), blockdiag(uv))) for m, wa, uv in zip(MRB, WA, UV)]
    PW = 2 * RWKV_HEAD
    zpair = jnp.zeros((PW, PW), F32)

    def pair_tn(bd, wa, uv):
        blocks = []
        for q in range(L // PW):
            pl_ = slice(q * PW, (q + 1) * PW)
            blocks.append(_dot_tn(bf(bd[:, pl_]), bf(cat1(wa[:, pl_], uv[:, pl_]))))
        rows_g = [cat1(*[blocks[q][:, 0:PW] if q2 == q else zpair for q2 in range(L // PW)])
                  for q in range(L // PW)]
        rows_h = [cat1(*[blocks[q][:, PW:2 * PW] if q2 == q else zpair for q2 in range(L // PW)])
                  for q in range(L // PW)]
        return cat1(jnp.concatenate(rows_g, axis=0), jnp.concatenate(rows_h, axis=0))

    if final:
        GH = [pair_tn(bd, wa, uv) for bd, wa, uv in zip(BD, WA, UV)]
    else:
        GH = [_dot_tn(bf(bd), bf(cat1(wa, uv))) for bd, wa, uv in zip(BD, WA, UV)]
    KV = [_dot_tn(bf(kd), bf(v)) for kd, v in zip(KD, V)]
    for n_, (c, gi) in enumerate(chains):
        rows, lanes = rows_of(c), lanes_of(gi)
        reff_s[rows, lanes] = RT[n_] + RO[n_][:, 0:L]
        oloc_s[rows, lanes] = RO[n_][:, L:2 * L] + MV[n_]
        gm_s[c, gi] = bf(eye_l * GL[n_] + jnp.where(bd_mask, GH[n_][:, 0:L], 0.0))
        hm_s[c, gi] = jnp.where(bd_mask, GH[n_][:, L:2 * L] + KV[n_], 0.0)

    for ci in range(nc):
        c = (nc - 1 - ci) if reverse else ci
        rows = slice(c * CHUNK, (c + 1) * CHUNK)
        for gi in range(ngrp):
            lanes = slice(gi * L, (gi + 1) * L)
            pst = p_ref[gi].astype(BF16)
            lhs = jnp.concatenate([reff_s[rows, lanes].astype(BF16), gm_s[c, gi]], axis=0)
            yp = _dot(lhs, pst)
            y_s[rows, lanes] = yp[0:CHUNK] + oloc_s[rows, lanes]
            p_ref[gi] = yp[CHUNK:CHUNK + L] + hm_s[c, gi]

    if final:
        y = y_s[...] + yf_ref[...]
        inv_n = 1.0 / RWKV_HEAD
        mu = head_sum(y) * inv_n
        yc = y - mu
        var = head_sum(yc * yc) * inv_n
        yn = yc * lax.rsqrt(var + RWKV_GN_EPS) * lnw_ref[...] + lnb_ref[...]
        o_ref[...] = (yn + bonus) * g_gate


def _rwkv(p, ps, w0, w2p, a0, a2p, k_k, k_a, batch, seq_len, reverse,
          y_fwd=None, g2=None, r_k=None, ln_w=None, ln_b=None):
    final = y_fwd is not None
    tb = min(RWKV_SEQ_TILE, seq_len)
    nt = seq_len // tb
    n = batch * seq_len
    L = RWKV_GROUP * RWKV_HEAD
    ngrp = RWKV_W // L
    lora = RWKV_DECAY_LORA + RWKV_AAA_LORA

    def rowblk(b, j):
        return b * nt + ((nt - 1 - j) if reverse else j)

    vec = pl.BlockSpec((1, RWKV_W), lambda b, j: (0, 0))
    in_specs = [
        pl.BlockSpec((tb, 512), lambda b, j: (rowblk(b, j), COL_R // 512)),
        pl.BlockSpec((tb, 512), lambda b, j: (rowblk(b, j), COL_RK // 512)),
        pl.BlockSpec((tb, 512), lambda b, j: (rowblk(b, j), COL_RV // 512)),
        pl.BlockSpec((tb, 256), lambda b, j: (rowblk(b, j), (COL_RS - P_MAIN_COLS) // 256)),
        vec,
        pl.BlockSpec((lora, RWKV_W), lambda b, j: (0, 0)),
        vec,
        pl.BlockSpec((lora, RWKV_W), lambda b, j: (0, 0)),
        vec, vec,
    ]
    args = [p, p, p, ps, w0, w2p, a0, a2p, k_k, k_a]
    nc = tb // CHUNK
    scratch = ([pltpu.VMEM((ngrp, L, L), F32)] + [pltpu.VMEM((tb, RWKV_W), F32)] * 7
               + [pltpu.VMEM((nc, ngrp, L, L), BF16), pltpu.VMEM((nc, ngrp, L, L), F32)])
    if final:
        in_specs += [
            pl.BlockSpec((tb, RWKV_W), lambda b, j: (rowblk(b, j), 0)),
            pl.BlockSpec((RWKV_GATE_LORA, RWKV_W), lambda b, j: (0, 0)),
            vec, vec, vec,
        ]
        args += [y_fwd, g2, r_k, ln_w, ln_b]
        scratch.append(pltpu.VMEM((tb, RWKV_W), F32))
    kern = functools.partial(_rwkv_kernel, reverse=reverse, tb=tb, final=final)
    return pl.pallas_call(
        kern,
        grid=(batch, nt),
        in_specs=in_specs,
        out_specs=pl.BlockSpec((tb, RWKV_W), lambda b, j: (rowblk(b, j), 0)),
        out_shape=jax.ShapeDtypeStruct((n, RWKV_W), F32),
        scratch_shapes=scratch,
        compiler_params=pltpu.CompilerParams(
            dimension_semantics=("arbitrary", "arbitrary"),
            vmem_limit_bytes=VMEM_LIMIT),
        name="rwkv_bwd" if reverse else "rwkv_fwd",
    )(*args)


def _merge_kernel(x_ref, oa_ref, ob_ref, ga_ref, gb_ref, wa_ref, wb_ref, wo_ref, o_ref):
    ya = _dot(oa_ref[...].astype(BF16), wa_ref[...])
    yb = _dot(ob_ref[...].astype(BF16), wb_ref[...])
    merged = (_sigmoid(ga_ref[...].astype(F32)) * ya
              + _sigmoid(gb_ref[...].astype(F32)) * yb)
    o_ref[...] = x_ref[...] + _dot(merged.astype(BF16), wo_ref[...])


def _merge(x2, oa, ob, p, gla_proj, rwkv_proj, w_out, seq_len):
    n = x2.shape[0]
    tm = min(ROW_TILE, seq_len)
    return pl.pallas_call(
        _merge_kernel,
        grid=(n // tm,),
        in_specs=[
            pl.BlockSpec((tm, D_MODEL), lambda i: (i, 0)),
            pl.BlockSpec((tm, GLA_V_W), lambda i: (i, 0)),
            pl.BlockSpec((tm, RWKV_W), lambda i: (i, 0)),
            pl.BlockSpec((tm, D_MODEL), lambda i: (i, COL_GA // D_MODEL)),
            pl.BlockSpec((tm, D_MODEL), lambda i: (i, COL_GB // D_MODEL)),
            pl.BlockSpec((GLA_V_W, D_MODEL), lambda i: (0, 0)),
            pl.BlockSpec((RWKV_W, D_MODEL), lambda i: (0, 0)),
            pl.BlockSpec((D_MODEL, D_MODEL), lambda i: (0, 0)),
        ],
        out_specs=pl.BlockSpec((tm, D_MODEL), lambda i: (i, 0)),
        out_shape=jax.ShapeDtypeStruct((n, D_MODEL), F32),
        compiler_params=pltpu.CompilerParams(
            dimension_semantics=("arbitrary",),
            vmem_limit_bytes=VMEM_LIMIT),
        name="merge",
    )(x2, oa, ob, p, p, gla_proj, rwkv_proj, w_out)


def _ffn_kernel(xp_ref, x_ref, xn_ref, g2_ref, wu_ref, cw_ref, cb_ref, wd_ref, gf_ref, o_ref,
                h_ref, u_ref, *, tm, tiles_per_seq):
    i = pl.program_id(0)
    pos = lax.rem(i, tiles_per_seq)
    keep_prev = jnp.where(pos == 0, 0.0, 1.0)
    keep_next = jnp.where(pos == tiles_per_seq - 1, 0.0, 1.0)
    g2 = g2_ref[...]
    h_ref[0:HALO, :] = _rmsnorm(xp_ref[...] * keep_prev, g2).astype(BF16)
    h_ref[HALO:HALO + tm, :] = _rmsnorm(x_ref[...], g2).astype(BF16)
    h_ref[HALO + tm:HALO + tm + HALO, :] = _rmsnorm(xn_ref[...] * keep_next, g2).astype(BF16)

    up = lambda c: _dot(h_ref[...], wu_ref[c])
    u_ref[0] = up(0)
    for c in range(FF_NCHUNK):
        slot = c % 2
        if c + 1 < FF_NCHUNK:
            u_ref[1 - slot] = up(c + 1)
        cw = cw_ref[c]
        uc = (cw[0:1, :] * u_ref[slot, HALO - 1:HALO - 1 + tm, :]
              + cw[1:2, :] * u_ref[slot, HALO:HALO + tm, :]
              + cw[2:3, :] * u_ref[slot, HALO + 1:HALO + 1 + tm, :]
              + cb_ref[c])
        ug = uc[:, 0:FF_CHUNK]
        act = (ug * _sigmoid(ug)) * uc[:, FF_CHUNK:2 * FF_CHUNK]
        down = _dot(act.astype(BF16), wd_ref[c])
        acc = down if c == 0 else acc + down
    o_ref[...] = _rmsnorm(x_ref[...] + acc, gf_ref[...])


def _ffn(x1, g2, wu_p, cw_p, cb_p, wd_p, gf, seq_len):
    n = x1.shape[0]
    tm = min(FFN_ROW_TILE, seq_len)
    tiles_per_seq = seq_len // tm
    hb = tm // HALO
    nhb = n // HALO
    kern = functools.partial(_ffn_kernel, tm=tm, tiles_per_seq=tiles_per_seq)
    return pl.pallas_call(
        kern,
        grid=(n // tm,),
        in_specs=[
            pl.BlockSpec((HALO, D_MODEL), lambda i: (jnp.maximum(i * hb - 1, 0), 0)),
            pl.BlockSpec((tm, D_MODEL), lambda i: (i, 0)),
            pl.BlockSpec((HALO, D_MODEL), lambda i: (jnp.minimum((i + 1) * hb, nhb - 1), 0)),
            pl.BlockSpec((1, D_MODEL), lambda i: (0, 0)),
            pl.BlockSpec((FF_NCHUNK, D_MODEL, 2 * FF_CHUNK), lambda i: (0, 0, 0)),
            pl.BlockSpec((FF_NCHUNK, 3, 2 * FF_CHUNK), lambda i: (0, 0, 0)),
            pl.BlockSpec((FF_NCHUNK, 1, 2 * FF_CHUNK), lambda i: (0, 0, 0)),
            pl.BlockSpec((FF_NCHUNK, FF_CHUNK, D_MODEL), lambda i: (0, 0, 0)),
            pl.BlockSpec((1, D_MODEL), lambda i: (0, 0)),
        ],
        out_specs=pl.BlockSpec((tm, D_MODEL), lambda i: (i, 0)),
        out_shape=jax.ShapeDtypeStruct((n, D_MODEL), F32),
        scratch_shapes=[
            pltpu.VMEM((tm + 2 * HALO, D_MODEL), BF16),
            pltpu.VMEM((2, tm + 2 * HALO, 2 * FF_CHUNK), F32),
        ],
        compiler_params=pltpu.CompilerParams(
            dimension_semantics=("arbitrary",),
            vmem_limit_bytes=VMEM_LIMIT),
        name="ffn",
    )(x1, x1, x1, g2, wu_p, cw_p, cb_p, wd_p, gf)


def _pack_w_in(w):
    gla_w = 2 * GLA_QK_W + 2 * GLA_V_W + 2 * GLA_GATE_RANK
    r0 = gla_w
    g0 = r0 + 3 * RWKV_W + RWKV_DECAY_LORA + RWKV_AAA_LORA + RWKV_GATE_LORA
    pad = jnp.zeros((w.shape[0], 256 - 2 * GLA_GATE_RANK), w.dtype)
    return jnp.concatenate([
        w[:, g0:g0 + 2 * D_MODEL],
        w[:, 0:2 * GLA_QK_W + 2 * GLA_V_W],
        w[:, r0:r0 + 3 * RWKV_W + 256],
        w[:, 2 * GLA_QK_W + 2 * GLA_V_W:gla_w],
        pad], axis=1)


def _pack_mu(mu):
    z = jnp.zeros((COL_R,), mu.dtype)
    z2 = jnp.zeros((P_COLS - COL_GS,), mu.dtype)
    return jnp.concatenate([z, mu, z2])[None, :]


def _pack_ff_cols(t):
    pad = [(0, 0)] * (t.ndim - 1) + [(0, FF_PAD - D_FF)]
    g = jnp.pad(t[..., :D_FF], pad)
    v = jnp.pad(t[..., D_FF:], pad)
    lead = t.shape[:-1]
    g = g.reshape(lead + (FF_NCHUNK, FF_CHUNK))
    v = v.reshape(lead + (FF_NCHUNK, FF_CHUNK))
    gv = jnp.concatenate([g, v], axis=-1)
    return jnp.moveaxis(gv, -2, 0)


def kernel(x, norm1_g, w_in, gla_wa2_f, gla_ba_f, gla_wa2_b, gla_ba_b, gla_norm_g, gla_proj, rwkv_mu_prev, rwkv_mu_next, rwkv_w0_f, rwkv_w2_f, rwkv_w0_b, rwkv_w2_b, rwkv_a0, rwkv_a2, rwkv_g2, rwkv_k_k, rwkv_k_a, rwkv_r_k, rwkv_ln_w, rwkv_ln_b, rwkv_proj, w_out, norm2_g, ffn_up, ffn_conv_w, ffn_conv_b, ffn_down, norm_f_g):
    batch, seq_len, d = x.shape
    assert w_in.shape[0] == 1 and d == D_MODEL and seq_len % CHUNK == 0
    x2 = x.reshape(batch * seq_len, d)
    row = lambda t: t.reshape(1, -1)
    for l in range(1):
        w_packed = _pack_w_in(w_in[l]).astype(BF16)
        wa2_f = jnp.pad(gla_wa2_f[l], ((0, 256 - GLA_GATE_RANK), (0, 0)))
        wa2_b = jnp.pad(gla_wa2_b[l], ((GLA_GATE_RANK, 256 - 2 * GLA_GATE_RANK), (0, 0)))
        zl = jnp.zeros((RWKV_AAA_LORA, RWKV_W), F32)
        w2_f = jnp.concatenate([rwkv_w2_f[l], zl], axis=0)
        w2_b = jnp.concatenate([rwkv_w2_b[l], zl], axis=0)
        a2p = jnp.concatenate([jnp.zeros((RWKV_DECAY_LORA, RWKV_W), F32), rwkv_a2[l]], axis=0).astype(BF16)

        p, ps = _inproj(x2, row(norm1_g[l]), w_packed, _pack_mu(rwkv_mu_prev[l]),
                        _pack_mu(rwkv_mu_next[l]), seq_len)

        o_f = _gla(p, ps, wa2_f, row(gla_ba_f[l]), batch, seq_len, reverse=False)
        oa = _gla(p, ps, wa2_b, row(gla_ba_b[l]), batch, seq_len, reverse=True,
                  o_fwd=o_f, norm_g=row(gla_norm_g[l]))

        rw_common = (row(rwkv_a0[l]), a2p, row(rwkv_k_k[l]), row(rwkv_k_a[l]))
        y_f = _rwkv(p, ps, row(rwkv_w0_f[l]), w2_f, *rw_common, batch, seq_len, reverse=False)
        ob = _rwkv(p, ps, row(rwkv_w0_b[l]), w2_b, *rw_common, batch, seq_len, reverse=True,
                   y_fwd=y_f, g2=rwkv_g2[l].astype(BF16), r_k=row(rwkv_r_k[l]),
                   ln_w=row(rwkv_ln_w[l]), ln_b=row(rwkv_ln_b[l]))

        x1 = _merge(x2, oa, ob, p, gla_proj[l].astype(BF16), rwkv_proj[l].astype(BF16),
                    w_out[l].astype(BF16), seq_len)

        wu_p = _pack_ff_cols(ffn_up[l]).astype(BF16)
        cw_p = _pack_ff_cols(ffn_conv_w[l])
        cb_p = _pack_ff_cols(ffn_conv_b[l][None, :])
        wd_p = jnp.pad(ffn_down[l], ((0, FF_PAD - D_FF), (0, 0))).reshape(
            FF_NCHUNK, FF_CHUNK, D_MODEL).astype(BF16)
        x2 = _ffn(x1, row(norm2_g[l]), wu_p, cw_p, cb_p, wd_p, row(norm_f_g), seq_len)
    return x2.reshape(batch, seq_len, d)
```

```python
import functools

import jax
import jax.numpy as jnp
from jax import lax
from jax.experimental import pallas as pl
from jax.experimental.pallas import tpu as pltpu

F32 = jnp.float32
BF16 = jnp.bfloat16

D_MODEL = 1024
GLA_HEADS = 4
GLA_DK = 64
GLA_DV = 128
GLA_QK_W = GLA_HEADS * GLA_DK
GLA_V_W = GLA_HEADS * GLA_DV
GLA_GATE_RANK = 16
GLA_LOGIT_NORM = 16.0
CHUNK = 64
RWKV_HEAD = 64
RWKV_W = 512
RWKV_HEADS = RWKV_W // RWKV_HEAD
RWKV_DECAY_LORA = 64
RWKV_AAA_LORA = 64
RWKV_GATE_LORA = 128
RWKV_GN_EPS = RWKV_HEAD * 1e-5
D_FF = 2752
NORM_EPS = 1e-6
HEAD_NORM_EPS = 1e-5

COL_GA, COL_GB = 0, 1024
COL_QK, COL_V, COL_OG = 2048, 2560, 3072
COL_R, COL_RK, COL_RV = 3584, 4096, 4608
COL_RS, COL_GS = 5120, 5376
P_COLS = 5632
P_MAIN_COLS = COL_RS
HALO = 16

FF_CHUNK = 256
FF_PAD = 2816
FF_NCHUNK = FF_PAD // FF_CHUNK

VMEM_LIMIT = 56 * 1024 * 1024
ROW_TILE = 1024
INPROJ_ROW_TILE = 512
FFN_ROW_TILE = 1024
GLA_SEQ_TILE = 2048
RWKV_SEQ_TILE = 256
RWKV_GROUP = 4


def _sigmoid(x):
    return 1.0 / (1.0 + jnp.exp(-x))


def _softplus(z):
    return jnp.maximum(z, 0.0) + jnp.log(1.0 + jnp.exp(-jnp.abs(z)))


def _split3(x):
    hi = x.astype(BF16)
    r1 = x - hi.astype(F32)
    mid = r1.astype(BF16)
    lo = (r1 - mid.astype(F32)).astype(BF16)
    return hi, mid, lo


def _dot(a, b):
    return jnp.dot(a, b, preferred_element_type=F32)


def _dot_nt(a, b):
    return lax.dot_general(a, b, (((1,), (1,)), ((), ())), preferred_element_type=F32)


def _dot_tn(a, b):
    return lax.dot_general(a, b, (((0,), (0,)), ((), ())), preferred_element_type=F32)


def _dot_exact_lhs(sel, x):
    hi = x.astype(BF16)
    lo = (x - hi.astype(F32)).astype(BF16)
    s = sel.astype(BF16)
    return _dot(s, hi) + _dot(s, lo)


def _dot_exact_rhs(x, sel):
    hi = x.astype(BF16)
    lo = (x - hi.astype(F32)).astype(BF16)
    s = sel.astype(BF16)
    return _dot(hi, s) + _dot(lo, s)


def _dot_x3(a, b):
    ah = a.astype(BF16)
    al = (a - ah.astype(F32)).astype(BF16)
    bh = b.astype(BF16)
    bl = (b - bh.astype(F32)).astype(BF16)
    return _dot(ah, bh) + _dot(al, bh) + _dot(ah, bl)


def _rmsnorm(x, g):
    ms = jnp.mean(x * x, axis=-1, keepdims=True)
    return x * lax.rsqrt(ms + NORM_EPS) * g


def _iota2(shape, dim):
    return lax.broadcasted_iota(jnp.int32, shape, dim)


def _div(x, size):
    return lax.shift_right_logical(x, size.bit_length() - 1)


def _mod(x, size):
    return lax.bitwise_and(x, size - 1)


def _chunk_cumsum(x, reverse):
    rows = x.shape[0]
    blk = min(rows, 256)
    row = _iota2((blk, blk), 0)
    col = _iota2((blk, blk), 1)
    same = _div(row, CHUNK) == _div(col, CHUNK)
    tri = jnp.where(same & ((col >= row) if reverse else (col <= row)), 1.0, 0.0)
    parts = [_dot_exact_lhs(tri, x[i:i + blk]) for i in range(0, rows, blk)]
    return parts[0] if len(parts) == 1 else jnp.concatenate(parts, axis=0)


def _blockdiag(z, head_masks):
    return jnp.concatenate([jnp.where(m, z, 0.0) for m in head_masks], axis=0).astype(BF16)


def _inproj_kernel(xp_ref, x_ref, xn_ref, g_ref, w_ref, mup_ref, mun_ref, o_ref, os_ref,
                   h_ref, p_ref, *, tm, tiles_per_seq, first_shift_tile, n_main_tiles, n_sub):
    i = pl.program_id(0)
    n_tiles, _, tn = w_ref.shape
    pos = lax.rem(i, tiles_per_seq)
    keep_prev = jnp.where(pos == 0, 0.0, 1.0)
    keep_next = jnp.where(pos == tiles_per_seq - 1, 0.0, 1.0)
    g = g_ref[...]
    h_ref[0:HALO, :] = _rmsnorm(xp_ref[...] * keep_prev, g).astype(BF16)
    h_ref[HALO:HALO + tm, :] = _rmsnorm(x_ref[...], g).astype(BF16)
    h_ref[HALO + tm:HALO + tm + HALO, :] = _rmsnorm(xn_ref[...] * keep_next, g).astype(BF16)

    for t in range(first_shift_tile):
        o_ref[:, t * tn:(t + 1) * tn] = _dot(h_ref[HALO:HALO + tm, :], w_ref[t]).astype(o_ref.dtype)

    sub = tm // n_sub
    jobs = [(t, s) for t in range(first_shift_tile, n_tiles) for s in range(n_sub)]
    mm = lambda t, s: _dot(h_ref[s * sub:(s + 1) * sub + 2 * HALO, :], w_ref[t])
    p_ref[0] = mm(*jobs[0])
    for n_, (t, s) in enumerate(jobs):
        slot = n_ % 2
        if n_ + 1 < len(jobs):
            p_ref[1 - slot] = mm(*jobs[n_ + 1])
        p = p_ref[slot, HALO:HALO + sub, :]
        pp = p_ref[slot, HALO - 1:HALO - 1 + sub, :]
        pn = p_ref[slot, HALO + 1:HALO + 1 + sub, :]
        val = p + mup_ref[t] * (pp - p) + mun_ref[t] * (pn - p)
        rows = slice(s * sub, (s + 1) * sub)
        if t < n_main_tiles:
            o_ref[rows, t * tn:(t + 1) * tn] = val.astype(o_ref.dtype)
        else:
            os_ref[rows, :] = val


def _inproj(x2, g, w_packed, mu_prev, mu_next, seq_len):
    n = x2.shape[0]
    tm = min(INPROJ_ROW_TILE, seq_len)
    tn = 512
    n_tiles = P_COLS // tn
    tiles_per_seq = seq_len // tm
    hb = tm // HALO
    nhb = n // HALO
    n_sub = max(1, tm // 256)
    sub = tm // n_sub
    kern = functools.partial(_inproj_kernel, tm=tm, tiles_per_seq=tiles_per_seq,
                             first_shift_tile=COL_R // tn, n_main_tiles=P_MAIN_COLS // tn,
                             n_sub=n_sub)
    w_t = jnp.moveaxis(w_packed.reshape(D_MODEL, n_tiles, tn), 1, 0)
    mup_t = jnp.moveaxis(mu_prev.reshape(1, n_tiles, tn), 1, 0)
    mun_t = jnp.moveaxis(mu_next.reshape(1, n_tiles, tn), 1, 0)
    return pl.pallas_call(
        kern,
        grid=(n // tm,),
        in_specs=[
            pl.BlockSpec((HALO, D_MODEL), lambda i: (jnp.maximum(i * hb - 1, 0), 0)),
            pl.BlockSpec((tm, D_MODEL), lambda i: (i, 0)),
            pl.BlockSpec((HALO, D_MODEL), lambda i: (jnp.minimum((i + 1) * hb, nhb - 1), 0)),
            pl.BlockSpec((1, D_MODEL), lambda i: (0, 0)),
            pl.BlockSpec((n_tiles, D_MODEL, tn), lambda i: (0, 0, 0)),
            pl.BlockSpec((n_tiles, 1, tn), lambda i: (0, 0, 0)),
            pl.BlockSpec((n_tiles, 1, tn), lambda i: (0, 0, 0)),
        ],
        out_specs=[pl.BlockSpec((tm, P_MAIN_COLS), lambda i: (i, 0)),
                   pl.BlockSpec((tm, P_COLS - P_MAIN_COLS), lambda i: (i, 0))],
        out_shape=[jax.ShapeDtypeStruct((n, P_MAIN_COLS), BF16),
                   jax.ShapeDtypeStruct((n, P_COLS - P_MAIN_COLS), F32)],
        scratch_shapes=[
            pltpu.VMEM((tm + 2 * HALO, D_MODEL), BF16),
            pltpu.VMEM((2, sub + 2 * HALO, tn), F32),
        ],
        compiler_params=pltpu.CompilerParams(
            dimension_semantics=("arbitrary",),
            vmem_limit_bytes=VMEM_LIMIT),
        name="inproj",
    )(x2, x2, x2, g, w_t, mup_t, mun_t)


def _gla_kernel(*refs, reverse, tb, final):
    if final:
        (qk_ref, v_ref, gs_ref, wa2_ref, ba_ref, of_ref, og_ref, ng_ref,
         o_ref, st_ref, b_ref, acc_ref) = refs
    else:
        qk_ref, v_ref, gs_ref, wa2_ref, ba_ref, o_ref, st_ref, b_ref = refs
    j = pl.program_id(1)
    nc = tb // CHUNK

    @pl.when(j == 0)
    def _():
        st_ref[...] = jnp.zeros_like(st_ref)

    logit = _dot_x3(gs_ref[...], wa2_ref[...]) + ba_ref[...]
    la = (jnp.minimum(logit, 0.0) - jnp.log(1.0 + jnp.exp(-jnp.abs(logit)))) * (1.0 / GLA_LOGIT_NORM)
    b_ref[...] = _chunk_cumsum(la, reverse)

    lane_k = _div(_iota2((CHUNK, GLA_QK_W), 1), GLA_DK)
    lane_v = _div(_iota2((CHUNK, GLA_V_W), 1), GLA_DV)
    k_masks = [lane_k == h for h in range(GLA_HEADS)]
    v_masks = [lane_v == h for h in range(GLA_HEADS)]
    t_idx = _iota2((CHUNK, GLA_QK_W), 0)
    s_idx = _mod(_iota2((CHUNK, GLA_QK_W), 1), CHUNK)
    causal = (s_idx >= t_idx) if reverse else (s_idx <= t_idx)
    st_mask = (_div(_iota2((GLA_V_W, GLA_QK_W), 0), GLA_DV)
               == _div(_iota2((GLA_V_W, GLA_QK_W), 1), GLA_DK))
    i_ref = (CHUNK - 1 - CHUNK // 2) if reverse else CHUNK // 2
    i_last = 0 if reverse else CHUNK - 1

    QI, KD, QE, VB, KI_BD, V_BD, DEC = ([] for _ in range(7))
    for c in range(nc):
        rows = slice(c * CHUNK, (c + 1) * CHUNK)
        b = b_ref[rows, :]
        b_mid = b[i_ref:i_ref + 1, :]
        b_last = b[i_last:i_last + 1, :]
        q = qk_ref[rows, 0:GLA_QK_W].astype(F32) * (GLA_DK ** -0.5)
        k = qk_ref[rows, GLA_QK_W:2 * GLA_QK_W].astype(F32)
        v = v_ref[rows, :].astype(F32)
        QI.append((q * jnp.exp(b - b_mid)).astype(BF16))
        KD.append((k * jnp.exp(b_last - b)).astype(BF16))
        QE.append((q * jnp.exp(b)).astype(BF16))
        VB.append(v.astype(BF16))
        KI_BD.append(_blockdiag(k * jnp.exp(b_mid - b), k_masks))
        V_BD.append(_blockdiag(v, v_masks))
        DEC.append(jnp.exp(b_last))
    A = [jnp.where(causal, _dot_nt(qi, kb), 0.0) for qi, kb in zip(QI, KI_BD)]
    O_IN = [_dot(a.astype(BF16), vb) for a, vb in zip(A, V_BD)]
    KV = [jnp.where(st_mask, _dot_tn(vb, kd), 0.0) for vb, kd in zip(VB, KD)]
    for ci in range(nc):
        c = (nc - 1 - ci) if reverse else ci
        rows = slice(c * CHUNK, (c + 1) * CHUNK)
        st = st_ref[...]
        o = O_IN[c] + _dot_nt(QE[c], st.astype(BF16))
        st_ref[...] = st * DEC[c] + KV[c]
        if final:
            acc_ref[rows, :] = of_ref[rows, :] + o
        else:
            o_ref[rows, :] = o

    if final:
        o = acc_ref[...]
        parts = []
        for h in range(GLA_HEADS):
            oh = o[:, h * GLA_DV:(h + 1) * GLA_DV]
            ms = jnp.mean(oh * oh, axis=-1, keepdims=True)
            parts.append(oh * lax.rsqrt(ms + HEAD_NORM_EPS))
        on = jnp.concatenate(parts, axis=1) * ng_ref[...]
        og = og_ref[...].astype(F32)
        o_ref[...] = on * (og * _sigmoid(og))


def _gla(p, ps, wa2p, ba, batch, seq_len, reverse, o_fwd=None, norm_g=None):
    final = o_fwd is not None
    tb = min(GLA_SEQ_TILE, seq_len)
    nt = seq_len // tb
    n = batch * seq_len

    def rowblk(b, j):
        return b * nt + ((nt - 1 - j) if reverse else j)

    in_specs = [
        pl.BlockSpec((tb, 512), lambda b, j: (rowblk(b, j), COL_QK // 512)),
        pl.BlockSpec((tb, 512), lambda b, j: (rowblk(b, j), COL_V // 512)),
        pl.BlockSpec((tb, 256), lambda b, j: (rowblk(b, j), (COL_GS - P_MAIN_COLS) // 256)),
        pl.BlockSpec((256, GLA_QK_W), lambda b, j: (0, 0)),
        pl.BlockSpec((1, GLA_QK_W), lambda b, j: (0, 0)),
    ]
    args = [p, p, ps, wa2p, ba]
    scratch = [pltpu.VMEM((GLA_V_W, GLA_QK_W), F32), pltpu.VMEM((tb, GLA_QK_W), F32)]
    if final:
        in_specs += [
            pl.BlockSpec((tb, GLA_V_W), lambda b, j: (rowblk(b, j), 0)),
            pl.BlockSpec((tb, 512), lambda b, j: (rowblk(b, j), COL_OG // 512)),
            pl.BlockSpec((1, GLA_V_W), lambda b, j: (0, 0)),
        ]
        args += [o_fwd, p, norm_g]
        scratch.append(pltpu.VMEM((tb, GLA_V_W), F32))
    kern = functools.partial(_gla_kernel, reverse=reverse, tb=tb, final=final)
    return pl.pallas_call(
        kern,
        grid=(batch, nt),
        in_specs=in_specs,
        out_specs=pl.BlockSpec((tb, GLA_V_W), lambda b, j: (rowblk(b, j), 0)),
        out_shape=jax.ShapeDtypeStruct((n, GLA_V_W), F32),
        scratch_shapes=scratch,
        compiler_params=pltpu.CompilerParams(
            dimension_semantics=("arbitrary", "arbitrary"),
            vmem_limit_bytes=VMEM_LIMIT),
        name="gla_bwd" if reverse else "gla_fwd",
    )(*args)


def _rwkv_kernel(*refs, reverse, tb, final):
    if final:
        (r_ref, k_ref, v_ref, rs_ref, w0_ref, w2_ref, a0_ref, a2_ref, kk_ref, ka_ref,
         yf_ref, g2_ref, rk_ref, lnw_ref, lnb_ref,
         o_ref, p_ref, cs_s, lw_s, kk_s, ba_s, k2_s, reff_s, oloc_s, gm_s, hm_s, y_s) = refs
    else:
        (r_ref, k_ref, v_ref, rs_ref, w0_ref, w2_ref, a0_ref, a2_ref, kk_ref, ka_ref,
         o_ref, p_ref, cs_s, lw_s, kk_s, ba_s, k2_s, reff_s, oloc_s, gm_s, hm_s) = refs
        y_s = o_ref
    j = pl.program_id(1)
    nc = tb // CHUNK
    G = RWKV_GROUP
    L = G * RWKV_HEAD
    ngrp = RWKV_W // L

    @pl.when(j == 0)
    def _():
        p_ref[...] = jnp.zeros_like(p_ref)

    r = r_ref[...].astype(F32)
    k = k_ref[...].astype(F32)
    lora_in = rs_ref[:, 0:RWKV_DECAY_LORA + RWKV_AAA_LORA]
    w_pre = w0_ref[...] + _dot_x3(jnp.tanh(lora_in), w2_ref[...])
    logw = -jnp.exp(-_softplus(-w_pre) - 0.5)
    a_lr = _sigmoid(a0_ref[...] + _dot(lora_in.astype(BF16), a2_ref[...]))
    hr = _div(_iota2((RWKV_W, RWKV_W), 0), RWKV_HEAD)
    hc = _div(_iota2((RWKV_W, RWKV_W), 1), RWKV_HEAD)
    head_ones = jnp.where(hr == hc, 1.0, 0.0).astype(BF16)
    head_sum = lambda t: _dot(t.astype(BF16), head_ones)
    kk = k * kk_ref[...]
    ssq = head_sum(kk * kk)
    kk = kk / jnp.maximum(jnp.sqrt(ssq), 1e-12)
    k2 = k * (1.0 + (a_lr - 1.0) * ka_ref[...])
    if final:
        bonus = head_sum(r * k2 * rk_ref[...]) * v_ref[...].astype(F32)
        gl = rs_ref[:, RWKV_DECAY_LORA + RWKV_AAA_LORA:]
        g_gate = _dot(_sigmoid(gl).astype(BF16), g2_ref[...])

    cs_s[...] = _chunk_cumsum(logw, reverse)
    lw_s[...] = logw
    kk_s[...] = kk
    ba_s[...] = kk * a_lr
    k2_s[...] = k2
    i_last = 0 if reverse else CHUNK - 1

    lane_h = _div(_iota2((CHUNK, L), 1), RWKV_HEAD)
    h_masks = [lane_h == h for h in range(G)]
    t_idx = _iota2((CHUNK, L), 0)
    s_idx = _mod(_iota2((CHUNK, L), 1), CHUNK)
    strict = (s_idx > t_idx) if reverse else (s_idx < t_idx)
    incl = (s_idx >= t_idx) if reverse else (s_idx <= t_idx)
    eye_p = jnp.where(s_idx == t_idx, 1.0, 0.0)
    bd_mask = _div(_iota2((L, L), 0), RWKV_HEAD) == _div(_iota2((L, L), 1), RWKV_HEAD)
    eye_l = jnp.where(_iota2((L, L), 0) == _iota2((L, L), 1), 1.0, 0.0)

    def blockdiag(z):
        return _blockdiag(z, h_masks)

    chains = [(c, gi) for c in range(nc) for gi in range(ngrp)]
    rows_of = lambda c: slice(c * CHUNK, (c + 1) * CHUNK)
    lanes_of = lambda gi: slice(gi * L, (gi + 1) * L)
    AT, BT, KT, RT, BD, KD, GL, V = ([] for _ in range(8))
    for c in range(nc):
        rows = rows_of(c)
        cs = cs_s[rows, :]
        cs_tot = cs[i_last:i_last + 1, :]
        g_inv = jnp.exp(-cs)
        g_rem = jnp.exp(cs_tot - cs)
        at_c = (-kk_s[rows, :]) * jnp.exp(cs - lw_s[rows, :])
        bt_c = ba_s[rows, :] * g_inv
        kt_c = k2_s[rows, :] * g_inv
        rt_c = r_ref[rows, :].astype(F32) * jnp.exp(cs)
        bd_c = ba_s[rows, :] * g_rem
        kd_c = k2_s[rows, :] * g_rem
        gl_c = jnp.exp(cs_tot)
        for gi in range(ngrp):
            lanes = lanes_of(gi)
            AT.append(at_c[:, lanes])
            BT.append(bt_c[:, lanes])
            KT.append(kt_c[:, lanes])
            RT.append(rt_c[:, lanes])
            BD.append(bd_c[:, lanes])
            KD.append(kd_c[:, lanes])
            GL.append(gl_c[:, lanes])
            V.append(v_ref[rows, lanes].astype(F32))
    bf = lambda t: t.astype(BF16)
    cat0 = lambda a, b: jnp.concatenate([a, b], axis=0)
    cat1 = lambda a, b: jnp.concatenate([a, b], axis=1)
    X = [bf(cat0(a, r_)) for a, r_ in zip(AT, RT)]
    G1 = [_dot_nt(x, blockdiag(b)) for x, b in zip(X, BT)]
    G2 = [_dot_nt(x, blockdiag(k_)) for x, k_ in zip(X, KT)]
    LAB = [jnp.where(strict, g[0:CHUNK], 0.0) for g in G1]
    MRB = [jnp.where(incl, g[CHUNK:2 * CHUNK], 0.0) for g in G1]
    LAK = [jnp.where(strict, g[0:CHUNK], 0.0) for g in G2]
    MRK = [jnp.where(incl, g[CHUNK:2 * CHUNK], 0.0) for g in G2]
    TM = [eye_p + l for l in LAB]
    XP = [_dot(bf(l), blockdiag(l)) for l in LAB]
    for _ in range(4):
        Y = [_dot(bf(cat0(t, x)), blockdiag(x)) for t, x in zip(TM, XP)]
        TM = [t + y[0:CHUNK] for t, y in zip(TM, Y)]
        XP = [y[CHUNK:2 * CHUNK] for y in Y]
    TM = [t + _dot(bf(t), blockdiag(x)) for t, x in zip(TM, XP)]
    VBD = [blockdiag(v) for v in V]
    LMV = [_dot(bf(cat0(l, m)), vb) for l, m, vb in zip(LAK, MRK, VBD)]
    LAKV = [t[0:CHUNK] for t in LMV]
    MV = [t[CHUNK:2 * CHUNK] for t in LMV]
    WU =[_dot(bf(t), cat1(blockdiag(a), blockdiag(lv))) for t, a, lv in zip(TM, AT, LAKV)]
    WA = [wu[:, 0:L] for wu in WU]
    UV = [wu[:, L:2 * L] for wu in WU]
    RO = [_dot(bf(m), cat1(blockdiag(wa), blockdiag(uv))) for m, wa, uv in zip(MRB, WA, UV)]
    PW = 2 * RWKV_HEAD
    zpair = jnp.zeros((PW, PW), F32)

    def pair_tn(bd, wa, uv):
        blocks = []
        for q in range(L // PW):
            pl_ = slice(q * PW, (q + 1) * PW)
            blocks.append(_dot_tn(bf(bd[:, pl_]), bf(cat1(wa[:, pl_], uv[:, pl_]))))
        rows_g = [cat1(*[blocks[q][:, 0:PW] if q2 == q else zpair for q2 in range(L // PW)])
                  for q in range(L // PW)]
        rows_h = [cat1(*[blocks[q][:, PW:2 * PW] if q2 == q else zpair for q2 in range(L // PW)])
                  for q in range(L // PW)]
        return cat1(jnp.concatenate(rows_g, axis=0), jnp.concatenate(rows_h, axis=0))

    if final:
        GH = [pair_tn(bd, wa, uv) for bd, wa, uv in zip(BD, WA, UV)]
    else:
        GH = [_dot_tn(bf(bd), bf(cat1(wa, uv))) for bd, wa, uv in zip(BD, WA, UV)]
    KV = [_dot_tn(bf(kd), bf(v)) for kd, v in zip(KD, V)]
    for n_, (c, gi) in enumerate(chains):
        rows, lanes = rows_of(c), lanes_of(gi)
        reff_s[rows, lanes] = RT[n_] + RO[n_][:, 0:L]
        oloc_s[rows, lanes] = RO[n_][:, L:2 * L] + MV[n_]
        gm_s[c, gi] = bf(eye_l * GL[n_] + jnp.where(bd_mask, GH[n_][:, 0:L], 0.0))
        hm_s[c, gi] = jnp.where(bd_mask, GH[n_][:, L:2 * L] + KV[n_], 0.0)

    for ci in range(nc):
        c = (nc - 1 - ci) if reverse else ci
        rows = slice(c * CHUNK, (c + 1) * CHUNK)
        for gi in range(ngrp):
            lanes = slice(gi * L, (gi + 1) * L)
            pst = p_ref[gi].astype(BF16)
            lhs = jnp.concatenate([reff_s[rows, lanes].astype(BF16), gm_s[c, gi]], axis=0)
            yp = _dot(lhs, pst)
            y_s[rows, lanes] = yp[0:CHUNK] + oloc_s[rows, lanes]
            p_ref[gi] = yp[CHUNK:CHUNK + L] + hm_s[c, gi]

    if final:
        y = y_s[...] + yf_ref[...]
        inv_n = 1.0 / RWKV_HEAD
        mu = head_sum(y) * inv_n
        yc = y - mu
        var = head_sum(yc * yc) * inv_n
        yn = yc * lax.rsqrt(var + RWKV_GN_EPS) * lnw_ref[...] + lnb_ref[...]
        o_ref[...] = (yn + bonus) * g_gate


def _rwkv(p, ps, w0, w2p, a0, a2p, k_k, k_a, batch, seq_len, reverse,
          y_fwd=None, g2=None, r_k=None, ln_w=None, ln_b=None):
    final = y_fwd is not None
    tb = min(RWKV_SEQ_TILE, seq_len)
    nt = seq_len // tb
    n = batch * seq_len
    L = RWKV_GROUP * RWKV_HEAD
    ngrp = RWKV_W // L
    lora = RWKV_DECAY_LORA + RWKV_AAA_LORA

    def rowblk(b, j):
        return b * nt + ((nt - 1 - j) if reverse else j)

    vec = pl.BlockSpec((1, RWKV_W), lambda b, j: (0, 0))
    in_specs = [
        pl.BlockSpec((tb, 512), lambda b, j: (rowblk(b, j), COL_R // 512)),
        pl.BlockSpec((tb, 512), lambda b, j: (rowblk(b, j), COL_RK // 512)),
        pl.BlockSpec((tb, 512), lambda b, j: (rowblk(b, j), COL_RV // 512)),
        pl.BlockSpec((tb, 256), lambda b, j: (rowblk(b, j), (COL_RS - P_MAIN_COLS) // 256)),
        vec,
        pl.BlockSpec((lora, RWKV_W), lambda b, j: (0, 0)),
        vec,
        pl.BlockSpec((lora, RWKV_W), lambda b, j: (0, 0)),
        vec, vec,
    ]
    args = [p, p, p, ps, w0, w2p, a0, a2p, k_k, k_a]
    nc = tb // CHUNK
    scratch = ([pltpu.VMEM((ngrp, L, L), F32)] + [pltpu.VMEM((tb, RWKV_W), F32)] * 7
               + [pltpu.VMEM((nc, ngrp, L, L), BF16), pltpu.VMEM((nc, ngrp, L, L), F32)])
    if final:
        in_specs += [
            pl.BlockSpec((tb, RWKV_W), lambda b, j: (rowblk(b, j), 0)),
            pl.BlockSpec((RWKV_GATE_LORA, RWKV_W), lambda b, j: (0, 0)),
            vec, vec, vec,
        ]
        args += [y_fwd, g2, r_k, ln_w, ln_b]
        scratch.append(pltpu.VMEM((tb, RWKV_W), F32))
    kern = functools.partial(_rwkv_kernel, reverse=reverse, tb=tb, final=final)
    return pl.pallas_call(
        kern,
        grid=(batch, nt),
        in_specs=in_specs,
        out_specs=pl.BlockSpec((tb, RWKV_W), lambda b, j: (rowblk(b, j), 0)),
        out_shape=jax.ShapeDtypeStruct((n, RWKV_W), F32),
        scratch_shapes=scratch,
        compiler_params=pltpu.CompilerParams(
            dimension_semantics=("arbitrary", "arbitrary"),
            vmem_limit_bytes=VMEM_LIMIT),
        name="rwkv_bwd" if reverse else "rwkv_fwd",
    )(*args)


def _merge_kernel(x_ref, oa_ref, ob_ref, ga_ref, gb_ref, wa_ref, wb_ref, wo_ref, o_ref):
    ya = _dot(oa_ref[...].astype(BF16), wa_ref[...])
    yb = _dot(ob_ref[...].astype(BF16), wb_ref[...])
    merged = (_sigmoid(ga_ref[...].astype(F32)) * ya
              + _sigmoid(gb_ref[...].astype(F32)) * yb)
    o_ref[...] = x_ref[...] + _dot(merged.astype(BF16), wo_ref[...])


def _merge(x2, oa, ob, p, gla_proj, rwkv_proj, w_out, seq_len):
    n = x2.shape[0]
    tm = min(ROW_TILE, seq_len)
    return pl.pallas_call(
        _merge_kernel,
        grid=(n // tm,),
        in_specs=[
            pl.BlockSpec((tm, D_MODEL), lambda i: (i, 0)),
            pl.BlockSpec((tm, GLA_V_W), lambda i: (i, 0)),
            pl.BlockSpec((tm, RWKV_W), lambda i: (i, 0)),
            pl.BlockSpec((tm, D_MODEL), lambda i: (i, COL_GA // D_MODEL)),
            pl.BlockSpec((tm, D_MODEL), lambda i: (i, COL_GB // D_MODEL)),
            pl.BlockSpec((GLA_V_W, D_MODEL), lambda i: (0, 0)),
            pl.BlockSpec((RWKV_W, D_MODEL), lambda i: (0, 0)),
            pl.BlockSpec((D_MODEL, D_MODEL), lambda i: (0, 0)),
        ],
        out_specs=pl.BlockSpec((tm, D_MODEL), lambda i: (i, 0)),
        out_shape=jax.ShapeDtypeStruct((n, D_MODEL), F32),
        compiler_params=pltpu.CompilerParams(
            dimension_semantics=("arbitrary",),
            vmem_limit_bytes=VMEM_LIMIT),
        name="merge",
    )(x2, oa, ob, p, p, gla_proj, rwkv_proj, w_out)


def _ffn_kernel(xp_ref, x_ref, xn_ref, g2_ref, wu_ref, cw_ref, cb_ref, wd_ref, gf_ref, o_ref,
                h_ref, u_ref, *, tm, tiles_per_seq):
    i = pl.program_id(0)
    pos = lax.rem(i, tiles_per_seq)
    keep_prev = jnp.where(pos == 0, 0.0, 1.0)
    keep_next = jnp.where(pos == tiles_per_seq - 1, 0.0, 1.0)
    g2 = g2_ref[...]
    h_ref[0:HALO, :] = _rmsnorm(xp_ref[...] * keep_prev, g2).astype(BF16)
    h_ref[HALO:HALO + tm, :] = _rmsnorm(x_ref[...], g2).astype(BF16)
    h_ref[HALO + tm:HALO + tm + HALO, :] = _rmsnorm(xn_ref[...] * keep_next, g2).astype(BF16)

    up = lambda c: _dot(h_ref[...], wu_ref[c])
    u_ref[0] = up(0)
    for c in range(FF_NCHUNK):
        slot = c % 2
        if c + 1 < FF_NCHUNK:
            u_ref[1 - slot] = up(c + 1)
        cw = cw_ref[c]
        uc = (cw[0:1, :] * u_ref[slot, HALO - 1:HALO - 1 + tm, :]
              + cw[1:2, :] * u_ref[slot, HALO:HALO + tm, :]
              + cw[2:3, :] * u_ref[slot, HALO + 1:HALO + 1 + tm, :]
              + cb_ref[c])
        ug = uc[:, 0:FF_CHUNK]
        act = (ug * _sigmoid(ug)) * uc[:, FF_CHUNK:2 * FF_CHUNK]
        down = _dot(act.astype(BF16), wd_ref[c])
        acc = down if c == 0 else acc + down
    o_ref[...] = _rmsnorm(x_ref[...] + acc, gf_ref[...])


def _ffn(x1, g2, wu_p, cw_p, cb_p, wd_p, gf, seq_len):
    n = x1.shape[0]
    tm = min(FFN_ROW_TILE, seq_len)
    tiles_per_seq = seq_len // tm
    hb = tm // HALO
    nhb = n // HALO
    kern = functools.partial(_ffn_kernel, tm=tm, tiles_per_seq=tiles_per_seq)
    return pl.pallas_call(
        kern,
        grid=(n // tm,),
        in_specs=[
            pl.BlockSpec((HALO, D_MODEL), lambda i: (jnp.maximum(i * hb - 1, 0), 0)),
            pl.BlockSpec((tm, D_MODEL), lambda i: (i, 0)),
            pl.BlockSpec((HALO, D_MODEL), lambda i: (jnp.minimum((i + 1) * hb, nhb - 1), 0)),
            pl.BlockSpec((1, D_MODEL), lambda i: (0, 0)),
            pl.BlockSpec((FF_NCHUNK, D_MODEL, 2 * FF_CHUNK), lambda i: (0, 0, 0),
                         pipeline_mode=pl.Buffered(1)),
            pl.BlockSpec((FF_NCHUNK, 3, 2 * FF_CHUNK), lambda i: (0, 0, 0)),
            pl.BlockSpec((FF_NCHUNK, 1, 2 * FF_CHUNK), lambda i: (0, 0, 0)),
            pl.BlockSpec((FF_NCHUNK, FF_CHUNK, D_MODEL), lambda i: (0, 0, 0),
                         pipeline_mode=pl.Buffered(1)),
            pl.BlockSpec((1, D_MODEL), lambda i: (0, 0)),
        ],
        out_specs=pl.BlockSpec((tm, D_MODEL), lambda i: (i, 0)),
        out_shape=jax.ShapeDtypeStruct((n, D_MODEL), F32),
        scratch_shapes=[
            pltpu.VMEM((tm + 2 * HALO, D_MODEL), BF16),
            pltpu.VMEM((2, tm + 2 * HALO, 2 * FF_CHUNK), F32),
        ],
        compiler_params=pltpu.CompilerParams(
            dimension_semantics=("arbitrary",),
            vmem_limit_bytes=VMEM_LIMIT),
        name="ffn",
    )(x1, x1, x1, g2, wu_p, cw_p, cb_p, wd_p, gf)


def _pack_w_in(w):
    gla_w = 2 * GLA_QK_W + 2 * GLA_V_W + 2 * GLA_GATE_RANK
    r0 = gla_w
    g0 = r0 + 3 * RWKV_W + RWKV_DECAY_LORA + RWKV_AAA_LORA + RWKV_GATE_LORA
    pad = jnp.zeros((w.shape[0], 256 - 2 * GLA_GATE_RANK), w.dtype)
    return jnp.concatenate([
        w[:, g0:g0 + 2 * D_MODEL],
        w[:, 0:2 * GLA_QK_W + 2 * GLA_V_W],
        w[:, r0:r0 + 3 * RWKV_W + 256],
        w[:, 2 * GLA_QK_W + 2 * GLA_V_W:gla_w],
        pad], axis=1)


def _pack_mu(mu):
    z = jnp.zeros((COL_R,), mu.dtype)
    z2 = jnp.zeros((P_COLS - COL_GS,), mu.dtype)
    return jnp.concatenate([z, mu, z2])[None, :]


def _pack_ff_cols(t):
    pad = [(0, 0)] * (t.ndim - 1) + [(0, FF_PAD - D_FF)]
    g = jnp.pad(t[..., :D_FF], pad)
    v = jnp.pad(t[..., D_FF:], pad)
    lead = t.shape[:-1]
    g = g.reshape(lead + (FF_NCHUNK, FF_CHUNK))
    v = v.reshape(lead + (FF_NCHUNK, FF_CHUNK))
    gv = jnp.concatenate([g, v], axis=-1)
    return jnp.moveaxis(gv, -2, 0)


def kernel(x, norm1_g, w_in, gla_wa2_f, gla_ba_f, gla_wa2_b, gla_ba_b, gla_norm_g, gla_proj, rwkv_mu_prev, rwkv_mu_next, rwkv_w0_f, rwkv_w2_f, rwkv_w0_b, rwkv_w2_b, rwkv_a0, rwkv_a2, rwkv_g2, rwkv_k_k, rwkv_k_a, rwkv_r_k, rwkv_ln_w, rwkv_ln_b, rwkv_proj, w_out, norm2_g, ffn_up, ffn_conv_w, ffn_conv_b, ffn_down, norm_f_g):
    batch, seq_len, d = x.shape
    assert w_in.shape[0] == 1 and d == D_MODEL and seq_len % CHUNK == 0
    x2 = x.reshape(batch * seq_len, d)
    row = lambda t: t.reshape(1, -1)
    for l in range(1):
        w_packed = _pack_w_in(w_in[l]).astype(BF16)
        wa2_f = jnp.pad(gla_wa2_f[l], ((0, 256 - GLA_GATE_RANK), (0, 0)))
        wa2_b = jnp.pad(gla_wa2_b[l], ((GLA_GATE_RANK, 256 - 2 * GLA_GATE_RANK), (0, 0)))
        zl = jnp.zeros((RWKV_AAA_LORA, RWKV_W), F32)
        w2_f = jnp.concatenate([rwkv_w2_f[l], zl], axis=0)
        w2_b = jnp.concatenate([rwkv_w2_b[l], zl], axis=0)
        a2p = jnp.concatenate([jnp.zeros((RWKV_DECAY_LORA, RWKV_W), F32), rwkv_a2[l]], axis=0).astype(BF16)

        p, ps = _inproj(x2, row(norm1_g[l]), w_packed, _pack_mu(rwkv_mu_prev[l]),
                        _pack_mu(rwkv_mu_next[l]), seq_len)

        o_f = _gla(p, ps, wa2_f, row(gla_ba_f[l]), batch, seq_len, reverse=False)
        oa = _gla(p, ps, wa2_b, row(gla_ba_b[l]), batch, seq_len, reverse=True,
                  o_fwd=o_f, norm_g=row(gla_norm_g[l]))

        rw_common = (row(rwkv_a0[l]), a2p, row(rwkv_k_k[l]), row(rwkv_k_a[l]))
        y_f = _rwkv(p, ps, row(rwkv_w0_f[l]), w2_f, *rw_common, batch, seq_len, reverse=False)
        ob = _rwkv(p, ps, row(rwkv_w0_b[l]), w2_b, *rw_common, batch, seq_len, reverse=True,
                   y_fwd=y_f, g2=rwkv_g2[l].astype(BF16), r_k=row(rwkv_r_k[l]),
                   ln_w=row(rwkv_ln_w[l]), ln_b=row(rwkv_ln_b[l]))

        x1 = _merge(x2, oa, ob, p, gla_proj[l].astype(BF16), rwkv_proj[l].astype(BF16),
                    w_out[l].astype(BF16), seq_len)

        wu_p = _pack_ff_cols(ffn_up[l]).astype(BF16)
        cw_p = _pack_ff_cols(ffn_conv_w[l])
        cb_p = _pack_ff_cols(ffn_conv_b[l][None, :])
        wd_p = jnp.pad(ffn_down[l], ((0, FF_PAD - D_FF), (0, 0))).reshape(
            FF_NCHUNK, FF_CHUNK, D_MODEL).astype(BF16)
        x2 = _ffn(x1, row(norm2_g[l]), wu_p, cw_p, cb_p, wd_p, row(norm_f_g), seq_len)
    return x2.reshape(batch, seq_len, d)
```
